```python
import math
import jax, jax.numpy as jnp
from jax import lax
import numpy as np

D_MODEL = 1024
BATCH = 8
SEQ = 4096
DEPTH = 1

GRID_W = 64
CTX_LEN = 256
NA_HEADS = 8
NA_HEAD_DIM = 64
NA_WIN_H = 8
NA_WIN_W = 16
RET_HEADS = 4
RET_QK_DIM = 128
RET_V_DIM = 256
RET_CHUNK = 128
D_FF = int(math.ceil(8 * D_MODEL / 3 / 128)) * 128
CONV_W = 3
ROPE_BASE = 10000.0
EPS = 1e-6

NA_W = NA_HEADS * NA_HEAD_DIM
RET_QK_W = RET_HEADS * RET_QK_DIM
RET_V_W = RET_HEADS * RET_V_DIM
IN_SPLITS = [NA_W, NA_W, NA_W, RET_QK_W, RET_QK_W, RET_V_W, RET_V_W, D_MODEL, D_MODEL]
IN_COLS = sum(IN_SPLITS)
IN_OFFSETS = [int(o) for o in np.cumsum(IN_SPLITS)[:-1]]

kernel_name = "hybrid_natten_retention_convffn_dit"


def _rmsnorm(x, g):
    xf = x.astype(jnp.float32)
    y = xf * lax.rsqrt(jnp.mean(xf * xf, axis=-1, keepdims=True) + EPS)
    return y.astype(x.dtype) * g


def _modulate(h, shift, scale):
    return h * (1.0 + scale) + shift


def _heads(t, n_heads):
    b, n, _ = t.shape
    return t.reshape(b, n, n_heads, -1).transpose(0, 2, 1, 3)


def _merge_heads(t):
    b, h, n, d = t.shape
    return t.transpose(0, 2, 1, 3).reshape(b, n, h * d)


def _rope_axis(t, pos):
    nf = t.shape[-1] // 2
    inv = ROPE_BASE ** (-jnp.arange(nf, dtype=jnp.float32) / nf)
    ang = pos.astype(jnp.float32)[:, None] * inv[None, :]
    cos = jnp.cos(ang).astype(t.dtype)
    sin = jnp.sin(ang).astype(t.dtype)
    t1, t2 = t[..., :nf], t[..., nf:]
    return jnp.concatenate([t1 * cos - t2 * sin, t1 * sin + t2 * cos], axis=-1)


def _rope2d(t, rows, cols):
    half = t.shape[-1] // 2
    return jnp.concatenate([_rope_axis(t[..., :half], rows), _rope_axis(t[..., half:], cols)], axis=-1)


def _na_latent(q, k, v, kc, vc, rpb):
    b, h, n_rows, w, hd = q.shape
    kh = min(NA_WIN_H, n_rows)
    kw = NA_WIN_W
    cols = np.arange(w)
    c_start = np.clip(cols - kw // 2, 0, w - kw)
    cidx = c_start[:, None] + np.arange(kw)[None, :]
    dc = cidx - cols[:, None] + (NA_WIN_W - 1)
    qs = q * (hd ** -0.5)

    def row_block(r):
        r_start = jnp.clip(r - kh // 2, 0, n_rows - kh)
        kb = lax.dynamic_slice_in_dim(k, r_start, kh, axis=2)
        vb = lax.dynamic_slice_in_dim(v, r_start, kh, axis=2)
        k_win = kb[:, :, :, cidx]
        v_win = vb[:, :, :, cidx]
        q_row = lax.dynamic_index_in_dim(qs, r, axis=2, keepdims=False)
        dr = r_start + jnp.arange(kh) - r + (NA_WIN_H - 1)
        bias = rpb[:, dr[None, :, None], dc[:, None, :]]
        s_loc = jnp.einsum('bhqd,bhiqjd->bhqij', q_row, k_win) + bias[None]
        s_loc = s_loc.reshape(b, h, w, kh * kw)
        s_ctx = jnp.einsum('bhqd,bhcd->bhqc', q_row, kc)
        p = jax.nn.softmax(jnp.concatenate([s_loc, s_ctx], axis=-1).astype(jnp.float32), axis=-1).astype(v.dtype)
        p_loc = p[..., :kh * kw].reshape(b, h, w, kh, kw)
        p_ctx = p[..., kh * kw:]
        return (jnp.einsum('bhqij,bhiqjd->bhqd', p_loc, v_win)
                + jnp.einsum('bhqc,bhcd->bhqd', p_ctx, vc))

    out = lax.map(row_block, jnp.arange(n_rows))
    return out.transpose(1, 2, 0, 3, 4)


def _na_context(qc, kc, vc):
    s = jnp.einsum('bhqd,bhkd->bhqk', qc * (qc.shape[-1] ** -0.5), kc)
    p = jax.nn.softmax(s.astype(jnp.float32), axis=-1).astype(vc.dtype)
    return jnp.einsum('bhqk,bhkd->bhqd', p, vc)


def _ret_scan(q, k, v, log_g, s0, strict):
    b, h, n, dk = q.shape
    dv = v.shape[-1]
    L = RET_CHUNK
    nc = n // L
    lg = log_g.astype(jnp.float32)
    pos = jnp.arange(L, dtype=jnp.float32)
    diff = pos[:, None] - pos[None, :]
    keep = (diff > 0) if strict else (diff >= 0)
    dmat = jnp.where(keep[None], jnp.exp(lg[:, None, None] * jnp.maximum(diff, 0.0)[None]), 0.0).astype(q.dtype)
    xi = jnp.exp(lg[:, None] * (pos + 1.0)[None]).astype(q.dtype)[None, :, :, None]
    zeta = jnp.exp(lg[:, None] * (L - 1.0 - pos)[None]).astype(q.dtype)[None, :, :, None]
    g_chunk = jnp.exp(lg * L).astype(q.dtype)[None, :, None, None]

    def to_chunks(t):
        return t.reshape(b, h, nc, L, t.shape[-1]).transpose(2, 0, 1, 3, 4)

    def step(state, qkv):
        qc, kc, vc = qkv
        scores = jnp.einsum('bhld,bhmd->bhlm', qc, kc) * dmat
        out = (jnp.einsum('bhlm,bhmv->bhlv', scores, vc)
               + jnp.einsum('bhld,bhdv->bhlv', qc, state) * xi)
        state = g_chunk * state + jnp.einsum('bhmd,bhmv->bhdv', kc * zeta, vc)
        return state, out

    s_fin, outs = lax.scan(step, s0, (to_chunks(q), to_chunks(k), to_chunks(v)))
    return outs.transpose(1, 2, 0, 3, 4).reshape(b, h, n, dv), s_fin


def _bi_retention(q, k, v, lg_f, lg_b, s_f, s_b):
    o_f, sf = _ret_scan(q, k, v, lg_f, s_f, False)
    o_b, sb = _ret_scan(jnp.flip(q, 2), jnp.flip(k, 2), jnp.flip(v, 2), lg_b, s_b, True)
    return o_f + jnp.flip(o_b, 2), sf, sb


def _ret_output(o, g):
    of = o.astype(jnp.float32)
    mu = jnp.mean(of, axis=-1, keepdims=True)
    var = jnp.mean(jnp.square(of - mu), axis=-1, keepdims=True)
    on = ((of - mu) * lax.rsqrt(var + EPS)).astype(o.dtype)
    return _merge_heads(on) * jax.nn.silu(g)


def _merge(o_na, o_ret, gate_na, gate_ret, w_proj_na, w_proj_ret, w_out):
    merged = jax.nn.sigmoid(gate_na) * (o_na @ w_proj_na) + jax.nn.sigmoid(gate_ret) * (o_ret @ w_proj_ret)
    return merged @ w_out


def _conv_ffn(h, w_up, conv_w, conv_b, w_down):
    u = h @ w_up
    n = u.shape[1]
    pad = CONV_W // 2
    up = jnp.pad(u, ((0, 0), (pad, pad), (0, 0)))
    u = sum(up[:, i:i + n] * conv_w[i] for i in range(CONV_W)) + conv_b
    a, val = jnp.split(u, 2, axis=-1)
    return (jax.nn.silu(a) * val) @ w_down


def setup_inputs(seed: int = 0) -> dict:
    key = jax.random.key(seed)
    ks = jax.random.split(key, 20)

    def nrm(k, shape, s):
        return jax.random.normal(k, shape, jnp.float32) * s

    base_decay = jnp.asarray(np.log(1.0 - 2.0 ** (-5.0 - np.arange(RET_HEADS))), jnp.float32)
    return {
        "x": nrm(ks[0], (BATCH, SEQ, D_MODEL), 1.0),
        "c": nrm(ks[1], (BATCH, D_MODEL), 1.0),
        "ctx": nrm(ks[2], (BATCH, CTX_LEN, D_MODEL), 1.0),
        "c_ctx": nrm(ks[3], (D_MODEL,), 1.0),
        "w_ada": nrm(ks[4], (DEPTH, D_MODEL, 6 * D_MODEL), 0.5 * D_MODEL ** -0.5),
        "b_ada": nrm(ks[5], (DEPTH, 6 * D_MODEL), 0.01),
        "norm1_g": 1.0 + nrm(ks[6], (DEPTH, D_MODEL), 0.01),
        "w_in": nrm(ks[7], (DEPTH, D_MODEL, IN_COLS), D_MODEL ** -0.5),
        "na_rpb": nrm(ks[8], (DEPTH, NA_HEADS, 2 * NA_WIN_H - 1, 2 * NA_WIN_W - 1), 0.1),
        "ret_log_decay_fwd": base_decay[None] * (1.0 + nrm(ks[9], (DEPTH, RET_HEADS), 0.05)),
        "ret_log_decay_bwd": base_decay[None] * (1.0 + nrm(ks[10], (DEPTH, RET_HEADS), 0.05)),
        "w_proj_na": nrm(ks[11], (DEPTH, NA_W, D_MODEL), NA_W ** -0.5),
        "w_proj_ret": nrm(ks[12], (DEPTH, RET_V_W, D_MODEL), RET_V_W ** -0.5),
        "w_out": nrm(ks[13], (DEPTH, D_MODEL, D_MODEL), D_MODEL ** -0.5),
        "norm2_g": 1.0 + nrm(ks[14], (DEPTH, D_MODEL), 0.01),
        "w_up": nrm(ks[15], (DEPTH, D_MODEL, 2 * D_FF), D_MODEL ** -0.5),
        "conv_w": nrm(ks[16], (DEPTH, CONV_W, 2 * D_FF), CONV_W ** -0.5),
        "conv_b": nrm(ks[17], (DEPTH, 2 * D_FF), 0.01),
        "w_down": nrm(ks[18], (DEPTH, D_FF, D_MODEL), D_FF ** -0.5),
        "final_norm_g": 1.0 + nrm(ks[19], (D_MODEL,), 0.01),
    }


def reference(x, c, ctx, c_ctx, w_ada, b_ada, norm1_g, w_in, na_rpb, ret_log_decay_fwd,
              ret_log_decay_bwd, w_proj_na, w_proj_ret, w_out, norm2_g, w_up, conv_w, conv_b,
              w_down, final_norm_g):
    b, n, _ = x.shape
    n_rows = n // GRID_W
    t = jnp.arange(n)
    rows = t // GRID_W
    cols = t % GRID_W
    c_silu = jax.nn.silu(c)[:, None, :]
    cctx_silu = jax.nn.silu(c_ctx)[None, :]

    for l in range(DEPTH):
        last = l == DEPTH - 1
        sh1, sc1, g1, sh2, sc2, g2 = jnp.split(c_silu @ w_ada[l] + b_ada[l], 6, axis=-1)
        csh1, csc1, cg1, csh2, csc2, cg2 = jnp.split(cctx_silu @ w_ada[l] + b_ada[l], 6, axis=-1)

        hx = _modulate(_rmsnorm(x, norm1_g[l]), sh1, sc1)
        hc = _modulate(_rmsnorm(ctx, norm1_g[l]), csh1, csc1)
        (xq, xk, xv, xrq, xrk, xrv, xrg, xga, xgb) = jnp.split(hx @ w_in[l], IN_OFFSETS, axis=-1)
        (cq, ck, cv, crq, crk, crv, crg, cga, cgb) = jnp.split(hc @ w_in[l], IN_OFFSETS, axis=-1)

        kc_na, vc_na = _heads(ck, NA_HEADS), _heads(cv, NA_HEADS)
        grid = lambda tt: _heads(tt, NA_HEADS).reshape(b, NA_HEADS, n_rows, GRID_W, NA_HEAD_DIM)
        o_na = _na_latent(grid(xq), grid(xk), grid(xv), kc_na, vc_na, na_rpb[l])
        o_na = _merge_heads(o_na.reshape(b, NA_HEADS, n, NA_HEAD_DIM))

        rk_scale = RET_QK_DIM ** -0.5
        qc_r, kc_r, vc_r = _heads(crq, RET_HEADS), _heads(crk, RET_HEADS) * rk_scale, _heads(crv, RET_HEADS)
        s_zero = jnp.zeros((b, RET_HEADS, RET_QK_DIM, RET_V_DIM), x.dtype)
        o_ret_c, s_f, s_b = _bi_retention(qc_r, kc_r, vc_r, ret_log_decay_fwd[l], ret_log_decay_bwd[l], s_zero, s_zero)
        q_r = _rope2d(_heads(xrq, RET_HEADS), rows, cols)
        k_r = _rope2d(_heads(xrk, RET_HEADS), rows, cols) * rk_scale
        o_ret_x, _, _ = _bi_retention(q_r, k_r, _heads(xrv, RET_HEADS), ret_log_decay_fwd[l], ret_log_decay_bwd[l], s_f, s_b)
        o_ret = _ret_output(o_ret_x, xrg)

        x = x + g1 * _merge(o_na, o_ret, xga, xgb, w_proj_na[l], w_proj_ret[l], w_out[l])
        hx2 = _modulate(_rmsnorm(x, norm2_g[l]), sh2, sc2)
        x = x + g2 * _conv_ffn(hx2, w_up[l], conv_w[l], conv_b[l], w_down[l])

        if not last:
            o_na_c = _merge_heads(_na_context(_heads(cq, NA_HEADS), kc_na, vc_na))
            o_ret_c = _ret_output(o_ret_c, crg)
            ctx = ctx + cg1 * _merge(o_na_c, o_ret_c, cga, cgb, w_proj_na[l], w_proj_ret[l], w_out[l])
            hc2 = _modulate(_rmsnorm(ctx, norm2_g[l]), csh2, csc2)
            ctx = ctx + cg2 * _conv_ffn(hc2, w_up[l], conv_w[l], conv_b[l], w_down[l])

    return _rmsnorm(x, final_norm_g)
```

```python
import functools
import math

import numpy as np
import jax
import jax.numpy as jnp
from jax import lax
from jax.experimental import pallas as pl
from jax.experimental.pallas import tpu as pltpu

F32 = jnp.float32
BF16 = jnp.bfloat16

D_MODEL = 1024
GRID_W = 64
NA_HEADS = 8
NA_HEAD_DIM = 64
NA_WIN_H = 8
NA_WIN_W = 16
RET_HEADS = 4
RET_QK_DIM = 128
RET_V_DIM = 256
RET_CHUNK = 128
D_FF = int(math.ceil(8 * D_MODEL / 3 / 128)) * 128
CONV_W = 3
ROPE_BASE = 10000.0
EPS = 1e-6

NA_W = NA_HEADS * NA_HEAD_DIM
RET_QK_W = RET_HEADS * RET_QK_DIM
RET_V_W = RET_HEADS * RET_V_DIM
OFF_Q, OFF_K, OFF_V = 0, NA_W, 2 * NA_W
OFF_RQ = 3 * NA_W
OFF_RK = OFF_RQ + RET_QK_W
OFF_RV = OFF_RK + RET_QK_W
OFF_RG = OFF_RV + RET_V_W
OFF_GA = OFF_RG + RET_V_W
OFF_GB = OFF_GA + D_MODEL
IN_COLS = OFF_GB + D_MODEL

NA_SCALE = NA_HEAD_DIM ** -0.5
RK_SCALE = RET_QK_DIM ** -0.5
MASK_VALUE = -1e30

VMEM_LIMIT_BYTES = 56 * 1024 * 1024
LANES = 128
HALO = 16

TM_PROJ = 512
TM_FFN = 512
ADA_ROWS = 16
ADA_TN = 1536
FF_CHUNK = 256
PROJ_CHUNK = 512


def _cparams(n_axes):
    return pltpu.CompilerParams(dimension_semantics=("arbitrary",) * n_axes,
                                vmem_limit_bytes=VMEM_LIMIT_BYTES)


def _resident(shape):
    nd = len(shape)
    return pl.BlockSpec(shape, lambda *_: (0,) * nd, pipeline_mode=pl.Buffered(1))


def _rms_modulate(x, gain, shift, scale):
    y = x * lax.rsqrt(jnp.mean(x * x, axis=-1, keepdims=True) + EPS)
    return (y * gain) * (1.0 + scale) + shift


def _ada_kernel(c_ref, w_ref, b_ref, o_ref):
    s = jax.nn.silu(c_ref[...])
    o_ref[...] = jnp.dot(s.astype(BF16), w_ref[...], preferred_element_type=F32) + b_ref[...]


def _ada(cc, w_ada, b_ada):
    n_out = w_ada.shape[1]
    return pl.pallas_call(
        _ada_kernel,
        grid=(n_out // ADA_TN,),
        in_specs=[pl.BlockSpec((ADA_ROWS, D_MODEL), lambda j: (0, 0)),
                  pl.BlockSpec((D_MODEL, ADA_TN), lambda j: (0, j)),
                  pl.BlockSpec((1, ADA_TN), lambda j: (0, j))],
        out_specs=pl.BlockSpec((ADA_ROWS, ADA_TN), lambda j: (0, j)),
        out_shape=jax.ShapeDtypeStruct((ADA_ROWS, n_out), F32),
        compiler_params=_cparams(1),
        name="ada",
    )(cc, w_ada, b_ada)


def _inproj_kernel(x_ref, sh_ref, sc_ref, g_ref, w_ref, cs_ref, s1_ref, s2_ref,
                   qkv_ref, rqk_ref, rv_ref, rg_ref, gate_ref):
    h = _rms_modulate(x_ref[0], g_ref[...], sh_ref[0], sc_ref[0]).astype(BF16)

    def proj(c0):
        return jnp.dot(h, w_ref[:, c0:c0 + PROJ_CHUNK], preferred_element_type=F32)

    for j in range(3 * NA_W // PROJ_CHUNK):
        r = proj(OFF_Q + j * PROJ_CHUNK)
        if j * PROJ_CHUNK < NA_W:
            r = r * NA_SCALE
        qkv_ref[0, :, j * PROJ_CHUNK:(j + 1) * PROJ_CHUNK] = r.astype(BF16)

    cs, s1, s2 = cs_ref[...], s1_ref[...], s2_ref[...]
    quarter = RET_QK_DIM // 4
    for j in range(2 * RET_QK_W // PROJ_CHUNK):
        r = proj(OFF_RQ + j * PROJ_CHUNK)
        for hh in range(PROJ_CHUNK // RET_QK_DIM):
            t = r[:, hh * RET_QK_DIM:(hh + 1) * RET_QK_DIM]
            t = (t * cs + pltpu.roll(t, RET_QK_DIM - quarter, 1) * s1
                 + pltpu.roll(t, quarter, 1) * s2)
            if j * PROJ_CHUNK >= RET_QK_W:
                t = t * RK_SCALE
            c0 = j * PROJ_CHUNK + hh * RET_QK_DIM
            rqk_ref[0, :, c0:c0 + RET_QK_DIM] = t.astype(BF16)

    for j in range(RET_V_W // PROJ_CHUNK):
        rv_ref[0, :, j * PROJ_CHUNK:(j + 1) * PROJ_CHUNK] = proj(OFF_RV + j * PROJ_CHUNK).astype(BF16)
    for j in range(RET_V_W // PROJ_CHUNK):
        r = proj(OFF_RG + j * PROJ_CHUNK)
        rg_ref[0, :, j * PROJ_CHUNK:(j + 1) * PROJ_CHUNK] = jax.nn.silu(r).astype(BF16)
    for j in range(2 * D_MODEL // PROJ_CHUNK):
        r = proj(OFF_GA + j * PROJ_CHUNK)
        gate_ref[0, :, j * PROJ_CHUNK:(j + 1) * PROJ_CHUNK] = jax.nn.sigmoid(r).astype(BF16)


def _inproj(x, shift, scale, gain, w_in, cs, s1, s2, name):
    b, t, _ = x.shape
    tm = min(TM_PROJ, t)
    per_batch = shift.shape[0] == b
    mod_map = (lambda bi, i: (bi, 0, 0)) if per_batch else (lambda bi, i: (0, 0, 0))
    tok = lambda w: pl.BlockSpec((1, tm, w), lambda bi, i: (bi, i, 0))
    tab = pl.BlockSpec((tm, RET_QK_DIM), lambda bi, i: (i, 0))
    out_w = (3 * NA_W, 2 * RET_QK_W, RET_V_W, RET_V_W, 2 * D_MODEL)
    return pl.pallas_call(
        _inproj_kernel,
        grid=(b, t // tm),
        in_specs=[tok(D_MODEL),
                  pl.BlockSpec((1, 1, D_MODEL), mod_map),
                  pl.BlockSpec((1, 1, D_MODEL), mod_map),
                  _resident((1, D_MODEL)),
                  _resident((D_MODEL, IN_COLS)),
                  tab, tab, tab],
        out_specs=[tok(w) for w in out_w],
        out_shape=[jax.ShapeDtypeStruct((b, t, w), BF16) for w in out_w],
        compiler_params=_cparams(2),
        name=name,
    )(x, shift, scale, gain, w_in, cs, s1, s2)


def _na_kernel(q_ref, k_ref, v_ref, kc_ref, vc_ref, bias_ref, o_ref):
    n_rows = q_ref.shape[1] // GRID_W
    win = NA_WIN_H * GRID_W
    lane = lax.broadcasted_iota(jnp.int32, (GRID_W, LANES), 1)
    first = lane < NA_HEAD_DIM
    kc = kc_ref[0]
    vc = vc_ref[0]
    nt = (((1,), (1,)), ((), ()))

    def row(r, carry):
        rs = jnp.clip(r - NA_WIN_H // 2, 0, n_rows - NA_WIN_H)
        var = rs - r + (NA_WIN_H - 1)
        q = q_ref[0, pl.ds(pl.multiple_of(r * GRID_W, GRID_W), GRID_W), :]
        kw = k_ref[0, pl.ds(pl.multiple_of(rs * GRID_W, GRID_W), win), :]
        vw = v_ref[0, pl.ds(pl.multiple_of(rs * GRID_W, GRID_W), win), :]
        outs = []
        for hh in range(2):
            qm = jnp.where(first if hh == 0 else jnp.logical_not(first), q, jnp.zeros_like(q))
            s_loc = lax.dot_general(qm, kw, nt, preferred_element_type=F32) + bias_ref[hh, var]
            s_ctx = lax.dot_general(qm, kc, nt, preferred_element_type=F32)
            mx = jnp.maximum(jnp.max(s_loc, axis=-1, keepdims=True),
                             jnp.max(s_ctx, axis=-1, keepdims=True))
            p_loc = jnp.exp(s_loc - mx)
            p_ctx = jnp.exp(s_ctx - mx)
            den = jnp.sum(p_loc, axis=-1, keepdims=True) + jnp.sum(p_ctx, axis=-1, keepdims=True)
            o = (jnp.dot(p_loc.astype(BF16), vw, preferred_element_type=F32)
                 + jnp.dot(p_ctx.astype(BF16), vc, preferred_element_type=F32))
            outs.append(o / den)
        o = jnp.where(first, outs[0], outs[1])
        o_ref[0, pl.ds(pl.multiple_of(r * GRID_W, GRID_W), GRID_W), :] = o.astype(BF16)
        return carry

    lax.fori_loop(0, n_rows, row, 0)


def _na(qkv, ckv, bias):
    b, n, _ = qkv.shape
    ctx = ckv.shape[1]
    pairs = NA_W // LANES
    blk = lambda t, off: pl.BlockSpec((1, t, LANES), lambda bi, p: (bi, 0, off + p))
    return pl.pallas_call(
        _na_kernel,
        grid=(b, pairs),
        in_specs=[blk(n, 0), blk(n, pairs), blk(n, 2 * pairs),
                  blk(ctx, pairs), blk(ctx, 2 * pairs),
                  pl.BlockSpec((2, NA_WIN_H, GRID_W, NA_WIN_H * GRID_W), lambda bi, p: (p, 0, 0, 0))],
        out_specs=pl.BlockSpec((1, n, LANES), lambda bi, p: (bi, 0, p)),
        out_shape=jax.ShapeDtypeStruct((b, n, NA_W), BF16),
        compiler_params=_cparams(2),
        name="na",
    )(qkv, qkv, qkv, ckv, ckv, bias)


def _ret_kernel(lgf_ref, lgb_ref, q_ref, k_ref, v_ref, g_ref, ck_ref, cv_ref, o_ref,
                dmat, xi_f, xi_b, zeta_f, zeta_b, gl_f, gl_b, s_f, s_b, o_acc):
    L = RET_CHUNK
    hd = pl.program_id(1)
    lgf = lgf_ref[hd]
    lgb = lgb_ref[hd]
    n_chunks = q_ref.shape[1] // L
    c_chunks = ck_ref.shape[1] // L

    li = lax.broadcasted_iota(jnp.int32, (L, L), 0).astype(F32)
    mi = lax.broadcasted_iota(jnp.int32, (L, L), 1).astype(F32)
    diff = li - mi
    dmat[...] = jnp.where(diff >= 0.0, jnp.exp(lgf * jnp.maximum(diff, 0.0)),
                          jnp.exp(lgb * jnp.maximum(-diff, 0.0)))
    zeta_f[...] = jnp.exp(lgf * (L - 1.0 - li))
    zeta_b[...] = jnp.exp(lgb * li)
    lv = lax.broadcasted_iota(jnp.int32, (L, RET_V_DIM), 0).astype(F32)
    xi_f[...] = jnp.exp(lgf * (lv + 1.0))
    xi_b[...] = jnp.exp(lgb * (L - lv))
    gl_f[...] = jnp.exp(jnp.full((L, RET_V_DIM), L, F32) * lgf)
    gl_b[...] = jnp.exp(jnp.full((L, RET_V_DIM), L, F32) * lgb)

    tn = (((0,), (0,)), ((), ()))
    nt = (((1,), (1,)), ((), ()))

    def kv_update(state, gl, zeta, k, v):
        kz = (k.astype(F32) * zeta[...]).astype(BF16)
        state[...] = gl[...] * state[...] + lax.dot_general(kz, v, tn, preferred_element_type=F32)

    s_f[...] = jnp.zeros_like(s_f)
    s_b[...] = jnp.zeros_like(s_b)
    for c in range(c_chunks):
        kv_update(s_f, gl_f, zeta_f, ck_ref[0, c * L:(c + 1) * L, :], cv_ref[0, c * L:(c + 1) * L, :])
    for c in reversed(range(c_chunks)):
        kv_update(s_b, gl_b, zeta_b, ck_ref[0, c * L:(c + 1) * L, :], cv_ref[0, c * L:(c + 1) * L, :])

    def bwd(i, carry):
        rows = pl.ds(pl.multiple_of((n_chunks - 1 - i) * L, L), L)
        q = q_ref[0, rows, :]
        o_acc[rows, :] = jnp.dot(q, s_b[...].astype(BF16), preferred_element_type=F32) * xi_b[...]
        kv_update(s_b, gl_b, zeta_b, k_ref[0, rows, :], v_ref[0, rows, :])
        return carry

    lax.fori_loop(0, n_chunks, bwd, 0)

    def fwd(c, carry):
        rows = pl.ds(pl.multiple_of(c * L, L), L)
        q = q_ref[0, rows, :]
        k = k_ref[0, rows, :]
        v = v_ref[0, rows, :]
        a = lax.dot_general(q, k, nt, preferred_element_type=F32) * dmat[...]
        o = (jnp.dot(a.astype(BF16), v, preferred_element_type=F32)
             + jnp.dot(q, s_f[...].astype(BF16), preferred_element_type=F32) * xi_f[...]
             + o_acc[rows, :])
        kv_update(s_f, gl_f, zeta_f, k, v)
        mu = jnp.mean(o, axis=-1, keepdims=True)
        d = o - mu
        var = jnp.mean(d * d, axis=-1, keepdims=True)
        o_ref[0, rows, :] = (d * lax.rsqrt(var + EPS) * g_ref[0, rows, :].astype(F32)).astype(BF16)
        return carry

    lax.fori_loop(0, n_chunks, fwd, 0)


def _ret(lgf, lgb, rqk, rv, rg, crqk, crv):
    b, n, _ = rqk.shape
    ctx = crqk.shape[1]
    L = RET_CHUNK
    smem = pl.BlockSpec(memory_space=pltpu.SMEM)
    qk = lambda t, off: pl.BlockSpec((1, t, RET_QK_DIM), lambda bi, h: (bi, 0, off + h))
    vv = lambda t: pl.BlockSpec((1, t, RET_V_DIM), lambda bi, h: (bi, 0, h))
    vm = lambda r, c: pltpu.VMEM((r, c), F32)
    return pl.pallas_call(
        _ret_kernel,
        grid=(b, RET_HEADS),
        in_specs=[smem, smem, qk(n, 0), qk(n, RET_HEADS), vv(n), vv(n), qk(ctx, RET_HEADS), vv(ctx)],
        out_specs=vv(n),
        out_shape=jax.ShapeDtypeStruct((b, n, RET_V_W), BF16),
        scratch_shapes=[vm(L, L), vm(L, RET_V_DIM), vm(L, RET_V_DIM), vm(L, L), vm(L, L),
                        vm(L, RET_V_DIM), vm(L, RET_V_DIM),
                        vm(RET_QK_DIM, RET_V_DIM), vm(RET_QK_DIM, RET_V_DIM), vm(n, RET_V_DIM)],
        compiler_params=_cparams(2),
        name="ret",
    )(lgf, lgb, rqk, rqk, rv, rg, crqk, crv)


def _merge_kernel(x_ref, ona_ref, oret_ref, gate_ref, g1_ref, wna_ref, wret_ref, wout_ref, o_ref):
    a = jnp.dot(ona_ref[0], wna_ref[...], preferred_element_type=F32)
    r = jnp.dot(oret_ref[0], wret_ref[...], preferred_element_type=F32)
    merged = (gate_ref[0, :, :D_MODEL].astype(F32) * a
              + gate_ref[0, :, D_MODEL:].astype(F32) * r).astype(BF16)
    y = jnp.dot(merged, wout_ref[...], preferred_element_type=F32)
    o_ref[0] = x_ref[0] + g1_ref[0] * y


def _merge(x, o_na, o_ret, gates, g1, w_na, w_ret, w_out):
    b, n, _ = x.shape
    tm = TM_PROJ
    tok = lambda w: pl.BlockSpec((1, tm, w), lambda bi, i: (bi, i, 0))
    return pl.pallas_call(
        _merge_kernel,
        grid=(b, n // tm),
        in_specs=[tok(D_MODEL), tok(NA_W), tok(RET_V_W), tok(2 * D_MODEL),
                  pl.BlockSpec((1, 1, D_MODEL), lambda bi, i: (bi, 0, 0)),
                  _resident((NA_W, D_MODEL)), _resident((RET_V_W, D_MODEL)),
                  _resident((D_MODEL, D_MODEL))],
        out_specs=tok(D_MODEL),
        out_shape=jax.ShapeDtypeStruct((b, n, D_MODEL), F32),
        compiler_params=_cparams(2),
        name="merge",
    )(x, o_na, o_ret, gates, g1, w_na, w_ret, w_out)


def _ffn_kernel(x_ref, xp_ref, xn_ref, sh_ref, sc_ref, g2_ref, gn_ref, gf_ref,
                wup_ref, cw_ref, cb_ref, wdown_ref, o_ref):
    i = pl.program_id(1)
    last = pl.num_programs(1) - 1
    tm = x_ref.shape[1]
    rows = tm + 2 * HALO
    x = x_ref[0]
    xa = jnp.concatenate([xp_ref[0], x, xn_ref[0]], axis=0)
    h = _rms_modulate(xa, gn_ref[...], sh_ref[0], sc_ref[0])
    ridx = lax.broadcasted_iota(jnp.int32, (rows, 1), 0)
    valid = jnp.logical_and(jnp.logical_or(ridx >= HALO, i > 0),
                            jnp.logical_or(ridx < tm + HALO, i < last))
    h = jnp.where(valid, h, 0.0).astype(BF16)

    def conv(c0):
        u = jnp.dot(h, wup_ref[:, c0:c0 + FF_CHUNK], preferred_element_type=F32)
        w = cw_ref[:, c0:c0 + FF_CHUNK]
        y = (pltpu.roll(u, 1, 0) * w[0:1] + u * w[1:2] + pltpu.roll(u, rows - 1, 0) * w[2:3]
             + cb_ref[:, c0:c0 + FF_CHUNK])
        return y[HALO:HALO + tm]

    acc = jnp.zeros((tm, D_MODEL), F32)
    for j in range(D_FF // FF_CHUNK):
        a = conv(j * FF_CHUNK)
        val = conv(D_FF + j * FF_CHUNK)
        y = (jax.nn.silu(a) * val).astype(BF16)
        acc = acc + jnp.dot(y, wdown_ref[j * FF_CHUNK:(j + 1) * FF_CHUNK, :],
                            preferred_element_type=F32)
    z = x + g2_ref[0] * acc
    o_ref[0] = z * lax.rsqrt(jnp.mean(z * z, axis=-1, keepdims=True) + EPS) * gf_ref[...]


def _ffn(x, sh2, sc2, g2, gain2, gain_f, w_up, conv_w, conv_b, w_down):
    b, n, _ = x.shape
    tm = TM_FFN
    per = tm // HALO
    n_halo = n // HALO
    tok = pl.BlockSpec((1, tm, D_MODEL), lambda bi, i: (bi, i, 0))
    mod = pl.BlockSpec((1, 1, D_MODEL), lambda bi, i: (bi, 0, 0))
    prev = pl.BlockSpec((1, HALO, D_MODEL), lambda bi, i: (bi, jnp.maximum(i * per - 1, 0), 0))
    nxt = pl.BlockSpec((1, HALO, D_MODEL), lambda bi, i: (bi, jnp.minimum((i + 1) * per, n_halo - 1), 0))
    return pl.pallas_call(
        _ffn_kernel,
        grid=(b, n // tm),
        in_specs=[tok, prev, nxt, mod, mod, mod,
                  _resident((1, D_MODEL)), _resident((1, D_MODEL)),
                  _resident((D_MODEL, 2 * D_FF)), _resident((CONV_W, 2 * D_FF)),
                  _resident((1, 2 * D_FF)), _resident((D_FF, D_MODEL))],
        out_specs=tok,
        out_shape=jax.ShapeDtypeStruct((b, n, D_MODEL), F32),
        compiler_params=_cparams(2),
        name="ffn",
    )(x, x, x, sh2, sc2, g2, gain2, gain_f, w_up, conv_w, conv_b, w_down)


def _rope_tables(n):
    nf = RET_QK_DIM // 4
    inv = ROPE_BASE ** (-jnp.arange(nf, dtype=F32) / nf)
    t = jnp.arange(n)
    ang_r = (t // GRID_W).astype(F32)[:, None] * inv[None, :]
    ang_c = (t % GRID_W).astype(F32)[:, None] * inv[None, :]
    cos = jnp.concatenate([jnp.cos(ang_r)] * 2 + [jnp.cos(ang_c)] * 2, axis=-1)
    zero = jnp.zeros((n, nf), F32)
    s1 = jnp.concatenate([-jnp.sin(ang_r), zero, -jnp.sin(ang_c), zero], axis=-1)
    s2 = jnp.concatenate([zero, jnp.sin(ang_r), zero, jnp.sin(ang_c)], axis=-1)
    return cos, s1, s2


def _na_bias_table(rpb):
    w, kh, kw = GRID_W, NA_WIN_H, NA_WIN_W
    cols = np.arange(w)
    c_start = np.clip(cols - kw // 2, 0, w - kw)
    kcol = np.arange(w)
    valid = (kcol[None, :] >= c_start[:, None]) & (kcol[None, :] < c_start[:, None] + kw)
    dc = np.clip(kcol[None, :] - cols[:, None] + (kw - 1), 0, 2 * kw - 2)
    dr = np.arange(kh)[:, None] + np.arange(kh)[None, :]
    tab = rpb[:, dr[:, None, :, None], dc[None, :, None, :]]
    tab = jnp.where(valid[None, None, :, None, :], tab, MASK_VALUE)
    return tab.reshape(rpb.shape[0], kh, w, kh * w).astype(F32)


def kernel(x, c, ctx, c_ctx, w_ada, b_ada, norm1_g, w_in, na_rpb, ret_log_decay_fwd, ret_log_decay_bwd,
           w_proj_na, w_proj_ret, w_out, norm2_g, w_up, conv_w, conv_b, w_down, final_norm_g):
    b, n, d = x.shape
    n_ctx = ctx.shape[1]
    assert w_ada.shape[0] == 1, "single layer"
    row = lambda v: v.reshape(1, -1)

    cc = jnp.concatenate([c, c_ctx[None], jnp.zeros((ADA_ROWS - b - 1, d), F32)], axis=0)
    mod = _ada(cc, w_ada[0].astype(BF16), row(b_ada[0]))
    xmod = mod[:b].reshape(b, 1, 6, d)
    sh1, sc1, g1, sh2, sc2, g2 = (xmod[:, :, i] for i in range(6))
    cmod = mod[b:b + 1].reshape(1, 1, 6, d)
    csh1, csc1 = cmod[:, :, 0], cmod[:, :, 1]

    w_in_b = w_in[0].astype(BF16)
    cos, s1, s2 = _rope_tables(n)
    qkv, rqk, rv, rg, gates = _inproj(x, sh1, sc1, row(norm1_g[0]), w_in_b, cos, s1, s2, "inproj")
    ones = jnp.ones((n_ctx, RET_QK_DIM), F32)
    zeros = jnp.zeros((n_ctx, RET_QK_DIM), F32)
    cqkv, crqk, crv, _, _ = _inproj(ctx, csh1, csc1, row(norm1_g[0]), w_in_b, ones, zeros, zeros, "inproj_ctx")

    o_na = _na(qkv, cqkv, _na_bias_table(na_rpb[0]))
    o_ret = _ret(ret_log_decay_fwd[0], ret_log_decay_bwd[0], rqk, rv, rg, crqk, crv)

    x1 = _merge(x, o_na, o_ret, gates, g1, w_proj_na[0].astype(BF16), w_proj_ret[0].astype(BF16),
                w_out[0].astype(BF16))
    return _ffn(x1, sh2, sc2, g2, row(norm2_g[0]), row(final_norm_g), w_up[0].astype(BF16),
                conv_w[0], row(conv_b[0]), w_down[0].astype(BF16))
```

```python
import functools
import math

import numpy as np
import jax
import jax.numpy as jnp
from jax import lax
from jax.experimental import pallas as pl
from jax.experimental.pallas import tpu as pltpu

F32 = jnp.float32
BF16 = jnp.bfloat16

D_MODEL = 1024
GRID_W = 64
NA_HEADS = 8
NA_HEAD_DIM = 64
NA_WIN_H = 8
NA_WIN_W = 16
RET_HEADS = 4
RET_QK_DIM = 128
RET_V_DIM = 256
RET_CHUNK = 128
D_FF = int(math.ceil(8 * D_MODEL / 3 / 128)) * 128
CONV_W = 3
ROPE_BASE = 10000.0
EPS = 1e-6

NA_W = NA_HEADS * NA_HEAD_DIM
RET_QK_W = RET_HEADS * RET_QK_DIM
RET_V_W = RET_HEADS * RET_V_DIM
OFF_Q, OFF_K, OFF_V = 0, NA_W, 2 * NA_W
OFF_RQ = 3 * NA_W
OFF_RK = OFF_RQ + RET_QK_W
OFF_RV = OFF_RK + RET_QK_W
OFF_RG = OFF_RV + RET_V_W
OFF_GA = OFF_RG + RET_V_W
OFF_GB = OFF_GA + D_MODEL
IN_COLS = OFF_GB + D_MODEL

NA_SCALE = NA_HEAD_DIM ** -0.5
RK_SCALE = RET_QK_DIM ** -0.5
MASK_VALUE = -1e30

VMEM_LIMIT_BYTES = 56 * 1024 * 1024
LANES = 128
HALO = 16

TM_PROJ = 512
TM_FFN = 512
ADA_ROWS = 16
ADA_TN = 1536
FF_CHUNK = 256
PROJ_CHUNK = 512
NA_UNROLL = 8
RET_UNROLL = 4


def _cparams(n_axes):
    return pltpu.CompilerParams(dimension_semantics=("arbitrary",) * n_axes,
                                vmem_limit_bytes=VMEM_LIMIT_BYTES)


def _resident(shape):
    nd = len(shape)
    return pl.BlockSpec(shape, lambda *_: (0,) * nd, pipeline_mode=pl.Buffered(1))


def _rms_modulate(x, gain, shift, scale):
    y = x * lax.rsqrt(jnp.mean(x * x, axis=-1, keepdims=True) + EPS)
    return (y * gain) * (1.0 + scale) + shift


def _ada_kernel(c_ref, w_ref, b_ref, o_ref):
    s = jax.nn.silu(c_ref[...])
    o_ref[...] = jnp.dot(s.astype(BF16), w_ref[...], preferred_element_type=F32) + b_ref[...]


def _ada(cc, w_ada, b_ada):
    n_out = w_ada.shape[1]
    return pl.pallas_call(
        _ada_kernel,
        grid=(n_out // ADA_TN,),
        in_specs=[pl.BlockSpec((ADA_ROWS, D_MODEL), lambda j: (0, 0)),
                  pl.BlockSpec((D_MODEL, ADA_TN), lambda j: (0, j)),
                  pl.BlockSpec((1, ADA_TN), lambda j: (0, j))],
        out_specs=pl.BlockSpec((ADA_ROWS, ADA_TN), lambda j: (0, j)),
        out_shape=jax.ShapeDtypeStruct((ADA_ROWS, n_out), F32),
        compiler_params=_cparams(1),
        name="ada",
    )(cc, w_ada, b_ada)


def _inproj_kernel(x_ref, sh_ref, sc_ref, g_ref, w_ref, cs_ref, s1_ref, s2_ref,
                   qkv_ref, rqk_ref, rv_ref, rg_ref, gate_ref):
    h = _rms_modulate(x_ref[0], g_ref[...], sh_ref[0], sc_ref[0]).astype(BF16)

    def proj(c0):
        return jnp.dot(h, w_ref[:, c0:c0 + PROJ_CHUNK], preferred_element_type=F32)

    for j in range(3 * NA_W // PROJ_CHUNK):
        r = proj(OFF_Q + j * PROJ_CHUNK)
        if j * PROJ_CHUNK < NA_W:
            r = r * NA_SCALE
        qkv_ref[0, :, j * PROJ_CHUNK:(j + 1) * PROJ_CHUNK] = r.astype(BF16)

    cs, s1, s2 = cs_ref[...], s1_ref[...], s2_ref[...]
    quarter = RET_QK_DIM // 4
    for j in range(2 * RET_QK_W // PROJ_CHUNK):
        r = proj(OFF_RQ + j * PROJ_CHUNK)
        for hh in range(PROJ_CHUNK // RET_QK_DIM):
            t = r[:, hh * RET_QK_DIM:(hh + 1) * RET_QK_DIM]
            t = (t * cs + pltpu.roll(t, RET_QK_DIM - quarter, 1) * s1
                 + pltpu.roll(t, quarter, 1) * s2)
            if j * PROJ_CHUNK >= RET_QK_W:
                t = t * RK_SCALE
            c0 = j * PROJ_CHUNK + hh * RET_QK_DIM
            rqk_ref[0, :, c0:c0 + RET_QK_DIM] = t.astype(BF16)

    for j in range(RET_V_W // PROJ_CHUNK):
        rv_ref[0, :, j * PROJ_CHUNK:(j + 1) * PROJ_CHUNK] = proj(OFF_RV + j * PROJ_CHUNK).astype(BF16)
    for j in range(RET_V_W // PROJ_CHUNK):
        r = proj(OFF_RG + j * PROJ_CHUNK)
        rg_ref[0, :, j * PROJ_CHUNK:(j + 1) * PROJ_CHUNK] = jax.nn.silu(r).astype(BF16)
    for j in range(2 * D_MODEL // PROJ_CHUNK):
        r = proj(OFF_GA + j * PROJ_CHUNK)
        gate_ref[0, :, j * PROJ_CHUNK:(j + 1) * PROJ_CHUNK] = jax.nn.sigmoid(r).astype(BF16)


def _inproj(x, shift, scale, gain, w_in, cs, s1, s2, name):
    b, t, _ = x.shape
    tm = min(TM_PROJ, t)
    per_batch = shift.shape[0] == b
    mod_map = (lambda bi, i: (bi, 0, 0)) if per_batch else (lambda bi, i: (0, 0, 0))
    tok = lambda w: pl.BlockSpec((1, tm, w), lambda bi, i: (bi, i, 0))
    tab = pl.BlockSpec((tm, RET_QK_DIM), lambda bi, i: (i, 0))
    out_w = (3 * NA_W, 2 * RET_QK_W, RET_V_W, RET_V_W, 2 * D_MODEL)
    return pl.pallas_call(
        _inproj_kernel,
        grid=(b, t // tm),
        in_specs=[tok(D_MODEL),
                  pl.BlockSpec((1, 1, D_MODEL), mod_map),
                  pl.BlockSpec((1, 1, D_MODEL), mod_map),
                  _resident((1, D_MODEL)),
                  _resident((D_MODEL, IN_COLS)),
                  tab, tab, tab],
        out_specs=[tok(w) for w in out_w],
        out_shape=[jax.ShapeDtypeStruct((b, t, w), BF16) for w in out_w],
        compiler_params=_cparams(2),
        name=name,
    )(x, shift, scale, gain, w_in, cs, s1, s2)


def _na_kernel(q_ref, k_ref, v_ref, kc_ref, vc_ref, bias_ref, o_ref, vext, vcext, m_scr, p_scr):
    n_rows = q_ref.shape[1] // GRID_W
    win = NA_WIN_H * GRID_W
    first = lax.broadcasted_iota(jnp.int32, (GRID_W, LANES), 1) < NA_HEAD_DIM
    nt = (((1,), (1,)), ((), ()))

    vext[:, :LANES] = v_ref[0]
    vext[:, LANES:] = jnp.ones((vext.shape[0], LANES), BF16)
    vcext[:, :LANES] = vc_ref[0]
    vcext[:, LANES:] = jnp.ones((vcext.shape[0], LANES), BF16)
    kc = kc_ref[0]

    def window(r):
        rs = jnp.clip(r - NA_WIN_H // 2, 0, n_rows - NA_WIN_H)
        return (pl.ds(pl.multiple_of(r * GRID_W, GRID_W), GRID_W),
                pl.ds(pl.multiple_of(rs * GRID_W, GRID_W), win), rs - r + (NA_WIN_H - 1))

    def scores(r):
        rows, wrows, var = window(r)
        q = q_ref[0, rows, :]
        kw = k_ref[0, wrows, :]
        zero = jnp.zeros_like(q)
        qs = jnp.concatenate([jnp.where(first, q, zero), jnp.where(first, zero, q)], axis=0)
        s_loc = lax.dot_general(qs, kw, nt, preferred_element_type=F32)
        s_ctx = lax.dot_general(qs, kc, nt, preferred_element_type=F32)
        res = [(s_loc[hh * GRID_W:(hh + 1) * GRID_W] + bias_ref[hh, var],
                s_ctx[hh * GRID_W:(hh + 1) * GRID_W]) for hh in range(2)]
        return rows, res

    def max_pass(r, carry):
        rows, res = scores(r)
        for hh, (s_loc, s_ctx) in enumerate(res):
            mx = jnp.maximum(jnp.max(s_loc, axis=-1, keepdims=True),
                             jnp.max(s_ctx, axis=-1, keepdims=True))
            m_scr[hh, rows, :] = jnp.broadcast_to(mx, (GRID_W, LANES))
        return carry

    lax.fori_loop(0, n_rows, max_pass, 0, unroll=NA_UNROLL)

    def prob_pass(r, carry):
        rows, res = scores(r)
        for hh, (s_loc, s_ctx) in enumerate(res):
            mx = m_scr[hh, rows, :]
            s = jnp.concatenate([s_loc, s_ctx], axis=1)
            for j in range(s.shape[1] // LANES):
                p_scr[hh, rows, j * LANES:(j + 1) * LANES] = jnp.exp(
                    s[:, j * LANES:(j + 1) * LANES] - mx).astype(BF16)
        return carry

    lax.fori_loop(0, n_rows, prob_pass, 0, unroll=NA_UNROLL)

    def out_pass(r, carry):
        rows, wrows, _ = window(r)
        p = jnp.concatenate([p_scr[0, rows, :], p_scr[1, rows, :]], axis=0)
        o = (jnp.dot(p[:, :win], vext[wrows, :], preferred_element_type=F32)
             + jnp.dot(p[:, win:], vcext[...], preferred_element_type=F32))
        num = jnp.where(first, o[:GRID_W, :LANES], o[GRID_W:, :LANES])
        den = jnp.where(first, o[:GRID_W, LANES:], o[GRID_W:, LANES:])
        o_ref[0, rows, :] = (num / den).astype(BF16)
        return carry

    lax.fori_loop(0, n_rows, out_pass, 0, unroll=NA_UNROLL)


def _na(qkv, ckv, bias):
    b, n, _ = qkv.shape
    ctx = ckv.shape[1]
    pairs = NA_W // LANES
    win = NA_WIN_H * GRID_W
    blk = lambda t, off: pl.BlockSpec((1, t, LANES), lambda bi, p: (bi, 0, off + p))
    return pl.pallas_call(
        _na_kernel,
        grid=(b, pairs),
        in_specs=[blk(n, 0), blk(n, pairs), blk(n, 2 * pairs),
                  blk(ctx, pairs), blk(ctx, 2 * pairs),
                  pl.BlockSpec((2, NA_WIN_H, GRID_W, win), lambda bi, p: (p, 0, 0, 0))],
        out_specs=pl.BlockSpec((1, n, LANES), lambda bi, p: (bi, 0, p)),
        out_shape=jax.ShapeDtypeStruct((b, n, NA_W), BF16),
        scratch_shapes=[pltpu.VMEM((n, 2 * LANES), BF16), pltpu.VMEM((ctx, 2 * LANES), BF16),
                        pltpu.VMEM((2, n, LANES), F32), pltpu.VMEM((2, n, win + ctx), BF16)],
        compiler_params=_cparams(2),
        name="na",
    )(qkv, qkv, qkv, ckv, ckv, bias)


def _ret_kernel(lgf_ref, lgb_ref, q_ref, k_ref, v_ref, g_ref, ck_ref, cv_ref, o_ref,
                dmat, xi_f, xi_b, zeta_f, zeta_b, gl_f, gl_b, sf_scr, sb_scr, p_scr, o_acc):
    L = RET_CHUNK
    hd = pl.program_id(1)
    lgf = lgf_ref[hd]
    lgb = lgb_ref[hd]
    n_chunks = q_ref.shape[1] // L
    c_chunks = ck_ref.shape[1] // L

    li = lax.broadcasted_iota(jnp.int32, (L, L), 0).astype(F32)
    mi = lax.broadcasted_iota(jnp.int32, (L, L), 1).astype(F32)
    diff = li - mi
    dmat[...] = jnp.where(diff >= 0.0, jnp.exp(lgf * jnp.maximum(diff, 0.0)),
                          jnp.exp(lgb * jnp.maximum(-diff, 0.0)))
    zeta_f[...] = jnp.exp(lgf * (L - 1.0 - li))
    zeta_b[...] = jnp.exp(lgb * li)
    lv = lax.broadcasted_iota(jnp.int32, (L, RET_V_DIM), 0).astype(F32)
    xi_f[...] = jnp.exp(lgf * (lv + 1.0))
    xi_b[...] = jnp.exp(lgb * (L - lv))
    gl_f[...] = jnp.exp(jnp.full((L, RET_V_DIM), L, F32) * lgf)
    gl_b[...] = jnp.exp(jnp.full((L, RET_V_DIM), L, F32) * lgb)

    tn = (((0,), (0,)), ((), ()))
    nt = (((1,), (1,)), ((), ()))

    def advance(state, gl, zeta, k, v):
        kz = (k.astype(F32) * zeta[...]).astype(BF16)
        return gl[...] * state + lax.dot_general(kz, v, tn, preferred_element_type=F32)

    sf = jnp.zeros((RET_QK_DIM, RET_V_DIM), F32)
    sb = jnp.zeros((RET_QK_DIM, RET_V_DIM), F32)
    for c in range(c_chunks):
        sf = advance(sf, gl_f, zeta_f, ck_ref[0, c * L:(c + 1) * L, :], cv_ref[0, c * L:(c + 1) * L, :])
    for c in reversed(range(c_chunks)):
        sb = advance(sb, gl_b, zeta_b, ck_ref[0, c * L:(c + 1) * L, :], cv_ref[0, c * L:(c + 1) * L, :])

    chunk = lambda c: pl.ds(pl.multiple_of(c * L, L), L)

    def states(i, carry):
        sf, sb = carry
        cb = n_chunks - 1 - i
        sf_scr[i] = sf.astype(BF16)
        sb_scr[cb] = sb.astype(BF16)
        return (advance(sf, gl_f, zeta_f, k_ref[0, chunk(i), :], v_ref[0, chunk(i), :]),
                advance(sb, gl_b, zeta_b, k_ref[0, chunk(cb), :], v_ref[0, chunk(cb), :]))

    lax.fori_loop(0, n_chunks, states, (sf, sb), unroll=RET_UNROLL)

    def probs(c, carry):
        rows = chunk(c)
        a = lax.dot_general(q_ref[0, rows, :], k_ref[0, rows, :], nt, preferred_element_type=F32)
        p_scr[rows, :] = (a * dmat[...]).astype(BF16)
        return carry

    lax.fori_loop(0, n_chunks, probs, 0, unroll=RET_UNROLL)

    def outs(c, carry):
        rows = chunk(c)
        q = q_ref[0, rows, :]
        o_acc[rows, :] = (jnp.dot(p_scr[rows, :], v_ref[0, rows, :], preferred_element_type=F32)
                          + jnp.dot(q, sf_scr[c], preferred_element_type=F32) * xi_f[...]
                          + jnp.dot(q, sb_scr[c], preferred_element_type=F32) * xi_b[...])
        return carry

    lax.fori_loop(0, n_chunks, outs, 0, unroll=RET_UNROLL)

    def norm(c, carry):
        rows = chunk(c)
        o = o_acc[rows, :]
        mu = jnp.mean(o, axis=-1, keepdims=True)
        d = o - mu
        var = jnp.mean(d * d, axis=-1, keepdims=True)
        o_ref[0, rows, :] = (d * lax.rsqrt(var + EPS) * g_ref[0, rows, :].astype(F32)).astype(BF16)
        return carry

    lax.fori_loop(0, n_chunks, norm, 0, unroll=RET_UNROLL)


def _ret(lgf, lgb, rqk, rv, rg, crqk, crv):
    b, n, _ = rqk.shape
    ctx = crqk.shape[1]
    L = RET_CHUNK
    smem = pl.BlockSpec(memory_space=pltpu.SMEM)
    qk = lambda t, off: pl.BlockSpec((1, t, RET_QK_DIM), lambda bi, h: (bi, 0, off + h))
    vv = lambda t: pl.BlockSpec((1, t, RET_V_DIM), lambda bi, h: (bi, 0, h))
    vm = lambda r, c: pltpu.VMEM((r, c), F32)
    st = pltpu.VMEM((n // L, RET_QK_DIM, RET_V_DIM), BF16)
    return pl.pallas_call(
        _ret_kernel,
        grid=(b, RET_HEADS),
        in_specs=[smem, smem, qk(n, 0), qk(n, RET_HEADS), vv(n), vv(n), qk(ctx, RET_HEADS), vv(ctx)],
        out_specs=vv(n),
        out_shape=jax.ShapeDtypeStruct((b, n, RET_V_W), BF16),
        scratch_shapes=[vm(L, L), vm(L, RET_V_DIM), vm(L, RET_V_DIM), vm(L, L), vm(L, L),
                        vm(L, RET_V_DIM), vm(L, RET_V_DIM), st, st,
                        pltpu.VMEM((n, L), BF16), vm(n, RET_V_DIM)],
        compiler_params=_cparams(2),
        name="ret",
    )(lgf, lgb, rqk, rqk, rv, rg, crqk, crv)


def _merge_kernel(x_ref, ona_ref, oret_ref, gate_ref, g1_ref, wna_ref, wret_ref, wout_ref, o_ref):
    a = jnp.dot(ona_ref[0], wna_ref[...], preferred_element_type=F32)
    r = jnp.dot(oret_ref[0], wret_ref[...], preferred_element_type=F32)
    merged = (gate_ref[0, :, :D_MODEL].astype(F32) * a
              + gate_ref[0, :, D_MODEL:].astype(F32) * r).astype(BF16)
    y = jnp.dot(merged, wout_ref[...], preferred_element_type=F32)
    o_ref[0] = x_ref[0] + g1_ref[0] * y


def _merge(x, o_na, o_ret, gates, g1, w_na, w_ret, w_out):
    b, n, _ = x.shape
    tm = TM_PROJ
    tok = lambda w: pl.BlockSpec((1, tm, w), lambda bi, i: (bi, i, 0))
    return pl.pallas_call(
        _merge_kernel,
        grid=(b, n // tm),
        in_specs=[tok(D_MODEL), tok(NA_W), tok(RET_V_W), tok(2 * D_MODEL),
                  pl.BlockSpec((1, 1, D_MODEL), lambda bi, i: (bi, 0, 0)),
                  _resident((NA_W, D_MODEL)), _resident((RET_V_W, D_MODEL)),
                  _resident((D_MODEL, D_MODEL))],
        out_specs=tok(D_MODEL),
        out_shape=jax.ShapeDtypeStruct((b, n, D_MODEL), F32),
        compiler_params=_cparams(2),
        name="merge",
    )(x, o_na, o_ret, gates, g1, w_na, w_ret, w_out)


def _ffn_kernel(x_ref, xp_ref, xn_ref, sh_ref, sc_ref, g2_ref, gn_ref, gf_ref,
                wup_ref, cw_ref, cb_ref, wdown_ref, o_ref):
    i = pl.program_id(1)
    last = pl.num_programs(1) - 1
    tm = x_ref.shape[1]
    rows = tm + 2 * HALO
    x = x_ref[0]
    xa = jnp.concatenate([xp_ref[0], x, xn_ref[0]], axis=0)
    h = _rms_modulate(xa, gn_ref[...], sh_ref[0], sc_ref[0])
    ridx = lax.broadcasted_iota(jnp.int32, (rows, 1), 0)
    valid = jnp.logical_and(jnp.logical_or(ridx >= HALO, i > 0),
                            jnp.logical_or(ridx < tm + HALO, i < last))
    h = jnp.where(valid, h, 0.0).astype(BF16)

    def conv(c0):
        u = jnp.dot(h, wup_ref[:, c0:c0 + FF_CHUNK], preferred_element_type=F32)
        w = cw_ref[:, c0:c0 + FF_CHUNK]
        y = (pltpu.roll(u, 1, 0) * w[0:1] + u * w[1:2] + pltpu.roll(u, rows - 1, 0) * w[2:3]
             + cb_ref[:, c0:c0 + FF_CHUNK])
        return y[HALO:HALO + tm]

    acc = jnp.zeros((tm, D_MODEL), F32)
    for j in range(D_FF // FF_CHUNK):
        a = conv(j * FF_CHUNK)
        val = conv(D_FF + j * FF_CHUNK)
        y = (jax.nn.silu(a) * val).astype(BF16)
        acc = acc + jnp.dot(y, wdown_ref[j * FF_CHUNK:(j + 1) * FF_CHUNK, :],
                            preferred_element_type=F32)
    z = x + g2_ref[0] * acc
    o_ref[0] = z * lax.rsqrt(jnp.mean(z * z, axis=-1, keepdims=True) + EPS) * gf_ref[...]


def _ffn(x, sh2, sc2, g2, gain2, gain_f, w_up, conv_w, conv_b, w_down):
    b, n, _ = x.shape
    tm = TM_FFN
    per = tm // HALO
    n_halo = n // HALO
    tok = pl.BlockSpec((1, tm, D_MODEL), lambda bi, i: (bi, i, 0))
    mod = pl.BlockSpec((1, 1, D_MODEL), lambda bi, i: (bi, 0, 0))
    prev = pl.BlockSpec((1, HALO, D_MODEL), lambda bi, i: (bi, jnp.maximum(i * per - 1, 0), 0))
    nxt = pl.BlockSpec((1, HALO, D_MODEL), lambda bi, i: (bi, jnp.minimum((i + 1) * per, n_halo - 1), 0))
    return pl.pallas_call(
        _ffn_kernel,
        grid=(b, n // tm),
        in_specs=[tok, prev, nxt, mod, mod, mod,
                  _resident((1, D_MODEL)), _resident((1, D_MODEL)),
                  _resident((D_MODEL, 2 * D_FF)), _resident((CONV_W, 2 * D_FF)),
                  _resident((1, 2 * D_FF)), _resident((D_FF, D_MODEL))],
        out_specs=tok,
        out_shape=jax.ShapeDtypeStruct((b, n, D_MODEL), F32),
        compiler_params=_cparams(2),
        name="ffn",
    )(x, x, x, sh2, sc2, g2, gain2, gain_f, w_up, conv_w, conv_b, w_down)


def _rope_tables(n):
    nf = RET_QK_DIM // 4
    inv = ROPE_BASE ** (-jnp.arange(nf, dtype=F32) / nf)
    t = jnp.arange(n)
    ang_r = (t // GRID_W).astype(F32)[:, None] * inv[None, :]
    ang_c = (t % GRID_W).astype(F32)[:, None] * inv[None, :]
    cos = jnp.concatenate([jnp.cos(ang_r)] * 2 + [jnp.cos(ang_c)] * 2, axis=-1)
    zero = jnp.zeros((n, nf), F32)
    s1 = jnp.concatenate([-jnp.sin(ang_r), zero, -jnp.sin(ang_c), zero], axis=-1)
    s2 = jnp.concatenate([zero, jnp.sin(ang_r), zero, jnp.sin(ang_c)], axis=-1)
    return cos, s1, s2


def _na_bias_table(rpb):
    w, kh, kw = GRID_W, NA_WIN_H, NA_WIN_W
    cols = np.arange(w)
    c_start = np.clip(cols - kw // 2, 0, w - kw)
    kcol = np.arange(w)
    valid = (kcol[None, :] >= c_start[:, None]) & (kcol[None, :] < c_start[:, None] + kw)
    dc = kcol[None, :] - cols[:, None] + (kw - 1)
    onehot = ((dc[:, :, None] == np.arange(2 * kw - 1)) & valid[:, :, None]).astype(np.float32)
    base = jnp.einsum('hrd,ckd->hrck', rpb, onehot, precision=lax.Precision.HIGHEST)
    base = base + np.where(valid, 0.0, MASK_VALUE).astype(np.float32)
    tab = jnp.stack([base[:, v:v + kh] for v in range(kh)], axis=1)
    return tab.transpose(0, 1, 3, 2, 4).reshape(rpb.shape[0], kh, w, kh * w)


def kernel(x, c, ctx, c_ctx, w_ada, b_ada, norm1_g, w_in, na_rpb, ret_log_decay_fwd, ret_log_decay_bwd,
           w_proj_na, w_proj_ret, w_out, norm2_g, w_up, conv_w, conv_b, w_down, final_norm_g):
    b, n, d = x.shape
    n_ctx = ctx.shape[1]
    assert w_ada.shape[0] == 1, "single layer"
    row = lambda v: v.reshape(1, -1)

    cc = jnp.concatenate([c, c_ctx[None], jnp.zeros((ADA_ROWS - b - 1, d), F32)], axis=0)
    mod = _ada(cc, w_ada[0].astype(BF16), row(b_ada[0]))
    xmod = mod[:b].reshape(b, 1, 6, d)
    sh1, sc1, g1, sh2, sc2, g2 = (xmod[:, :, i] for i in range(6))
    cmod = mod[b:b + 1].reshape(1, 1, 6, d)
    csh1, csc1 = cmod[:, :, 0], cmod[:, :, 1]

    w_in_b = w_in[0].astype(BF16)
    cos, s1, s2 = _rope_tables(n)
    qkv, rqk, rv, rg, gates = _inproj(x, sh1, sc1, row(norm1_g[0]), w_in_b, cos, s1, s2, "inproj")
    ones = jnp.ones((n_ctx, RET_QK_DIM), F32)
    zeros = jnp.zeros((n_ctx, RET_QK_DIM), F32)
    cqkv, crqk, crv, _, _ = _inproj(ctx, csh1, csc1, row(norm1_g[0]), w_in_b, ones, zeros, zeros, "inproj_ctx")

    o_na = _na(qkv, cqkv, _na_bias_table(na_rpb[0]))
    o_ret = _ret(ret_log_decay_fwd[0], ret_log_decay_bwd[0], rqk, rv, rg, crqk, crv)

    x1 = _merge(x, o_na, o_ret, gates, g1, w_proj_na[0].astype(BF16), w_proj_ret[0].astype(BF16),
                w_out[0].astype(BF16))
    return _ffn(x1, sh2, sc2, g2, row(norm2_g[0]), row(final_norm_g), w_up[0].astype(BF16),
                conv_w[0], row(conv_b[0]), w_down[0].astype(BF16))
```

```python
import functools
import math

import numpy as np
import jax
import jax.numpy as jnp
from jax import lax
from jax.experimental import pallas as pl
from jax.experimental.pallas import tpu as pltpu

F32 = jnp.float32
BF16 = jnp.bfloat16

D_MODEL = 1024
GRID_W = 64
NA_HEADS = 8
NA_HEAD_DIM = 64
NA_WIN_H = 8
NA_WIN_W = 16
RET_HEADS = 4
RET_QK_DIM = 128
RET_V_DIM = 256
RET_CHUNK = 128
D_FF = int(math.ceil(8 * D_MODEL / 3 / 128)) * 128
CONV_W = 3
ROPE_BASE = 10000.0
EPS = 1e-6

NA_W = NA_HEADS * NA_HEAD_DIM
RET_QK_W = RET_HEADS * RET_QK_DIM
RET_V_W = RET_HEADS * RET_V_DIM
OFF_Q, OFF_K, OFF_V = 0, NA_W, 2 * NA_W
OFF_RQ = 3 * NA_W
OFF_RK = OFF_RQ + RET_QK_W
OFF_RV = OFF_RK + RET_QK_W
OFF_RG = OFF_RV + RET_V_W
OFF_GA = OFF_RG + RET_V_W
OFF_GB = OFF_GA + D_MODEL
IN_COLS = OFF_GB + D_MODEL

NA_SCALE = NA_HEAD_DIM ** -0.5
RK_SCALE = RET_QK_DIM ** -0.5
MASK_VALUE = -1e30

VMEM_LIMIT_BYTES = 56 * 1024 * 1024
LANES = 128
HALO = 16

TM_PROJ = 512
TM_FFN = 512
ADA_ROWS = 16
ADA_TN = 1536
FF_CHUNK = 256
PROJ_CHUNK = 512
NA_UNROLL = 8
RET_UNROLL = 4


def _cparams(n_axes):
    return pltpu.CompilerParams(dimension_semantics=("arbitrary",) * n_axes,
                                vmem_limit_bytes=VMEM_LIMIT_BYTES)


def _resident(shape):
    nd = len(shape)
    return pl.BlockSpec(shape, lambda *_: (0,) * nd, pipeline_mode=pl.Buffered(1))


def _rms_modulate(x, gain, shift, scale):
    y = x * lax.rsqrt(jnp.mean(x * x, axis=-1, keepdims=True) + EPS)
    return (y * gain) * (1.0 + scale) + shift


def _ada_kernel(c_ref, w_ref, b_ref, o_ref):
    s = jax.nn.silu(c_ref[...])
    o_ref[...] = jnp.dot(s.astype(BF16), w_ref[...], preferred_element_type=F32) + b_ref[...]


def _ada(cc, w_ada, b_ada):
    n_out = w_ada.shape[1]
    return pl.pallas_call(
        _ada_kernel,
        grid=(n_out // ADA_TN,),
        in_specs=[pl.BlockSpec((ADA_ROWS, D_MODEL), lambda j: (0, 0)),
                  pl.BlockSpec((D_MODEL, ADA_TN), lambda j: (0, j)),
                  pl.BlockSpec((1, ADA_TN), lambda j: (0, j))],
        out_specs=pl.BlockSpec((ADA_ROWS, ADA_TN), lambda j: (0, j)),
        out_shape=jax.ShapeDtypeStruct((ADA_ROWS, n_out), F32),
        compiler_params=_cparams(1),
        name="ada",
    )(cc, w_ada, b_ada)


def _inproj_kernel(x_ref, sh_ref, sc_ref, g_ref, w_ref, cs_ref, s1_ref, s2_ref,
                   qkv_ref, rqk_ref, rv_ref, rg_ref, gate_ref):
    h = _rms_modulate(x_ref[0], g_ref[...], sh_ref[0], sc_ref[0]).astype(BF16)

    def proj(c0):
        return jnp.dot(h, w_ref[:, c0:c0 + PROJ_CHUNK], preferred_element_type=F32)

    for j in range(3 * NA_W // PROJ_CHUNK):
        r = proj(OFF_Q + j * PROJ_CHUNK)
        if j * PROJ_CHUNK < NA_W:
            r = r * NA_SCALE
        qkv_ref[0, :, j * PROJ_CHUNK:(j + 1) * PROJ_CHUNK] = r.astype(BF16)

    cs, s1, s2 = cs_ref[...], s1_ref[...], s2_ref[...]
    quarter = RET_QK_DIM // 4
    for j in range(2 * RET_QK_W // PROJ_CHUNK):
        r = proj(OFF_RQ + j * PROJ_CHUNK)
        for hh in range(PROJ_CHUNK // RET_QK_DIM):
            t = r[:, hh * RET_QK_DIM:(hh + 1) * RET_QK_DIM]
            t = (t * cs + pltpu.roll(t, RET_QK_DIM - quarter, 1) * s1
                 + pltpu.roll(t, quarter, 1) * s2)
            if j * PROJ_CHUNK >= RET_QK_W:
                t = t * RK_SCALE
            c0 = j * PROJ_CHUNK + hh * RET_QK_DIM
            rqk_ref[0, :, c0:c0 + RET_QK_DIM] = t.astype(BF16)

    for j in range(RET_V_W // PROJ_CHUNK):
        rv_ref[0, :, j * PROJ_CHUNK:(j + 1) * PROJ_CHUNK] = proj(OFF_RV + j * PROJ_CHUNK).astype(BF16)
    for j in range(RET_V_W // PROJ_CHUNK):
        r = proj(OFF_RG + j * PROJ_CHUNK)
        rg_ref[0, :, j * PROJ_CHUNK:(j + 1) * PROJ_CHUNK] = jax.nn.silu(r).astype(BF16)
    for j in range(2 * D_MODEL // PROJ_CHUNK):
        r = proj(OFF_GA + j * PROJ_CHUNK)
        gate_ref[0, :, j * PROJ_CHUNK:(j + 1) * PROJ_CHUNK] = jax.nn.sigmoid(r).astype(BF16)


def _inproj(x, shift, scale, gain, w_in, cs, s1, s2, name):
    b, t, _ = x.shape
    tm = min(TM_PROJ, t)
    per_batch = shift.shape[0] == b
    mod_map = (lambda bi, i: (bi, 0, 0)) if per_batch else (lambda bi, i: (0, 0, 0))
    tok = lambda w: pl.BlockSpec((1, tm, w), lambda bi, i: (bi, i, 0))
    tab = pl.BlockSpec((tm, RET_QK_DIM), lambda bi, i: (i, 0))
    out_w = (3 * NA_W, 2 * RET_QK_W, RET_V_W, RET_V_W, 2 * D_MODEL)
    return pl.pallas_call(
        _inproj_kernel,
        grid=(b, t // tm),
        in_specs=[tok(D_MODEL),
                  pl.BlockSpec((1, 1, D_MODEL), mod_map),
                  pl.BlockSpec((1, 1, D_MODEL), mod_map),
                  _resident((1, D_MODEL)),
                  _resident((D_MODEL, IN_COLS)),
                  tab, tab, tab],
        out_specs=[tok(w) for w in out_w],
        out_shape=[jax.ShapeDtypeStruct((b, t, w), BF16) for w in out_w],
        compiler_params=_cparams(2),
        name=name,
    )(x, shift, scale, gain, w_in, cs, s1, s2)


def _na_kernel(q_ref, k_ref, v_ref, kc_ref, vc_ref, bias_ref, o_ref, vext, vcext, m_scr, p_scr):
    n_rows = q_ref.shape[1] // GRID_W
    win = NA_WIN_H * GRID_W
    first = lax.broadcasted_iota(jnp.int32, (GRID_W, LANES), 1) < NA_HEAD_DIM
    nt = (((1,), (1,)), ((), ()))

    vext[:, :LANES] = v_ref[0]
    vext[:, LANES:] = jnp.ones((vext.shape[0], LANES), BF16)
    vcext[:, :LANES] = vc_ref[0]
    vcext[:, LANES:] = jnp.ones((vcext.shape[0], LANES), BF16)
    kc = kc_ref[0]

    def window(r):
        rs = jnp.clip(r - NA_WIN_H // 2, 0, n_rows - NA_WIN_H)
        return (pl.ds(pl.multiple_of(r * GRID_W, GRID_W), GRID_W),
                pl.ds(pl.multiple_of(rs * GRID_W, GRID_W), win), rs - r + (NA_WIN_H - 1))

    def scores(r):
        rows, wrows, var = window(r)
        q = q_ref[0, rows, :]
        kw = k_ref[0, wrows, :]
        zero = jnp.zeros_like(q)
        qs = jnp.concatenate([jnp.where(first, q, zero), jnp.where(first, zero, q)], axis=0)
        s_loc = lax.dot_general(qs, kw, nt, preferred_element_type=F32)
        s_ctx = lax.dot_general(qs, kc, nt, preferred_element_type=F32)
        res = [(s_loc[hh * GRID_W:(hh + 1) * GRID_W] + bias_ref[hh, var],
                s_ctx[hh * GRID_W:(hh + 1) * GRID_W]) for hh in range(2)]
        return rows, res

    def max_pass(r, carry):
        rows, res = scores(r)
        for hh, (s_loc, s_ctx) in enumerate(res):
            mx = jnp.maximum(jnp.max(s_loc, axis=-1, keepdims=True),
                             jnp.max(s_ctx, axis=-1, keepdims=True))
            m_scr[hh, rows, :] = jnp.broadcast_to(mx, (GRID_W, LANES))
        return carry

    lax.fori_loop(0, n_rows, max_pass, 0, unroll=NA_UNROLL)

    def prob_pass(r, carry):
        rows, res = scores(r)
        for hh, (s_loc, s_ctx) in enumerate(res):
            mx = m_scr[hh, rows, :]
            s = jnp.concatenate([s_loc, s_ctx], axis=1)
            for j in range(s.shape[1] // LANES):
                p_scr[hh, rows, j * LANES:(j + 1) * LANES] = jnp.exp(
                    s[:, j * LANES:(j + 1) * LANES] - mx).astype(BF16)
        return carry

    lax.fori_loop(0, n_rows, prob_pass, 0, unroll=NA_UNROLL)

    def out_pass(r, carry):
        rows, wrows, _ = window(r)
        p = jnp.concatenate([p_scr[0, rows, :], p_scr[1, rows, :]], axis=0)
        o = (jnp.dot(p[:, :win], vext[wrows, :], preferred_element_type=F32)
             + jnp.dot(p[:, win:], vcext[...], preferred_element_type=F32))
        num = jnp.where(first, o[:GRID_W, :LANES], o[GRID_W:, :LANES])
        den = jnp.where(first, o[:GRID_W, LANES:], o[GRID_W:, LANES:])
        o_ref[0, rows, :] = (num / den).astype(BF16)
        return carry

    lax.fori_loop(0, n_rows, out_pass, 0, unroll=NA_UNROLL)


def _na(qkv, ckv, bias):
    b, n, _ = qkv.shape
    ctx = ckv.shape[1]
    pairs = NA_W // LANES
    win = NA_WIN_H * GRID_W
    blk = lambda t, off: pl.BlockSpec((1, t, LANES), lambda bi, p: (bi, 0, off + p))
    return pl.pallas_call(
        _na_kernel,
        grid=(b, pairs),
        in_specs=[blk(n, 0), blk(n, pairs), blk(n, 2 * pairs),
                  blk(ctx, pairs), blk(ctx, 2 * pairs),
                  pl.BlockSpec((2, NA_WIN_H, GRID_W, win), lambda bi, p: (p, 0, 0, 0))],
        out_specs=pl.BlockSpec((1, n, LANES), lambda bi, p: (bi, 0, p)),
        out_shape=jax.ShapeDtypeStruct((b, n, NA_W), BF16),
        scratch_shapes=[pltpu.VMEM((n, 2 * LANES), BF16), pltpu.VMEM((ctx, 2 * LANES), BF16),
                        pltpu.VMEM((2, n, LANES), F32), pltpu.VMEM((2, n, win + ctx), BF16)],
        compiler_params=_cparams(2),
        name="na",
    )(qkv, qkv, qkv, ckv, ckv, bias)


def _ret_kernel(lgf_ref, lgb_ref, q_ref, k_ref, v_ref, g_ref, ck_ref, cv_ref, o_ref,
                dmat, xi_f, xi_b, zeta_f, zeta_b, gl_f, gl_b, sf_scr, sb_scr, p_scr, o_acc):
    L = RET_CHUNK
    hd = pl.program_id(1)
    lgf = lgf_ref[hd]
    lgb = lgb_ref[hd]
    n_chunks = q_ref.shape[1] // L
    c_chunks = ck_ref.shape[1] // L

    li = lax.broadcasted_iota(jnp.int32, (L, L), 0).astype(F32)
    mi = lax.broadcasted_iota(jnp.int32, (L, L), 1).astype(F32)
    diff = li - mi
    dmat[...] = jnp.where(diff >= 0.0, jnp.exp(lgf * jnp.maximum(diff, 0.0)),
                          jnp.exp(lgb * jnp.maximum(-diff, 0.0)))
    zeta_f[...] = jnp.exp(lgf * (L - 1.0 - li))
    zeta_b[...] = jnp.exp(lgb * li)
    lv = lax.broadcasted_iota(jnp.int32, (L, RET_V_DIM), 0).astype(F32)
    xi_f[...] = jnp.exp(lgf * (lv + 1.0))
    xi_b[...] = jnp.exp(lgb * (L - lv))
    gl_f[...] = jnp.exp(jnp.full((L, RET_V_DIM), L, F32) * lgf)
    gl_b[...] = jnp.exp(jnp.full((L, RET_V_DIM), L, F32) * lgb)

    tn = (((0,), (0,)), ((), ()))
    nt = (((1,), (1,)), ((), ()))

    def advance(state, gl, zeta, k, v):
        kz = (k.astype(F32) * zeta[...]).astype(BF16)
        return gl[...] * state + lax.dot_general(kz, v, tn, preferred_element_type=F32)

    sf = jnp.zeros((RET_QK_DIM, RET_V_DIM), F32)
    sb = jnp.zeros((RET_QK_DIM, RET_V_DIM), F32)
    for c in range(c_chunks):
        sf = advance(sf, gl_f, zeta_f, ck_ref[0, c * L:(c + 1) * L, :], cv_ref[0, c * L:(c + 1) * L, :])
    for c in reversed(range(c_chunks)):
        sb = advance(sb, gl_b, zeta_b, ck_ref[0, c * L:(c + 1) * L, :], cv_ref[0, c * L:(c + 1) * L, :])

    chunk = lambda c: pl.ds(pl.multiple_of(c * L, L), L)

    def states(i, carry):
        sf, sb = carry
        cb = n_chunks - 1 - i
        sf_scr[i] = sf.astype(BF16)
        sb_scr[cb] = sb.astype(BF16)
        return (advance(sf, gl_f, zeta_f, k_ref[0, chunk(i), :], v_ref[0, chunk(i), :]),
                advance(sb, gl_b, zeta_b, k_ref[0, chunk(cb), :], v_ref[0, chunk(cb), :]))

    lax.fori_loop(0, n_chunks, states, (sf, sb), unroll=RET_UNROLL)

    def probs(c, carry):
        rows = chunk(c)
        a = lax.dot_general(q_ref[0, rows, :], k_ref[0, rows, :], nt, preferred_element_type=F32)
        p_scr[rows, :] = (a * dmat[...]).astype(BF16)
        return carry

    lax.fori_loop(0, n_chunks, probs, 0, unroll=RET_UNROLL)

    def outs(c, carry):
        rows = chunk(c)
        q = q_ref[0, rows, :]
        o_acc[rows, :] = (jnp.dot(p_scr[rows, :], v_ref[0, rows, :], preferred_element_type=F32)
                          + jnp.dot(q, sf_scr[c], preferred_element_type=F32) * xi_f[...]
                          + jnp.dot(q, sb_scr[c], preferred_element_type=F32) * xi_b[...])
        return carry

    lax.fori_loop(0, n_chunks, outs, 0, unroll=RET_UNROLL)

    def norm(c, carry):
        rows = chunk(c)
        o = o_acc[rows, :]
        mu = jnp.mean(o, axis=-1, keepdims=True)
        d = o - mu
        var = jnp.mean(d * d, axis=-1, keepdims=True)
        o_ref[0, rows, :] = (d * lax.rsqrt(var + EPS) * g_ref[0, rows, :].astype(F32)).astype(BF16)
        return carry

    lax.fori_loop(0, n_chunks, norm, 0, unroll=RET_UNROLL)


def _ret(lgf, lgb, rqk, rv, rg, crqk, crv):
    b, n, _ = rqk.shape
    ctx = crqk.shape[1]
    L = RET_CHUNK
    smem = pl.BlockSpec(memory_space=pltpu.SMEM)
    qk = lambda t, off: pl.BlockSpec((1, t, RET_QK_DIM), lambda bi, h: (bi, 0, off + h))
    vv = lambda t: pl.BlockSpec((1, t, RET_V_DIM), lambda bi, h: (bi, 0, h))
    vm = lambda r, c: pltpu.VMEM((r, c), F32)
    st = pltpu.VMEM((n // L, RET_QK_DIM, RET_V_DIM), BF16)
    return pl.pallas_call(
        _ret_kernel,
        grid=(b, RET_HEADS),
        in_specs=[smem, smem, qk(n, 0), qk(n, RET_HEADS), vv(n), vv(n), qk(ctx, RET_HEADS), vv(ctx)],
        out_specs=vv(n),
        out_shape=jax.ShapeDtypeStruct((b, n, RET_V_W), BF16),
        scratch_shapes=[vm(L, L), vm(L, RET_V_DIM), vm(L, RET_V_DIM), vm(L, L), vm(L, L),
                        vm(L, RET_V_DIM), vm(L, RET_V_DIM), st, st,
                        pltpu.VMEM((n, L), BF16), vm(n, RET_V_DIM)],
        compiler_params=_cparams(2),
        name="ret",
    )(lgf, lgb, rqk, rqk, rv, rg, crqk, crv)


def _merge_kernel(x_ref, ona_ref, oret_ref, gate_ref, g1_ref, wna_ref, wret_ref, wout_ref, o_ref):
    a = jnp.dot(ona_ref[0], wna_ref[...], preferred_element_type=F32)
    r = jnp.dot(oret_ref[0], wret_ref[...], preferred_element_type=F32)
    merged = (gate_ref[0, :, :D_MODEL].astype(F32) * a
              + gate_ref[0, :, D_MODEL:].astype(F32) * r).astype(BF16)
    y = jnp.dot(merged, wout_ref[...], preferred_element_type=F32)
    o_ref[0] = x_ref[0] + g1_ref[0] * y


def _merge(x, o_na, o_ret, gates, g1, w_na, w_ret, w_out):
    b, n, _ = x.shape
    tm = TM_PROJ
    tok = lambda w: pl.BlockSpec((1, tm, w), lambda bi, i: (bi, i, 0))
    return pl.pallas_call(
        _merge_kernel,
        grid=(b, n // tm),
        in_specs=[tok(D_MODEL), tok(NA_W), tok(RET_V_W), tok(2 * D_MODEL),
                  pl.BlockSpec((1, 1, D_MODEL), lambda bi, i: (bi, 0, 0)),
                  _resident((NA_W, D_MODEL)), _resident((RET_V_W, D_MODEL)),
                  _resident((D_MODEL, D_MODEL))],
        out_specs=tok(D_MODEL),
        out_shape=jax.ShapeDtypeStruct((b, n, D_MODEL), F32),
        compiler_params=_cparams(2),
        name="merge",
    )(x, o_na, o_ret, gates, g1, w_na, w_ret, w_out)


def _ffn_kernel(x_ref, xp_ref, xn_ref, sh_ref, sc_ref, g2_ref, gn_ref, gf_ref,
                wup_ref, cw_ref, cb_ref, wdown_ref, o_ref, h_scr, u_scr, y_scr):
    i = pl.program_id(1)
    last = pl.num_programs(1) - 1
    tm = x_ref.shape[1]
    rows = tm + 2 * HALO
    n_ch = wup_ref.shape[1]
    x = x_ref[0]
    xa = jnp.concatenate([xp_ref[0], x, xn_ref[0]], axis=0)
    h = _rms_modulate(xa, gn_ref[...], sh_ref[0], sc_ref[0])
    ridx = lax.broadcasted_iota(jnp.int32, (rows, 1), 0)
    valid = jnp.logical_and(jnp.logical_or(ridx >= HALO, i > 0),
                            jnp.logical_or(ridx < tm + HALO, i < last))
    h_scr[...] = jnp.where(valid, h, 0.0).astype(BF16)

    def up(j, slot):
        for g in range(2):
            u_scr[slot, g] = jnp.dot(h_scr[...], wup_ref[g, j], preferred_element_type=F32)

    def act(j, slot):
        def conv(g):
            u = u_scr[slot, g]
            w = cw_ref[g, j]
            y = (pltpu.roll(u, 1, 0) * w[0:1] + u * w[1:2] + pltpu.roll(u, rows - 1, 0) * w[2:3]
                 + cb_ref[g, j])
            return y[HALO:HALO + tm]

        y_scr[:, j * FF_CHUNK:(j + 1) * FF_CHUNK] = (jax.nn.silu(conv(0)) * conv(1)).astype(BF16)

    def down(j0, j1):
        c0, c1 = j0 * FF_CHUNK, j1 * FF_CHUNK
        return jnp.dot(y_scr[:, c0:c1], wdown_ref[c0:c1, :], preferred_element_type=F32)

    up(0, 0)
    up(1, 1)
    acc = None
    for j in range(n_ch):
        act(j, j % 3)
        if j + 2 < n_ch:
            up(j + 2, (j + 2) % 3)
        if j % 3 == 0 and j > 0:
            part = down(j - 3, j)
            acc = part if acc is None else acc + part
    j0 = (n_ch - 1) // 3 * 3
    acc = acc + down(j0, n_ch)
    z = x + g2_ref[0] * acc
    o_ref[0] = z * lax.rsqrt(jnp.mean(z * z, axis=-1, keepdims=True) + EPS) * gf_ref[...]


def _ffn(x, sh2, sc2, g2, gain2, gain_f, w_up, conv_w, conv_b, w_down):
    b, n, _ = x.shape
    tm = TM_FFN
    per = tm // HALO
    n_halo = n // HALO
    n_ch = D_FF // FF_CHUNK
    rows = tm + 2 * HALO
    w_up = w_up.reshape(D_MODEL, 2, n_ch, FF_CHUNK).transpose(1, 2, 0, 3)
    conv_w = conv_w.reshape(CONV_W, 2, n_ch, FF_CHUNK).transpose(1, 2, 0, 3)
    conv_b = conv_b.reshape(2, n_ch, 1, FF_CHUNK)
    tok = pl.BlockSpec((1, tm, D_MODEL), lambda bi, i: (bi, i, 0))
    mod = pl.BlockSpec((1, 1, D_MODEL), lambda bi, i: (bi, 0, 0))
    prev = pl.BlockSpec((1, HALO, D_MODEL), lambda bi, i: (bi, jnp.maximum(i * per - 1, 0), 0))
    nxt = pl.BlockSpec((1, HALO, D_MODEL), lambda bi, i: (bi, jnp.minimum((i + 1) * per, n_halo - 1), 0))
    return pl.pallas_call(
        _ffn_kernel,
        grid=(b, n // tm),
        in_specs=[tok, prev, nxt, mod, mod, mod,
                  _resident((1, D_MODEL)), _resident((1, D_MODEL)),
                  _resident(w_up.shape), _resident(conv_w.shape),
                  _resident(conv_b.shape), _resident((D_FF, D_MODEL))],
        out_specs=tok,
        out_shape=jax.ShapeDtypeStruct((b, n, D_MODEL), F32),
        scratch_shapes=[pltpu.VMEM((rows, D_MODEL), BF16),
                        pltpu.VMEM((3, 2, rows, FF_CHUNK), F32),
                        pltpu.VMEM((tm, D_FF), BF16)],
        compiler_params=_cparams(2),
        name="ffn",
    )(x, x, x, sh2, sc2, g2, gain2, gain_f, w_up, conv_w, conv_b, w_down)


def _rope_tables(n):
    nf = RET_QK_DIM // 4
    inv = ROPE_BASE ** (-jnp.arange(nf, dtype=F32) / nf)
    t = jnp.arange(n)
    ang_r = (t // GRID_W).astype(F32)[:, None] * inv[None, :]
    ang_c = (t % GRID_W).astype(F32)[:, None] * inv[None, :]
    cos = jnp.concatenate([jnp.cos(ang_r)] * 2 + [jnp.cos(ang_c)] * 2, axis=-1)
    zero = jnp.zeros((n, nf), F32)
    s1 = jnp.concatenate([-jnp.sin(ang_r), zero, -jnp.sin(ang_c), zero], axis=-1)
    s2 = jnp.concatenate([zero, jnp.sin(ang_r), zero, jnp.sin(ang_c)], axis=-1)
    return cos, s1, s2


def _na_bias_table(rpb):
    w, kh, kw = GRID_W, NA_WIN_H, NA_WIN_W
    cols = np.arange(w)
    c_start = np.clip(cols - kw // 2, 0, w - kw)
    kcol = np.arange(w)
    valid = (kcol[None, :] >= c_start[:, None]) & (kcol[None, :] < c_start[:, None] + kw)
    dc = kcol[None, :] - cols[:, None] + (kw - 1)
    onehot = ((dc[:, :, None] == np.arange(2 * kw - 1)) & valid[:, :, None]).astype(np.float32)
    base = jnp.einsum('hrd,ckd->hrck', rpb, onehot, precision=lax.Precision.HIGHEST)
    base = base + np.where(valid, 0.0, MASK_VALUE).astype(np.float32)
    tab = jnp.stack([base[:, v:v + kh] for v in range(kh)], axis=1)
    return tab.transpose(0, 1, 3, 2, 4).reshape(rpb.shape[0], kh, w, kh * w)


def kernel(x, c, ctx, c_ctx, w_ada, b_ada, norm1_g, w_in, na_rpb, ret_log_decay_fwd, ret_log_decay_bwd,
           w_proj_na, w_proj_ret, w_out, norm2_g, w_up, conv_w, conv_b, w_down, final_norm_g):
    b, n, d = x.shape
    n_ctx = ctx.shape[1]
    assert w_ada.shape[0] == 1, "single layer"
    row = lambda v: v.reshape(1, -1)

    cc = jnp.concatenate([c, c_ctx[None], jnp.zeros((ADA_ROWS - b - 1, d), F32)], axis=0)
    mod = _ada(cc, w_ada[0].astype(BF16), row(b_ada[0]))
    xmod = mod[:b].reshape(b, 1, 6, d)
    sh1, sc1, g1, sh2, sc2, g2 = (xmod[:, :, i] for i in range(6))
    cmod = mod[b:b + 1].reshape(1, 1, 6, d)
    csh1, csc1 = cmod[:, :, 0], cmod[:, :, 1]

    w_in_b = w_in[0].astype(BF16)
    cos, s1, s2 = _rope_tables(n)
    qkv, rqk, rv, rg, gates = _inproj(x, sh1, sc1, row(norm1_g[0]), w_in_b, cos, s1, s2, "inproj")
    ones = jnp.ones((n_ctx, RET_QK_DIM), F32)
    zeros = jnp.zeros((n_ctx, RET_QK_DIM), F32)
    cqkv, crqk, crv, _, _ = _inproj(ctx, csh1, csc1, row(norm1_g[0]), w_in_b, ones, zeros, zeros, "inproj_ctx")

    o_na = _na(qkv, cqkv, _na_bias_table(na_rpb[0]))
    o_ret = _ret(ret_log_decay_fwd[0], ret_log_decay_bwd[0], rqk, rv, rg, crqk, crv)

    x1 = _merge(x, o_na, o_ret, gates, g1, w_proj_na[0].astype(BF16), w_proj_ret[0].astype(BF16),
                w_out[0].astype(BF16))
    return _ffn(x1, sh2, sc2, g2, row(norm2_g[0]), row(final_norm_g), w_up[0].astype(BF16),
                conv_w[0], conv_b[0], w_down[0].astype(BF16))
```

```python
import functools
import math

import numpy as np
import jax
import jax.numpy as jnp
from jax import lax
from jax.experimental import pallas as pl
from jax.experimental.pallas import tpu as pltpu

F32 = jnp.float32
BF16 = jnp.bfloat16

D_MODEL = 1024
GRID_W = 64
NA_HEADS = 8
NA_HEAD_DIM = 64
NA_WIN_H = 8
NA_WIN_W = 16
RET_HEADS = 4
RET_QK_DIM = 128
RET_V_DIM = 256
RET_CHUNK = 128
D_FF = int(math.ceil(8 * D_MODEL / 3 / 128)) * 128
CONV_W = 3
ROPE_BASE = 10000.0
EPS = 1e-6

NA_W = NA_HEADS * NA_HEAD_DIM
RET_QK_W = RET_HEADS * RET_QK_DIM
RET_V_W = RET_HEADS * RET_V_DIM
OFF_Q, OFF_K, OFF_V = 0, NA_W, 2 * NA_W
OFF_RQ = 3 * NA_W
OFF_RK = OFF_RQ + RET_QK_W
OFF_RV = OFF_RK + RET_QK_W
OFF_RG = OFF_RV + RET_V_W
OFF_GA = OFF_RG + RET_V_W
OFF_GB = OFF_GA + D_MODEL
IN_COLS = OFF_GB + D_MODEL

NA_SCALE = NA_HEAD_DIM ** -0.5
RK_SCALE = RET_QK_DIM ** -0.5
MASK_VALUE = -1e30

VMEM_LIMIT_BYTES = 56 * 1024 * 1024
LANES = 128
HALO = 16

TM_PROJ = 512
TM_FFN = 512
ADA_ROWS = 16
ADA_TN = 1536
FF_CHUNK = 256
PROJ_CHUNK = 512
NA_BLOCK = 8
RET_UNROLL = 4
RET_BLOCK = 4


def _cparams(n_axes):
    return pltpu.CompilerParams(dimension_semantics=("arbitrary",) * n_axes,
                                vmem_limit_bytes=VMEM_LIMIT_BYTES)


def _resident(shape):
    nd = len(shape)
    return pl.BlockSpec(shape, lambda *_: (0,) * nd, pipeline_mode=pl.Buffered(1))


def _rms_modulate(x, gain, shift, scale):
    y = x * lax.rsqrt(jnp.mean(x * x, axis=-1, keepdims=True) + EPS)
    return (y * gain) * (1.0 + scale) + shift


def _ada_kernel(c_ref, w_ref, b_ref, o_ref):
    s = jax.nn.silu(c_ref[...])
    o_ref[...] = jnp.dot(s.astype(BF16), w_ref[...], preferred_element_type=F32) + b_ref[...]


def _ada(cc, w_ada, b_ada):
    n_out = w_ada.shape[1]
    return pl.pallas_call(
        _ada_kernel,
        grid=(n_out // ADA_TN,),
        in_specs=[pl.BlockSpec((ADA_ROWS, D_MODEL), lambda j: (0, 0)),
                  pl.BlockSpec((D_MODEL, ADA_TN), lambda j: (0, j)),
                  pl.BlockSpec((1, ADA_TN), lambda j: (0, j))],
        out_specs=pl.BlockSpec((ADA_ROWS, ADA_TN), lambda j: (0, j)),
        out_shape=jax.ShapeDtypeStruct((ADA_ROWS, n_out), F32),
        compiler_params=_cparams(1),
        name="ada",
    )(cc, w_ada, b_ada)


def _inproj_kernel(x_ref, sh_ref, sc_ref, g_ref, w_ref, cs_ref, s1_ref, s2_ref,
                   qkv_ref, rqk_ref, rv_ref, rg_ref, gate_ref):
    h = _rms_modulate(x_ref[0], g_ref[...], sh_ref[0], sc_ref[0]).astype(BF16)

    def proj(c0):
        return jnp.dot(h, w_ref[:, c0:c0 + PROJ_CHUNK], preferred_element_type=F32)

    for j in range(3 * NA_W // PROJ_CHUNK):
        r = proj(OFF_Q + j * PROJ_CHUNK)
        if j * PROJ_CHUNK < NA_W:
            r = r * NA_SCALE
        qkv_ref[0, :, j * PROJ_CHUNK:(j + 1) * PROJ_CHUNK] = r.astype(BF16)

    cs, s1, s2 = cs_ref[...], s1_ref[...], s2_ref[...]
    quarter = RET_QK_DIM // 4
    for j in range(2 * RET_QK_W // PROJ_CHUNK):
        r = proj(OFF_RQ + j * PROJ_CHUNK)
        for hh in range(PROJ_CHUNK // RET_QK_DIM):
            t = r[:, hh * RET_QK_DIM:(hh + 1) * RET_QK_DIM]
            t = (t * cs + pltpu.roll(t, RET_QK_DIM - quarter, 1) * s1
                 + pltpu.roll(t, quarter, 1) * s2)
            if j * PROJ_CHUNK >= RET_QK_W:
                t = t * RK_SCALE
            c0 = j * PROJ_CHUNK + hh * RET_QK_DIM
            rqk_ref[0, :, c0:c0 + RET_QK_DIM] = t.astype(BF16)

    for j in range(RET_V_W // PROJ_CHUNK):
        rv_ref[0, :, j * PROJ_CHUNK:(j + 1) * PROJ_CHUNK] = proj(OFF_RV + j * PROJ_CHUNK).astype(BF16)
    for j in range(RET_V_W // PROJ_CHUNK):
        r = proj(OFF_RG + j * PROJ_CHUNK)
        rg_ref[0, :, j * PROJ_CHUNK:(j + 1) * PROJ_CHUNK] = jax.nn.silu(r).astype(BF16)
    for j in range(2 * D_MODEL // PROJ_CHUNK):
        r = proj(OFF_GA + j * PROJ_CHUNK)
        gate_ref[0, :, j * PROJ_CHUNK:(j + 1) * PROJ_CHUNK] = jax.nn.sigmoid(r).astype(BF16)


def _inproj(x, shift, scale, gain, w_in, cs, s1, s2, name):
    b, t, _ = x.shape
    tm = min(TM_PROJ, t)
    per_batch = shift.shape[0] == b
    mod_map = (lambda bi, i: (bi, 0, 0)) if per_batch else (lambda bi, i: (0, 0, 0))
    tok = lambda w: pl.BlockSpec((1, tm, w), lambda bi, i: (bi, i, 0))
    tab = pl.BlockSpec((tm, RET_QK_DIM), lambda bi, i: (i, 0))
    out_w = (3 * NA_W, 2 * RET_QK_W, RET_V_W, RET_V_W, 2 * D_MODEL)
    return pl.pallas_call(
        _inproj_kernel,
        grid=(b, t // tm),
        in_specs=[tok(D_MODEL),
                  pl.BlockSpec((1, 1, D_MODEL), mod_map),
                  pl.BlockSpec((1, 1, D_MODEL), mod_map),
                  _resident((1, D_MODEL)),
                  _resident((D_MODEL, IN_COLS)),
                  tab, tab, tab],
        out_specs=[tok(w) for w in out_w],
        out_shape=[jax.ShapeDtypeStruct((b, t, w), BF16) for w in out_w],
        compiler_params=_cparams(2),
        name=name,
    )(x, shift, scale, gain, w_in, cs, s1, s2)


def _na_kernel(q_ref, k_ref, v_ref, kc_ref, vc_ref, bias_ref, o_ref, vext, vcext, s_scr, m_scr, p_scr):
    n_rows = q_ref.shape[1] // GRID_W
    n_blk = n_rows // NA_BLOCK
    win = NA_WIN_H * GRID_W
    width = s_scr.shape[2]
    pair = 2 * GRID_W
    first = lax.broadcasted_iota(jnp.int32, (GRID_W, LANES), 1) < NA_HEAD_DIM
    nt = (((1,), (1,)), ((), ()))

    vext[:, :LANES] = v_ref[0]
    vext[:, LANES:] = jnp.ones((vext.shape[0], LANES), BF16)
    vcext[:, :LANES] = vc_ref[0]
    vcext[:, LANES:] = jnp.ones((vcext.shape[0], LANES), BF16)
    kc = kc_ref[0]

    def window(r):
        rs = jnp.clip(r - NA_WIN_H // 2, 0, n_rows - NA_WIN_H)
        return (pl.ds(pl.multiple_of(r * GRID_W, GRID_W), GRID_W),
                pl.ds(pl.multiple_of(rs * GRID_W, GRID_W), win), rs - r + (NA_WIN_H - 1))

    def score_stage(blk, slot):
        for k in range(NA_BLOCK):
            rows, wrows, var = window(blk * NA_BLOCK + k)
            q = q_ref[0, rows, :]
            kw = k_ref[0, wrows, :]
            zero = jnp.zeros_like(q)
            qs = jnp.concatenate([jnp.where(first, q, zero), jnp.where(first, zero, q)], axis=0)
            s_loc = lax.dot_general(qs, kw, nt, preferred_element_type=F32)
            s_ctx = lax.dot_general(qs, kc, nt, preferred_element_type=F32)
            for hh in range(2):
                dst = slice(k * pair + hh * GRID_W, k * pair + (hh + 1) * GRID_W)
                sl = s_loc[hh * GRID_W:(hh + 1) * GRID_W] + bias_ref[hh, var]
                sc = s_ctx[hh * GRID_W:(hh + 1) * GRID_W]
                s_scr[slot, dst, :win] = sl
                s_scr[slot, dst, win:] = sc
                mx = jnp.maximum(jnp.max(sl, axis=-1, keepdims=True), jnp.max(sc, axis=-1, keepdims=True))
                m_scr[slot, dst, :] = jnp.broadcast_to(mx, (GRID_W, LANES))

    def prob_stage(slot):
        for k in range(NA_BLOCK):
            dst = slice(k * pair, (k + 1) * pair)
            mx = m_scr[slot, dst, :]
            for j in range(width // LANES):
                cols = slice(j * LANES, (j + 1) * LANES)
                p_scr[slot, dst, cols] = jnp.exp(s_scr[slot, dst, cols] - mx).astype(BF16)

    def out_stage(blk, slot):
        for k in range(NA_BLOCK):
            rows, wrows, _ = window(blk * NA_BLOCK + k)
            p = p_scr[slot, k * pair:(k + 1) * pair, :]
            o = (jnp.dot(p[:, :win], vext[wrows, :], preferred_element_type=F32)
                 + jnp.dot(p[:, win:], vcext[...], preferred_element_type=F32))
            num = jnp.where(first, o[:GRID_W, :LANES], o[GRID_W:, :LANES])
            den = jnp.where(first, o[:GRID_W, LANES:], o[GRID_W:, LANES:])
            o_ref[0, rows, :] = (num / den).astype(BF16)

    assert n_blk % 2 == 0 and n_blk >= 4
    score_stage(0, 0)
    score_stage(1, 1)
    prob_stage(0)

    def step(t, carry):
        blk = 2 * t
        out_stage(blk, 0)
        prob_stage(1)
        score_stage(blk + 2, 0)
        out_stage(blk + 1, 1)
        prob_stage(0)
        score_stage(blk + 3, 1)
        return carry

    lax.fori_loop(0, n_blk // 2 - 1, step, 0)
    out_stage(n_blk - 2, 0)
    prob_stage(1)
    out_stage(n_blk - 1, 1)


def _na(qkv, ckv, bias):
    b, n, _ = qkv.shape
    ctx = ckv.shape[1]
    pairs = NA_W // LANES
    win = NA_WIN_H * GRID_W
    stage_rows = NA_BLOCK * 2 * GRID_W
    blk = lambda t, off: pl.BlockSpec((1, t, LANES), lambda bi, p: (bi, 0, off + p))
    return pl.pallas_call(
        _na_kernel,
        grid=(b, pairs),
        in_specs=[blk(n, 0), blk(n, pairs), blk(n, 2 * pairs),
                  blk(ctx, pairs), blk(ctx, 2 * pairs),
                  pl.BlockSpec((2, NA_WIN_H, GRID_W, win), lambda bi, p: (p, 0, 0, 0))],
        out_specs=pl.BlockSpec((1, n, LANES), lambda bi, p: (bi, 0, p)),
        out_shape=jax.ShapeDtypeStruct((b, n, NA_W), BF16),
        scratch_shapes=[pltpu.VMEM((n, 2 * LANES), BF16), pltpu.VMEM((ctx, 2 * LANES), BF16),
                        pltpu.VMEM((2, stage_rows, win + ctx), F32), pltpu.VMEM((2, stage_rows, LANES), F32),
                        pltpu.VMEM((2, stage_rows, win + ctx), BF16)],
        compiler_params=_cparams(2),
        name="na",
    )(qkv, qkv, qkv, ckv, ckv, bias)


def _ret_kernel(lgf_ref, lgb_ref, q_ref, k_ref, v_ref, g_ref, ck_ref, cv_ref, o_ref,
                dmat, xi_f, xi_b, zeta_f, zeta_b, gl_f, gl_b, sb_scr, p_scr, o_acc):
    L = RET_CHUNK
    hd = pl.program_id(1)
    lgf = lgf_ref[hd]
    lgb = lgb_ref[hd]
    n_chunks = q_ref.shape[1] // L
    c_chunks = ck_ref.shape[1] // L

    li = lax.broadcasted_iota(jnp.int32, (L, L), 0).astype(F32)
    mi = lax.broadcasted_iota(jnp.int32, (L, L), 1).astype(F32)
    diff = li - mi
    dmat[...] = jnp.where(diff >= 0.0, jnp.exp(lgf * jnp.maximum(diff, 0.0)),
                          jnp.exp(lgb * jnp.maximum(-diff, 0.0)))
    zeta_f[...] = jnp.exp(lgf * (L - 1.0 - li))
    zeta_b[...] = jnp.exp(lgb * li)
    lv = lax.broadcasted_iota(jnp.int32, (L, RET_V_DIM), 0).astype(F32)
    xi_f[...] = jnp.exp(lgf * (lv + 1.0))
    xi_b[...] = jnp.exp(lgb * (L - lv))
    gl_f[...] = jnp.exp(jnp.full((L, RET_V_DIM), L, F32) * lgf)
    gl_b[...] = jnp.exp(jnp.full((L, RET_V_DIM), L, F32) * lgb)

    tn = (((0,), (0,)), ((), ()))
    nt = (((1,), (1,)), ((), ()))

    def advance(state, gl, zeta, k, v):
        kz = (k.astype(F32) * zeta[...]).astype(BF16)
        return gl[...] * state + lax.dot_general(kz, v, tn, preferred_element_type=F32)

    sf = jnp.zeros((RET_QK_DIM, RET_V_DIM), F32)
    sb = jnp.zeros((RET_QK_DIM, RET_V_DIM), F32)
    for c in range(c_chunks):
        sf = advance(sf, gl_f, zeta_f, ck_ref[0, c * L:(c + 1) * L, :], cv_ref[0, c * L:(c + 1) * L, :])
    for c in reversed(range(c_chunks)):
        sb = advance(sb, gl_b, zeta_b, ck_ref[0, c * L:(c + 1) * L, :], cv_ref[0, c * L:(c + 1) * L, :])

    chunk = lambda c: pl.ds(pl.multiple_of(c * L, L), L)

    def bwd_states(i, sb):
        cb = n_chunks - 1 - i
        sb_scr[cb] = sb.astype(BF16)
        return advance(sb, gl_b, zeta_b, k_ref[0, chunk(cb), :], v_ref[0, chunk(cb), :])

    lax.fori_loop(0, n_chunks, bwd_states, sb, unroll=RET_UNROLL)

    def probs(c):
        rows = chunk(c)
        a = lax.dot_general(q_ref[0, rows, :], k_ref[0, rows, :], nt, preferred_element_type=F32)
        p_scr[rows, :] = (a * dmat[...]).astype(BF16)

    def outs(c, sf):
        rows = chunk(c)
        q = q_ref[0, rows, :]
        v = v_ref[0, rows, :]
        o_acc[rows, :] = (jnp.dot(p_scr[rows, :], v, preferred_element_type=F32)
                          + jnp.dot(q, sf.astype(BF16), preferred_element_type=F32) * xi_f[...]
                          + jnp.dot(q, sb_scr[c], preferred_element_type=F32) * xi_b[...])
        return advance(sf, gl_f, zeta_f, k_ref[0, rows, :], v)

    def norm(c):
        rows = chunk(c)
        o = o_acc[rows, :]
        mu = jnp.mean(o, axis=-1, keepdims=True)
        d = o - mu
        var = jnp.mean(d * d, axis=-1, keepdims=True)
        o_ref[0, rows, :] = (d * lax.rsqrt(var + EPS) * g_ref[0, rows, :].astype(F32)).astype(BF16)

    n_blk = n_chunks // RET_BLOCK
    block = lambda t: [t * RET_BLOCK + i for i in range(RET_BLOCK)]

    def outs_block(t, sf):
        for c in block(t):
            sf = outs(c, sf)
        return sf

    for c in block(0):
        probs(c)
    sf = outs_block(0, sf)
    for c in block(1):
        probs(c)

    def step(t, sf):
        for c in block(t - 2):
            norm(c)
        sf = outs_block(t - 1, sf)
        for c in block(t):
            probs(c)
        return sf

    sf = lax.fori_loop(2, n_blk, step, sf)
    for c in block(n_blk - 2):
        norm(c)
    outs_block(n_blk - 1, sf)
    for c in block(n_blk - 1):
        norm(c)


def _ret(lgf, lgb, rqk, rv, rg, crqk, crv):
    b, n, _ = rqk.shape
    ctx = crqk.shape[1]
    L = RET_CHUNK
    smem = pl.BlockSpec(memory_space=pltpu.SMEM)
    qk = lambda t, off: pl.BlockSpec((1, t, RET_QK_DIM), lambda bi, h: (bi, 0, off + h))
    vv = lambda t: pl.BlockSpec((1, t, RET_V_DIM), lambda bi, h: (bi, 0, h))
    vm = lambda r, c: pltpu.VMEM((r, c), F32)
    st = pltpu.VMEM((n // L, RET_QK_DIM, RET_V_DIM), BF16)
    return pl.pallas_call(
        _ret_kernel,
        grid=(b, RET_HEADS),
        in_specs=[smem, smem, qk(n, 0), qk(n, RET_HEADS), vv(n), vv(n), qk(ctx, RET_HEADS), vv(ctx)],
        out_specs=vv(n),
        out_shape=jax.ShapeDtypeStruct((b, n, RET_V_W), BF16),
        scratch_shapes=[vm(L, L), vm(L, RET_V_DIM), vm(L, RET_V_DIM), vm(L, L), vm(L, L),
                        vm(L, RET_V_DIM), vm(L, RET_V_DIM), st,
                        pltpu.VMEM((n, L), BF16), vm(n, RET_V_DIM)],
        compiler_params=_cparams(2),
        name="ret",
    )(lgf, lgb, rqk, rqk, rv, rg, crqk, crv)


def _merge_kernel(x_ref, ona_ref, oret_ref, gate_ref, g1_ref, wna_ref, wret_ref, wout_ref, o_ref):
    a = jnp.dot(ona_ref[0], wna_ref[...], preferred_element_type=F32)
    r = jnp.dot(oret_ref[0], wret_ref[...], preferred_element_type=F32)
    merged = (gate_ref[0, :, :D_MODEL].astype(F32) * a
              + gate_ref[0, :, D_MODEL:].astype(F32) * r).astype(BF16)
    y = jnp.dot(merged, wout_ref[...], preferred_element_type=F32)
    o_ref[0] = x_ref[0] + g1_ref[0] * y


def _merge(x, o_na, o_ret, gates, g1, w_na, w_ret, w_out):
    b, n, _ = x.shape
    tm = TM_PROJ
    tok = lambda w: pl.BlockSpec((1, tm, w), lambda bi, i: (bi, i, 0))
    return pl.pallas_call(
        _merge_kernel,
        grid=(b, n // tm),
        in_specs=[tok(D_MODEL), tok(NA_W), tok(RET_V_W), tok(2 * D_MODEL),
                  pl.BlockSpec((1, 1, D_MODEL), lambda bi, i: (bi, 0, 0)),
                  _resident((NA_W, D_MODEL)), _resident((RET_V_W, D_MODEL)),
                  _resident((D_MODEL, D_MODEL))],
        out_specs=tok(D_MODEL),
        out_shape=jax.ShapeDtypeStruct((b, n, D_MODEL), F32),
        compiler_params=_cparams(2),
        name="merge",
    )(x, o_na, o_ret, gates, g1, w_na, w_ret, w_out)


def _ffn_kernel(x_ref, xp_ref, xn_ref, sh_ref, sc_ref, g2_ref, gn_ref, gf_ref,
                wup_ref, cw_ref, cb_ref, wdown_ref, o_ref, h_scr, u_scr, y_scr):
    i = pl.program_id(1)
    last = pl.num_programs(1) - 1
    tm = x_ref.shape[1]
    rows = tm + 2 * HALO
    n_ch = wup_ref.shape[1]
    x = x_ref[0]
    xa = jnp.concatenate([xp_ref[0], x, xn_ref[0]], axis=0)
    h = _rms_modulate(xa, gn_ref[...], sh_ref[0], sc_ref[0])
    ridx = lax.broadcasted_iota(jnp.int32, (rows, 1), 0)
    valid = jnp.logical_and(jnp.logical_or(ridx >= HALO, i > 0),
                            jnp.logical_or(ridx < tm + HALO, i < last))
    h_scr[...] = jnp.where(valid, h, 0.0).astype(BF16)

    def up(j, slot):
        for g in range(2):
            u_scr[slot, g] = jnp.dot(h_scr[...], wup_ref[g, j], preferred_element_type=F32)

    def act(j, slot):
        def conv(g):
            u = u_scr[slot, g]
            w = cw_ref[g, j]
            y = (pltpu.roll(u, 1, 0) * w[0:1] + u * w[1:2] + pltpu.roll(u, rows - 1, 0) * w[2:3]
                 + cb_ref[g, j])
            return y[HALO:HALO + tm]

        y_scr[:, j * FF_CHUNK:(j + 1) * FF_CHUNK] = (jax.nn.silu(conv(0)) * conv(1)).astype(BF16)

    def down(j0, j1):
        c0, c1 = j0 * FF_CHUNK, j1 * FF_CHUNK
        return jnp.dot(y_scr[:, c0:c1], wdown_ref[c0:c1, :], preferred_element_type=F32)

    up(0, 0)
    up(1, 1)
    acc = None
    for j in range(n_ch):
        act(j, j % 3)
        if j + 2 < n_ch:
            up(j + 2, (j + 2) % 3)
        if j % 3 == 0 and j > 0:
            part = down(j - 3, j)
            acc = part if acc is None else acc + part
    j0 = (n_ch - 1) // 3 * 3
    acc = acc + down(j0, n_ch)
    z = x + g2_ref[0] * acc
    o_ref[0] = z * lax.rsqrt(jnp.mean(z * z, axis=-1, keepdims=True) + EPS) * gf_ref[...]


def _ffn(x, sh2, sc2, g2, gain2, gain_f, w_up, conv_w, conv_b, w_down):
    b, n, _ = x.shape
    tm = TM_FFN
    per = tm // HALO
    n_halo = n // HALO
    n_ch = D_FF // FF_CHUNK
    rows = tm + 2 * HALO
    w_up = w_up.reshape(D_MODEL, 2, n_ch, FF_CHUNK).transpose(1, 2, 0, 3)
    conv_w = conv_w.reshape(CONV_W, 2, n_ch, FF_CHUNK).transpose(1, 2, 0, 3)
    conv_b = conv_b.reshape(2, n_ch, 1, FF_CHUNK)
    tok = pl.BlockSpec((1, tm, D_MODEL), lambda bi, i: (bi, i, 0))
    mod = pl.BlockSpec((1, 1, D_MODEL), lambda bi, i: (bi, 0, 0))
    prev = pl.BlockSpec((1, HALO, D_MODEL), lambda bi, i: (bi, jnp.maximum(i * per - 1, 0), 0))
    nxt = pl.BlockSpec((1, HALO, D_MODEL), lambda bi, i: (bi, jnp.minimum((i + 1) * per, n_halo - 1), 0))
    return pl.pallas_call(
        _ffn_kernel,
        grid=(b, n // tm),
        in_specs=[tok, prev, nxt, mod, mod, mod,
                  _resident((1, D_MODEL)), _resident((1, D_MODEL)),
                  _resident(w_up.shape), _resident(conv_w.shape),
                  _resident(conv_b.shape), _resident((D_FF, D_MODEL))],
        out_specs=tok,
        out_shape=jax.ShapeDtypeStruct((b, n, D_MODEL), F32),
        scratch_shapes=[pltpu.VMEM((rows, D_MODEL), BF16),
                        pltpu.VMEM((3, 2, rows, FF_CHUNK), F32),
                        pltpu.VMEM((tm, D_FF), BF16)],
        compiler_params=_cparams(2),
        name="ffn",
    )(x, x, x, sh2, sc2, g2, gain2, gain_f, w_up, conv_w, conv_b, w_down)


def _rope_tables(n):
    nf = RET_QK_DIM // 4
    inv = ROPE_BASE ** (-jnp.arange(nf, dtype=F32) / nf)
    t = jnp.arange(n)
    ang_r = (t // GRID_W).astype(F32)[:, None] * inv[None, :]
    ang_c = (t % GRID_W).astype(F32)[:, None] * inv[None, :]
    cos = jnp.concatenate([jnp.cos(ang_r)] * 2 + [jnp.cos(ang_c)] * 2, axis=-1)
    zero = jnp.zeros((n, nf), F32)
    s1 = jnp.concatenate([-jnp.sin(ang_r), zero, -jnp.sin(ang_c), zero], axis=-1)
    s2 = jnp.concatenate([zero, jnp.sin(ang_r), zero, jnp.sin(ang_c)], axis=-1)
    return cos, s1, s2


def _na_bias_table(rpb):
    w, kh, kw = GRID_W, NA_WIN_H, NA_WIN_W
    cols = np.arange(w)
    c_start = np.clip(cols - kw // 2, 0, w - kw)
    kcol = np.arange(w)
    valid = (kcol[None, :] >= c_start[:, None]) & (kcol[None, :] < c_start[:, None] + kw)
    dc = kcol[None, :] - cols[:, None] + (kw - 1)
    onehot = ((dc[:, :, None] == np.arange(2 * kw - 1)) & valid[:, :, None]).astype(np.float32)
    base = jnp.einsum('hrd,ckd->hrck', rpb, onehot, precision=lax.Precision.HIGHEST)
    base = base + np.where(valid, 0.0, MASK_VALUE).astype(np.float32)
    tab = jnp.stack([base[:, v:v + kh] for v in range(kh)], axis=1)
    return tab.transpose(0, 1, 3, 2, 4).reshape(rpb.shape[0], kh, w, kh * w)


def kernel(x, c, ctx, c_ctx, w_ada, b_ada, norm1_g, w_in, na_rpb, ret_log_decay_fwd, ret_log_decay_bwd,
           w_proj_na, w_proj_ret, w_out, norm2_g, w_up, conv_w, conv_b, w_down, final_norm_g):
    b, n, d = x.shape
    n_ctx = ctx.shape[1]
    assert w_ada.shape[0] == 1, "single layer"
    row = lambda v: v.reshape(1, -1)

    cc = jnp.concatenate([c, c_ctx[None], jnp.zeros((ADA_ROWS - b - 1, d), F32)], axis=0)
    mod = _ada(cc, w_ada[0].astype(BF16), row(b_ada[0]))
    xmod = mod[:b].reshape(b, 1, 6, d)
    sh1, sc1, g1, sh2, sc2, g2 = (xmod[:, :, i] for i in range(6))
    cmod = mod[b:b + 1].reshape(1, 1, 6, d)
    csh1, csc1 = cmod[:, :, 0], cmod[:, :, 1]

    w_in_b = w_in[0].astype(BF16)
    cos, s1, s2 = _rope_tables(n)
    qkv, rqk, rv, rg, gates = _inproj(x, sh1, sc1, row(norm1_g[0]), w_in_b, cos, s1, s2, "inproj")
    ones = jnp.ones((n_ctx, RET_QK_DIM), F32)
    zeros = jnp.zeros((n_ctx, RET_QK_DIM), F32)
    cqkv, crqk, crv, _, _ = _inproj(ctx, csh1, csc1, row(norm1_g[0]), w_in_b, ones, zeros, zeros, "inproj_ctx")

    o_na = _na(qkv, cqkv, _na_bias_table(na_rpb[0]))
    o_ret = _ret(ret_log_decay_fwd[0], ret_log_decay_bwd[0], rqk, rv, rg, crqk, crv)

    x1 = _merge(x, o_na, o_ret, gates, g1, w_proj_na[0].astype(BF16), w_proj_ret[0].astype(BF16),
                w_out[0].astype(BF16))
    return _ffn(x1, sh2, sc2, g2, row(norm2_g[0]), row(final_norm_g), w_up[0].astype(BF16),
                conv_w[0], conv_b[0], w_down[0].astype(BF16))
```

```python
import math

import numpy as np
import jax
import jax.numpy as jnp
from jax import lax
from jax.experimental import pallas as pl
from jax.experimental.pallas import tpu as pltpu

F32 = jnp.float32
BF16 = jnp.bfloat16

D_MODEL = 1024
GRID_W = 64
NA_HEADS = 8
NA_HEAD_DIM = 64
NA_WIN_H = 8
NA_WIN_W = 16
RET_HEADS = 4
RET_QK_DIM = 128
RET_V_DIM = 256
RET_CHUNK = 128
D_FF = int(math.ceil(8 * D_MODEL / 3 / 128)) * 128
CONV_W = 3
ROPE_BASE = 10000.0
EPS = 1e-6

NA_W = NA_HEADS * NA_HEAD_DIM
RET_QK_W = RET_HEADS * RET_QK_DIM
RET_V_W = RET_HEADS * RET_V_DIM
OFF_Q, OFF_K, OFF_V = 0, NA_W, 2 * NA_W
OFF_RQ = 3 * NA_W
OFF_RK = OFF_RQ + RET_QK_W
OFF_RV = OFF_RK + RET_QK_W
OFF_RG = OFF_RV + RET_V_W
OFF_GA = OFF_RG + RET_V_W
OFF_GB = OFF_GA + D_MODEL
IN_COLS = OFF_GB + D_MODEL

NA_SCALE = NA_HEAD_DIM ** -0.5
RK_SCALE = RET_QK_DIM ** -0.5
MASK_VALUE = -1e30

VMEM_LIMIT_BYTES = 56 * 1024 * 1024
LANES = 128
HALO = 16

TM_PROJ = 512
TM_FFN = 512
ADA_ROWS = 16
ADA_TN = 1536
FF_CHUNK = 256
PROJ_CHUNK = 512
NA_BLOCK = 8
RET_UNROLL = 4
RET_BLOCK = 4


def _cparams(n_axes):
    return pltpu.CompilerParams(dimension_semantics=("arbitrary",) * n_axes,
                                vmem_limit_bytes=VMEM_LIMIT_BYTES)


def _resident(shape):
    nd = len(shape)
    return pl.BlockSpec(shape, lambda *_: (0,) * nd, pipeline_mode=pl.Buffered(1))


def _rms_modulate(x, gain, shift, scale):
    y = x * lax.rsqrt(jnp.mean(x * x, axis=-1, keepdims=True) + EPS)
    return (y * gain) * (1.0 + scale) + shift


def _ada_kernel(c_ref, w_ref, b_ref, o_ref):
    s = jax.nn.silu(c_ref[...])
    o_ref[...] = jnp.dot(s.astype(BF16), w_ref[...].astype(BF16), preferred_element_type=F32) + b_ref[...]


def _ada(cc, w_ada, b_ada):
    n_out = w_ada.shape[1]
    return pl.pallas_call(
        _ada_kernel,
        grid=(n_out // ADA_TN,),
        in_specs=[pl.BlockSpec((ADA_ROWS, D_MODEL), lambda j: (0, 0)),
                  pl.BlockSpec((D_MODEL, ADA_TN), lambda j: (0, j)),
                  pl.BlockSpec((1, ADA_TN), lambda j: (0, j))],
        out_specs=pl.BlockSpec((ADA_ROWS, ADA_TN), lambda j: (0, j)),
        out_shape=jax.ShapeDtypeStruct((ADA_ROWS, n_out), F32),
        compiler_params=_cparams(1),
        name="ada",
    )(cc, w_ada, b_ada)


def _inproj_kernel(x_ref, sh_ref, sc_ref, g_ref, w_ref, cs_ref, s1_ref, s2_ref,
                   qkv_ref, rqk_ref, rv_ref, rg_ref, gate_ref):
    h = _rms_modulate(x_ref[0], g_ref[...], sh_ref[0], sc_ref[0]).astype(BF16)

    def proj(c0):
        return jnp.dot(h, w_ref[:, c0:c0 + PROJ_CHUNK], preferred_element_type=F32)

    for j in range(3 * NA_W // PROJ_CHUNK):
        r = proj(OFF_Q + j * PROJ_CHUNK)
        if j * PROJ_CHUNK < NA_W:
            r = r * NA_SCALE
        qkv_ref[0, :, j * PROJ_CHUNK:(j + 1) * PROJ_CHUNK] = r.astype(BF16)

    cs, s1, s2 = cs_ref[...], s1_ref[...], s2_ref[...]
    quarter = RET_QK_DIM // 4
    for j in range(2 * RET_QK_W // PROJ_CHUNK):
        r = proj(OFF_RQ + j * PROJ_CHUNK)
        for hh in range(PROJ_CHUNK // RET_QK_DIM):
            t = r[:, hh * RET_QK_DIM:(hh + 1) * RET_QK_DIM]
            t = (t * cs + pltpu.roll(t, RET_QK_DIM - quarter, 1) * s1
                 + pltpu.roll(t, quarter, 1) * s2)
            if j * PROJ_CHUNK >= RET_QK_W:
                t = t * RK_SCALE
            c0 = j * PROJ_CHUNK + hh * RET_QK_DIM
            rqk_ref[0, :, c0:c0 + RET_QK_DIM] = t.astype(BF16)

    for j in range(RET_V_W // PROJ_CHUNK):
        rv_ref[0, :, j * PROJ_CHUNK:(j + 1) * PROJ_CHUNK] = proj(OFF_RV + j * PROJ_CHUNK).astype(BF16)
    for j in range(RET_V_W // PROJ_CHUNK):
        r = proj(OFF_RG + j * PROJ_CHUNK)
        rg_ref[0, :, j * PROJ_CHUNK:(j + 1) * PROJ_CHUNK] = jax.nn.silu(r).astype(BF16)
    for j in range(2 * D_MODEL // PROJ_CHUNK):
        r = proj(OFF_GA + j * PROJ_CHUNK)
        gate_ref[0, :, j * PROJ_CHUNK:(j + 1) * PROJ_CHUNK] = jax.nn.sigmoid(r).astype(BF16)


def _inproj(x, shift, scale, gain, w_in, cs, s1, s2):
    b, t, _ = x.shape
    tm = TM_PROJ
    mod_map = lambda bi, i: (bi, 0, 0)
    tok = lambda w: pl.BlockSpec((1, tm, w), lambda bi, i: (bi, i, 0))
    tab = pl.BlockSpec((tm, RET_QK_DIM), lambda bi, i: (i, 0))
    out_w = (3 * NA_W, 2 * RET_QK_W, RET_V_W, RET_V_W, 2 * D_MODEL)
    return pl.pallas_call(
        _inproj_kernel,
        grid=(b, t // tm),
        in_specs=[tok(D_MODEL),
                  pl.BlockSpec((1, 1, D_MODEL), mod_map),
                  pl.BlockSpec((1, 1, D_MODEL), mod_map),
                  _resident((1, D_MODEL)),
                  _resident((D_MODEL, IN_COLS)),
                  tab, tab, tab],
        out_specs=[tok(w) for w in out_w],
        out_shape=[jax.ShapeDtypeStruct((b, t, w), BF16) for w in out_w],
        compiler_params=_cparams(2),
        name="inproj",
    )(x, shift, scale, gain, w_in, cs, s1, s2)


def _inproj_ctx_kernel(x_ref, sh_ref, sc_ref, g_ref, w_ref, kv_ref, rk_ref, rv_ref):
    h = _rms_modulate(x_ref[0], g_ref[...], sh_ref[0], sc_ref[0]).astype(BF16)

    def proj(c0):
        return jnp.dot(h, w_ref[:, c0:c0 + PROJ_CHUNK], preferred_element_type=F32)

    for j in range(2 * NA_W // PROJ_CHUNK):
        kv_ref[0, :, j * PROJ_CHUNK:(j + 1) * PROJ_CHUNK] = proj(OFF_K + j * PROJ_CHUNK).astype(BF16)
    for j in range(RET_QK_W // PROJ_CHUNK):
        rk_ref[0, :, j * PROJ_CHUNK:(j + 1) * PROJ_CHUNK] = (
            proj(OFF_RK + j * PROJ_CHUNK) * RK_SCALE).astype(BF16)
    for j in range(RET_V_W // PROJ_CHUNK):
        rv_ref[0, :, j * PROJ_CHUNK:(j + 1) * PROJ_CHUNK] = proj(OFF_RV + j * PROJ_CHUNK).astype(BF16)


def _inproj_ctx(ctx, shift, scale, gain, w_in):
    b, t, _ = ctx.shape
    tok = lambda w: pl.BlockSpec((1, t, w), lambda bi: (bi, 0, 0))
    mod = pl.BlockSpec((1, 1, D_MODEL), lambda bi: (0, 0, 0))
    out_w = (2 * NA_W, RET_QK_W, RET_V_W)
    return pl.pallas_call(
        _inproj_ctx_kernel,
        grid=(b,),
        in_specs=[tok(D_MODEL), mod, mod, _resident((1, D_MODEL)), _resident((D_MODEL, IN_COLS))],
        out_specs=[tok(w) for w in out_w],
        out_shape=[jax.ShapeDtypeStruct((b, t, w), BF16) for w in out_w],
        compiler_params=_cparams(1),
        name="inproj_ctx",
    )(ctx, shift, scale, gain, w_in)


def _na_kernel(q_ref, k_ref, v_ref, kc_ref, vc_ref, bias_ref, o_ref, vext, vcext, s_scr, m_scr, p_scr):
    n_rows = q_ref.shape[1] // GRID_W
    n_blk = n_rows // NA_BLOCK
    win = NA_WIN_H * GRID_W
    width = s_scr.shape[2]
    pair = 2 * GRID_W
    first = lax.broadcasted_iota(jnp.int32, (GRID_W, LANES), 1) < NA_HEAD_DIM
    nt = (((1,), (1,)), ((), ()))

    vext[:, :LANES] = v_ref[0]
    vext[:, LANES:] = jnp.ones((vext.shape[0], LANES), BF16)
    vcext[:, :LANES] = vc_ref[0]
    vcext[:, LANES:] = jnp.ones((vcext.shape[0], LANES), BF16)
    kc = kc_ref[0]

    def window(r):
        rs = jnp.clip(r - NA_WIN_H // 2, 0, n_rows - NA_WIN_H)
        return (pl.ds(pl.multiple_of(r * GRID_W, GRID_W), GRID_W),
                pl.ds(pl.multiple_of(rs * GRID_W, GRID_W), win), rs - r + (NA_WIN_H - 1))

    def score_stage(blk, slot):
        for k in range(NA_BLOCK):
            rows, wrows, var = window(blk * NA_BLOCK + k)
            q = q_ref[0, rows, :]
            kw = k_ref[0, wrows, :]
            zero = jnp.zeros_like(q)
            qs = jnp.concatenate([jnp.where(first, q, zero), jnp.where(first, zero, q)], axis=0)
            s_loc = lax.dot_general(qs, kw, nt, preferred_element_type=F32)
            s_ctx = lax.dot_general(qs, kc, nt, preferred_element_type=F32)
            for hh in range(2):
                dst = slice(k * pair + hh * GRID_W, k * pair + (hh + 1) * GRID_W)
                sl = s_loc[hh * GRID_W:(hh + 1) * GRID_W] + bias_ref[hh, var]
                sc = s_ctx[hh * GRID_W:(hh + 1) * GRID_W]
                s_scr[slot, dst, :win] = sl
                s_scr[slot, dst, win:] = sc
                mx = jnp.maximum(jnp.max(sl, axis=-1, keepdims=True), jnp.max(sc, axis=-1, keepdims=True))
                m_scr[slot, dst, :] = jnp.broadcast_to(mx, (GRID_W, LANES))

    def prob_stage(slot):
        for k in range(NA_BLOCK):
            dst = slice(k * pair, (k + 1) * pair)
            mx = m_scr[slot, dst, :]
            for j in range(width // LANES):
                cols = slice(j * LANES, (j + 1) * LANES)
                p_scr[slot, dst, cols] = jnp.exp(s_scr[slot, dst, cols] - mx).astype(BF16)

    def out_stage(blk, slot):
        for k in range(NA_BLOCK):
            rows, wrows, _ = window(blk * NA_BLOCK + k)
            p = p_scr[slot, k * pair:(k + 1) * pair, :]
            o = (jnp.dot(p[:, :win], vext[wrows, :], preferred_element_type=F32)
                 + jnp.dot(p[:, win:], vcext[...], preferred_element_type=F32))
            num = jnp.where(first, o[:GRID_W, :LANES], o[GRID_W:, :LANES])
            den = jnp.where(first, o[:GRID_W, LANES:], o[GRID_W:, LANES:])
            o_ref[0, rows, :] = (num / den).astype(BF16)

    assert n_blk % 2 == 0 and n_blk >= 4
    score_stage(0, 0)
    score_stage(1, 1)
    prob_stage(0)

    def step(t, carry):
        blk = 2 * t
        out_stage(blk, 0)
        prob_stage(1)
        score_stage(blk + 2, 0)
        out_stage(blk + 1, 1)
        prob_stage(0)
        score_stage(blk + 3, 1)
        return carry

    lax.fori_loop(0, n_blk // 2 - 1, step, 0)
    out_stage(n_blk - 2, 0)
    prob_stage(1)
    out_stage(n_blk - 1, 1)


def _na(qkv, ckv, bias):
    b, n, _ = qkv.shape
    ctx = ckv.shape[1]
    pairs = NA_W // LANES
    win = NA_WIN_H * GRID_W
    stage_rows = NA_BLOCK * 2 * GRID_W
    blk = lambda t, off: pl.BlockSpec((1, t, LANES), lambda bi, p: (bi, 0, off + p))
    return pl.pallas_call(
        _na_kernel,
        grid=(b, pairs),
        in_specs=[blk(n, 0), blk(n, pairs), blk(n, 2 * pairs),
                  blk(ctx, 0), blk(ctx, pairs),
                  pl.BlockSpec((2, NA_WIN_H, GRID_W, win), lambda bi, p: (p, 0, 0, 0))],
        out_specs=pl.BlockSpec((1, n, LANES), lambda bi, p: (bi, 0, p)),
        out_shape=jax.ShapeDtypeStruct((b, n, NA_W), BF16),
        scratch_shapes=[pltpu.VMEM((n, 2 * LANES), BF16), pltpu.VMEM((ctx, 2 * LANES), BF16),
                        pltpu.VMEM((2, stage_rows, win + ctx), F32), pltpu.VMEM((2, stage_rows, LANES), F32),
                        pltpu.VMEM((2, stage_rows, win + ctx), BF16)],
        compiler_params=_cparams(2),
        name="na",
    )(qkv, qkv, qkv, ckv, ckv, bias)


def _ret_kernel(lgf_ref, lgb_ref, q_ref, k_ref, v_ref, g_ref, ck_ref, cv_ref, o_ref,
                dmat, xi_f, xi_b, zeta_f, zeta_b, gl_f, gl_b, sb_scr, p_scr, o_acc):
    L = RET_CHUNK
    hd = pl.program_id(1)
    lgf = lgf_ref[hd]
    lgb = lgb_ref[hd]
    n_chunks = q_ref.shape[1] // L
    c_chunks = ck_ref.shape[1] // L

    li = lax.broadcasted_iota(jnp.int32, (L, L), 0).astype(F32)
    mi = lax.broadcasted_iota(jnp.int32, (L, L), 1).astype(F32)
    diff = li - mi
    dmat[...] = jnp.where(diff >= 0.0, jnp.exp(lgf * jnp.maximum(diff, 0.0)),
                          jnp.exp(lgb * jnp.maximum(-diff, 0.0)))
    zeta_f[...] = jnp.exp(lgf * (L - 1.0 - li))
    zeta_b[...] = jnp.exp(lgb * li)
    lv = lax.broadcasted_iota(jnp.int32, (L, RET_V_DIM), 0).astype(F32)
    xi_f[...] = jnp.exp(lgf * (lv + 1.0))
    xi_b[...] = jnp.exp(lgb * (L - lv))
    gl_f[...] = jnp.exp(jnp.full((L, RET_V_DIM), L, F32) * lgf)
    gl_b[...] = jnp.exp(jnp.full((L, RET_V_DIM), L, F32) * lgb)

    tn = (((0,), (0,)), ((), ()))
    nt = (((1,), (1,)), ((), ()))

    def advance(state, gl, zeta, k, v):
        kz = (k.astype(F32) * zeta[...]).astype(BF16)
        return gl[...] * state + lax.dot_general(kz, v, tn, preferred_element_type=F32)

    sf = jnp.zeros((RET_QK_DIM, RET_V_DIM), F32)
    sb = jnp.zeros((RET_QK_DIM, RET_V_DIM), F32)
    for c in range(c_chunks):
        sf = advance(sf, gl_f, zeta_f, ck_ref[0, c * L:(c + 1) * L, :], cv_ref[0, c * L:(c + 1) * L, :])
    for c in reversed(range(c_chunks)):
        sb = advance(sb, gl_b, zeta_b, ck_ref[0, c * L:(c + 1) * L, :], cv_ref[0, c * L:(c + 1) * L, :])

    chunk = lambda c: pl.ds(pl.multiple_of(c * L, L), L)

    def bwd_states(i, sb):
        cb = n_chunks - 1 - i
        sb_scr[cb] = sb.astype(BF16)
        return advance(sb, gl_b, zeta_b, k_ref[0, chunk(cb), :], v_ref[0, chunk(cb), :])

    lax.fori_loop(0, n_chunks, bwd_states, sb, unroll=RET_UNROLL)

    def probs(c):
        rows = chunk(c)
        a = lax.dot_general(q_ref[0, rows, :], k_ref[0, rows, :], nt, preferred_element_type=F32)
        p_scr[rows, :] = (a * dmat[...]).astype(BF16)

    def outs(c, sf):
        rows = chunk(c)
        q = q_ref[0, rows, :]
        v = v_ref[0, rows, :]
        o_acc[rows, :] = (jnp.dot(p_scr[rows, :], v, preferred_element_type=F32)
                          + jnp.dot(q, sf.astype(BF16), preferred_element_type=F32) * xi_f[...]
                          + jnp.dot(q, sb_scr[c], preferred_element_type=F32) * xi_b[...])
        return advance(sf, gl_f, zeta_f, k_ref[0, rows, :], v)

    def norm(c):
        rows = chunk(c)
        o = o_acc[rows, :]
        mu = jnp.mean(o, axis=-1, keepdims=True)
        d = o - mu
        var = jnp.mean(d * d, axis=-1, keepdims=True)
        o_ref[0, rows, :] = (d * lax.rsqrt(var + EPS) * g_ref[0, rows, :].astype(F32)).astype(BF16)

    n_blk = n_chunks // RET_BLOCK
    block = lambda t: [t * RET_BLOCK + i for i in range(RET_BLOCK)]

    def outs_block(t, sf):
        for c in block(t):
            sf = outs(c, sf)
        return sf

    for c in block(0):
        probs(c)
    sf = outs_block(0, sf)
    for c in block(1):
        probs(c)

    def step(t, sf):
        for c in block(t - 2):
            norm(c)
        sf = outs_block(t - 1, sf)
        for c in block(t):
            probs(c)
        return sf

    sf = lax.fori_loop(2, n_blk, step, sf)
    for c in block(n_blk - 2):
        norm(c)
    outs_block(n_blk - 1, sf)
    for c in block(n_blk - 1):
        norm(c)


def _ret(lgf, lgb, rqk, rv, rg, crk, crv):
    b, n, _ = rqk.shape
    ctx = crk.shape[1]
    L = RET_CHUNK
    smem = pl.BlockSpec(memory_space=pltpu.SMEM)
    qk = lambda t, off: pl.BlockSpec((1, t, RET_QK_DIM), lambda bi, h: (bi, 0, off + h))
    vv = lambda t: pl.BlockSpec((1, t, RET_V_DIM), lambda bi, h: (bi, 0, h))
    vm = lambda r, c: pltpu.VMEM((r, c), F32)
    st = pltpu.VMEM((n // L, RET_QK_DIM, RET_V_DIM), BF16)
    return pl.pallas_call(
        _ret_kernel,
        grid=(b, RET_HEADS),
        in_specs=[smem, smem, qk(n, 0), qk(n, RET_HEADS), vv(n), vv(n), qk(ctx, 0), vv(ctx)],
        out_specs=vv(n),
        out_shape=jax.ShapeDtypeStruct((b, n, RET_V_W), BF16),
        scratch_shapes=[vm(L, L), vm(L, RET_V_DIM), vm(L, RET_V_DIM), vm(L, L), vm(L, L),
                        vm(L, RET_V_DIM), vm(L, RET_V_DIM), st,
                        pltpu.VMEM((n, L), BF16), vm(n, RET_V_DIM)],
        compiler_params=_cparams(2),
        name="ret",
    )(lgf, lgb, rqk, rqk, rv, rg, crk, crv)


def _merge_kernel(x_ref, ona_ref, oret_ref, gate_ref, g1_ref, wna_ref, wret_ref, wout_ref, o_ref):
    a = jnp.dot(ona_ref[0], wna_ref[...], preferred_element_type=F32)
    r = jnp.dot(oret_ref[0], wret_ref[...], preferred_element_type=F32)
    merged = (gate_ref[0, :, :D_MODEL].astype(F32) * a
              + gate_ref[0, :, D_MODEL:].astype(F32) * r).astype(BF16)
    y = jnp.dot(merged, wout_ref[...], preferred_element_type=F32)
    o_ref[0] = x_ref[0] + g1_ref[0] * y


def _merge(x, o_na, o_ret, gates, g1, w_na, w_ret, w_out):
    b, n, _ = x.shape
    tm = TM_PROJ
    tok = lambda w: pl.BlockSpec((1, tm, w), lambda bi, i: (bi, i, 0))
    return pl.pallas_call(
        _merge_kernel,
        grid=(b, n // tm),
        in_specs=[tok(D_MODEL), tok(NA_W), tok(RET_V_W), tok(2 * D_MODEL),
                  pl.BlockSpec((1, 1, D_MODEL), lambda bi, i: (bi, 0, 0)),
                  _resident((NA_W, D_MODEL)), _resident((RET_V_W, D_MODEL)),
                  _resident((D_MODEL, D_MODEL))],
        out_specs=tok(D_MODEL),
        out_shape=jax.ShapeDtypeStruct((b, n, D_MODEL), F32),
        compiler_params=_cparams(2),
        name="merge",
    )(x, o_na, o_ret, gates, g1, w_na, w_ret, w_out)


def _ffn_kernel(x_ref, xp_ref, xn_ref, sh_ref, sc_ref, g2_ref, gn_ref, gf_ref,
                wup_ref, cw_ref, cb_ref, wdown_ref, o_ref, h_scr, u_scr, y_scr):
    i = pl.program_id(1)
    last = pl.num_programs(1) - 1
    tm = x_ref.shape[1]
    rows = tm + 2 * HALO
    n_ch = D_FF // FF_CHUNK
    x = x_ref[0]
    xa = jnp.concatenate([xp_ref[0], x, xn_ref[0]], axis=0)
    h = _rms_modulate(xa, gn_ref[...], sh_ref[0], sc_ref[0])
    ridx = lax.broadcasted_iota(jnp.int32, (rows, 1), 0)
    valid = jnp.logical_and(jnp.logical_or(ridx >= HALO, i > 0),
                            jnp.logical_or(ridx < tm + HALO, i < last))
    h_scr[...] = jnp.where(valid, h, 0.0).astype(BF16)

    def up(j, slot):
        for g in range(2):
            c0 = g * D_FF + j * FF_CHUNK
            u_scr[slot, g] = jnp.dot(h_scr[...], wup_ref[:, c0:c0 + FF_CHUNK], preferred_element_type=F32)

    def act(j, slot):
        def conv(g):
            c0 = g * D_FF + j * FF_CHUNK
            u = u_scr[slot, g]
            w = cw_ref[:, c0:c0 + FF_CHUNK]
            y = (pltpu.roll(u, 1, 0) * w[0:1] + u * w[1:2] + pltpu.roll(u, rows - 1, 0) * w[2:3]
                 + cb_ref[:, c0:c0 + FF_CHUNK])
            return y[HALO:HALO + tm]

        y_scr[:, j * FF_CHUNK:(j + 1) * FF_CHUNK] = (jax.nn.silu(conv(0)) * conv(1)).astype(BF16)

    def down(j0, j1):
        c0, c1 = j0 * FF_CHUNK, j1 * FF_CHUNK
        return jnp.dot(y_scr[:, c0:c1], wdown_ref[c0:c1, :], preferred_element_type=F32)

    up(0, 0)
    up(1, 1)
    acc = None
    for j in range(n_ch):
        act(j, j % 3)
        if j + 2 < n_ch:
            up(j + 2, (j + 2) % 3)
        if j % 3 == 0 and j > 0:
            part = down(j - 3, j)
            acc = part if acc is None else acc + part
    j0 = (n_ch - 1) // 3 * 3
    acc = acc + down(j0, n_ch)
    z = x + g2_ref[0] * acc
    o_ref[0] = z * lax.rsqrt(jnp.mean(z * z, axis=-1, keepdims=True) + EPS) * gf_ref[...]


def _ffn(x, sh2, sc2, g2, gain2, gain_f, w_up, conv_w, conv_b, w_down):
    b, n, _ = x.shape
    tm = TM_FFN
    per = tm // HALO
    n_halo = n // HALO
    rows = tm + 2 * HALO
    tok = pl.BlockSpec((1, tm, D_MODEL), lambda bi, i: (bi, i, 0))
    mod = pl.BlockSpec((1, 1, D_MODEL), lambda bi, i: (bi, 0, 0))
    prev = pl.BlockSpec((1, HALO, D_MODEL), lambda bi, i: (bi, jnp.maximum(i * per - 1, 0), 0))
    nxt = pl.BlockSpec((1, HALO, D_MODEL), lambda bi, i: (bi, jnp.minimum((i + 1) * per, n_halo - 1), 0))
    return pl.pallas_call(
        _ffn_kernel,
        grid=(b, n // tm),
        in_specs=[tok, prev, nxt, mod, mod, mod,
                  _resident((1, D_MODEL)), _resident((1, D_MODEL)),
                  _resident(w_up.shape), _resident(conv_w.shape),
                  _resident(conv_b.shape), _resident((D_FF, D_MODEL))],
        out_specs=tok,
        out_shape=jax.ShapeDtypeStruct((b, n, D_MODEL), F32),
        scratch_shapes=[pltpu.VMEM((rows, D_MODEL), BF16),
                        pltpu.VMEM((3, 2, rows, FF_CHUNK), F32),
                        pltpu.VMEM((tm, D_FF), BF16)],
        compiler_params=_cparams(2),
        name="ffn",
    )(x, x, x, sh2, sc2, g2, gain2, gain_f, w_up, conv_w, conv_b, w_down)


def _rope_tables(n):
    nf = RET_QK_DIM // 4
    inv = (ROPE_BASE ** (-np.arange(nf, dtype=np.float32) / nf)).astype(np.float32)
    t = np.arange(n)
    ang_r = (t // GRID_W).astype(np.float32)[:, None] * inv[None, :]
    ang_c = (t % GRID_W).astype(np.float32)[:, None] * inv[None, :]
    cos = np.concatenate([np.cos(ang_r)] * 2 + [np.cos(ang_c)] * 2, axis=-1)
    zero = np.zeros((n, nf), np.float32)
    s1 = np.concatenate([-np.sin(ang_r), zero, -np.sin(ang_c), zero], axis=-1)
    s2 = np.concatenate([zero, np.sin(ang_r), zero, np.sin(ang_c)], axis=-1)
    return tuple(jnp.asarray(a.astype(np.float32)) for a in (cos, s1, s2))


def _na_bias_table(rpb):
    w, kh, kw = GRID_W, NA_WIN_H, NA_WIN_W
    cols = np.arange(w)
    c_start = np.clip(cols - kw // 2, 0, w - kw)
    kcol = np.arange(w)
    valid = (kcol[None, :] >= c_start[:, None]) & (kcol[None, :] < c_start[:, None] + kw)
    dc = kcol[None, :] - cols[:, None] + (kw - 1)
    onehot = ((dc[:, :, None] == np.arange(2 * kw - 1)) & valid[:, :, None]).astype(np.float32)
    base = jnp.einsum('hrd,ckd->hcrk', rpb, onehot, precision=lax.Precision.HIGHEST)
    base = base + np.where(valid, 0.0, MASK_VALUE).astype(np.float32)[:, None, :]
    return jnp.stack([base[:, :, v:v + kh].reshape(rpb.shape[0], w, kh * w) for v in range(kh)], axis=1)


def kernel(x, c, ctx, c_ctx, w_ada, b_ada, norm1_g, w_in, na_rpb, ret_log_decay_fwd, ret_log_decay_bwd,
           w_proj_na, w_proj_ret, w_out, norm2_g, w_up, conv_w, conv_b, w_down, final_norm_g):
    b, n, d = x.shape
    assert w_ada.shape[0] == 1, "single layer"
    row = lambda v: v.reshape(1, -1)

    cc = jnp.concatenate([c, c_ctx[None], jnp.zeros((ADA_ROWS - b - 1, d), F32)], axis=0)
    mod = _ada(cc, w_ada[0], row(b_ada[0]))
    xmod = mod[:b].reshape(b, 1, 6, d)
    sh1, sc1, g1, sh2, sc2, g2 = (xmod[:, :, i] for i in range(6))
    cmod = mod[b:b + 1].reshape(1, 1, 6, d)
    csh1, csc1 = cmod[:, :, 0], cmod[:, :, 1]

    w_in_b = w_in[0].astype(BF16)
    cos, s1, s2 = _rope_tables(n)
    qkv, rqk, rv, rg, gates = _inproj(x, sh1, sc1, row(norm1_g[0]), w_in_b, cos, s1, s2)
    ckv, crk, crv = _inproj_ctx(ctx, csh1, csc1, row(norm1_g[0]), w_in_b)

    o_na = _na(qkv, ckv, _na_bias_table(na_rpb[0]))
    o_ret = _ret(ret_log_decay_fwd[0], ret_log_decay_bwd[0], rqk, rv, rg, crk, crv)

    x1 = _merge(x, o_na, o_ret, gates, g1, w_proj_na[0].astype(BF16), w_proj_ret[0].astype(BF16),
                w_out[0].astype(BF16))
    return _ffn(x1, sh2, sc2, g2, row(norm2_g[0]), row(final_norm_g), w_up[0].astype(BF16),
                conv_w[0], row(conv_b[0]), w_down[0].astype(BF16))
```

```python
import math

import numpy as np
import jax
import jax.numpy as jnp
from jax import lax
from jax.experimental import pallas as pl
from jax.experimental.pallas import tpu as pltpu

F32 = jnp.float32
BF16 = jnp.bfloat16

D_MODEL = 1024
GRID_W = 64
NA_HEADS = 8
NA_HEAD_DIM = 64
NA_WIN_H = 8
NA_WIN_W = 16
RET_HEADS = 4
RET_QK_DIM = 128
RET_V_DIM = 256
RET_CHUNK = 128
D_FF = int(math.ceil(8 * D_MODEL / 3 / 128)) * 128
CONV_W = 3
ROPE_BASE = 10000.0
EPS = 1e-6

NA_W = NA_HEADS * NA_HEAD_DIM
RET_QK_W = RET_HEADS * RET_QK_DIM
RET_V_W = RET_HEADS * RET_V_DIM
OFF_Q, OFF_K, OFF_V = 0, NA_W, 2 * NA_W
OFF_RQ = 3 * NA_W
OFF_RK = OFF_RQ + RET_QK_W
OFF_RV = OFF_RK + RET_QK_W
OFF_RG = OFF_RV + RET_V_W
OFF_GA = OFF_RG + RET_V_W
OFF_GB = OFF_GA + D_MODEL
IN_COLS = OFF_GB + D_MODEL

NA_SCALE = NA_HEAD_DIM ** -0.5
RK_SCALE = RET_QK_DIM ** -0.5
MASK_VALUE = -1e30

VMEM_LIMIT_BYTES = 56 * 1024 * 1024
LANES = 128
HALO = 8

TM_PROJ = 512
TM_FFN = 512
ADA_ROWS = 16
ADA_TN = 1536
FF_CHUNK = 256
PROJ_CHUNK = 512
NA_BLOCK = 8
RET_UNROLL = 4
RET_BLOCK = 4


def _cparams(n_axes):
    return pltpu.CompilerParams(dimension_semantics=("arbitrary",) * n_axes,
                                vmem_limit_bytes=VMEM_LIMIT_BYTES)


def _resident(shape):
    nd = len(shape)
    return pl.BlockSpec(shape, lambda *_: (0,) * nd, pipeline_mode=pl.Buffered(1))


def _rms_modulate(x, gain, shift, scale):
    y = x * lax.rsqrt(jnp.mean(x * x, axis=-1, keepdims=True) + EPS)
    return (y * gain) * (1.0 + scale) + shift


def _ada_kernel(c_ref, w_ref, b_ref, o_ref):
    s = jax.nn.silu(c_ref[...])
    o_ref[...] = jnp.dot(s.astype(BF16), w_ref[...].astype(BF16), preferred_element_type=F32) + b_ref[...]


def _ada(cc, w_ada, b_ada):
    n_out = w_ada.shape[1]
    return pl.pallas_call(
        _ada_kernel,
        grid=(n_out // ADA_TN,),
        in_specs=[pl.BlockSpec((ADA_ROWS, D_MODEL), lambda j: (0, 0)),
                  pl.BlockSpec((D_MODEL, ADA_TN), lambda j: (0, j)),
                  pl.BlockSpec((1, ADA_TN), lambda j: (0, j))],
        out_specs=pl.BlockSpec((ADA_ROWS, ADA_TN), lambda j: (0, j)),
        out_shape=jax.ShapeDtypeStruct((ADA_ROWS, n_out), F32),
        compiler_params=_cparams(1),
        name="ada",
    )(cc, w_ada, b_ada)


def _inproj_kernel(x_ref, sh_ref, sc_ref, g_ref, w_ref, cs_ref, s1_ref, s2_ref,
                   qkv_ref, rqk_ref, rv_ref, rg_ref, gate_ref):
    h = _rms_modulate(x_ref[0], g_ref[...], sh_ref[0], sc_ref[0]).astype(BF16)

    def proj(c0):
        return jnp.dot(h, w_ref[:, c0:c0 + PROJ_CHUNK], preferred_element_type=F32)

    for j in range(3 * NA_W // PROJ_CHUNK):
        r = proj(OFF_Q + j * PROJ_CHUNK)
        if j * PROJ_CHUNK < NA_W:
            r = r * NA_SCALE
        qkv_ref[0, :, j * PROJ_CHUNK:(j + 1) * PROJ_CHUNK] = r.astype(BF16)

    cs, s1, s2 = cs_ref[...], s1_ref[...], s2_ref[...]
    quarter = RET_QK_DIM // 4
    for j in range(2 * RET_QK_W // PROJ_CHUNK):
        r = proj(OFF_RQ + j * PROJ_CHUNK)
        for hh in range(PROJ_CHUNK // RET_QK_DIM):
            t = r[:, hh * RET_QK_DIM:(hh + 1) * RET_QK_DIM]
            t = (t * cs + pltpu.roll(t, RET_QK_DIM - quarter, 1) * s1
                 + pltpu.roll(t, quarter, 1) * s2)
            if j * PROJ_CHUNK >= RET_QK_W:
                t = t * RK_SCALE
            c0 = j * PROJ_CHUNK + hh * RET_QK_DIM
            rqk_ref[0, :, c0:c0 + RET_QK_DIM] = t.astype(BF16)

    for j in range(RET_V_W // PROJ_CHUNK):
        rv_ref[0, :, j * PROJ_CHUNK:(j + 1) * PROJ_CHUNK] = proj(OFF_RV + j * PROJ_CHUNK).astype(BF16)
    for j in range(RET_V_W // PROJ_CHUNK):
        r = proj(OFF_RG + j * PROJ_CHUNK)
        rg_ref[0, :, j * PROJ_CHUNK:(j + 1) * PROJ_CHUNK] = jax.nn.silu(r).astype(BF16)
    for j in range(2 * D_MODEL // PROJ_CHUNK):
        r = proj(OFF_GA + j * PROJ_CHUNK)
        gate_ref[0, :, j * PROJ_CHUNK:(j + 1) * PROJ_CHUNK] = jax.nn.sigmoid(r).astype(BF16)


def _inproj(x, shift, scale, gain, w_in, cs, s1, s2):
    b, t, _ = x.shape
    tm = TM_PROJ
    mod_map = lambda bi, i: (bi, 0, 0)
    tok = lambda w: pl.BlockSpec((1, tm, w), lambda bi, i: (bi, i, 0))
    tab = pl.BlockSpec((tm, RET_QK_DIM), lambda bi, i: (i, 0))
    out_w = (3 * NA_W, 2 * RET_QK_W, RET_V_W, RET_V_W, 2 * D_MODEL)
    return pl.pallas_call(
        _inproj_kernel,
        grid=(b, t // tm),
        in_specs=[tok(D_MODEL),
                  pl.BlockSpec((1, 1, D_MODEL), mod_map),
                  pl.BlockSpec((1, 1, D_MODEL), mod_map),
                  _resident((1, D_MODEL)),
                  _resident((D_MODEL, IN_COLS)),
                  tab, tab, tab],
        out_specs=[tok(w) for w in out_w],
        out_shape=[jax.ShapeDtypeStruct((b, t, w), BF16) for w in out_w],
        compiler_params=_cparams(2),
        name="inproj",
    )(x, shift, scale, gain, w_in, cs, s1, s2)


def _inproj_ctx_kernel(x_ref, sh_ref, sc_ref, g_ref, w_ref, kv_ref, rk_ref, rv_ref):
    h = _rms_modulate(x_ref[0], g_ref[...], sh_ref[0], sc_ref[0]).astype(BF16)

    def proj(c0):
        return jnp.dot(h, w_ref[:, c0:c0 + PROJ_CHUNK], preferred_element_type=F32)

    for j in range(2 * NA_W // PROJ_CHUNK):
        kv_ref[0, :, j * PROJ_CHUNK:(j + 1) * PROJ_CHUNK] = proj(OFF_K + j * PROJ_CHUNK).astype(BF16)
    for j in range(RET_QK_W // PROJ_CHUNK):
        rk_ref[0, :, j * PROJ_CHUNK:(j + 1) * PROJ_CHUNK] = (
            proj(OFF_RK + j * PROJ_CHUNK) * RK_SCALE).astype(BF16)
    for j in range(RET_V_W // PROJ_CHUNK):
        rv_ref[0, :, j * PROJ_CHUNK:(j + 1) * PROJ_CHUNK] = proj(OFF_RV + j * PROJ_CHUNK).astype(BF16)


def _inproj_ctx(ctx, shift, scale, gain, w_in):
    b, t, _ = ctx.shape
    tok = lambda w: pl.BlockSpec((1, t, w), lambda bi: (bi, 0, 0))
    mod = pl.BlockSpec((1, 1, D_MODEL), lambda bi: (0, 0, 0))
    out_w = (2 * NA_W, RET_QK_W, RET_V_W)
    return pl.pallas_call(
        _inproj_ctx_kernel,
        grid=(b,),
        in_specs=[tok(D_MODEL), mod, mod, _resident((1, D_MODEL)), _resident((D_MODEL, IN_COLS))],
        out_specs=[tok(w) for w in out_w],
        out_shape=[jax.ShapeDtypeStruct((b, t, w), BF16) for w in out_w],
        compiler_params=_cparams(1),
        name="inproj_ctx",
    )(ctx, shift, scale, gain, w_in)


def _na_kernel(q_ref, k_ref, v_ref, kc_ref, vc_ref, bias_ref, o_ref, vext, vcext, s_scr, m_scr, p_scr, oc_scr):
    n_rows = q_ref.shape[1] // GRID_W
    n_blk = n_rows // NA_BLOCK
    win = NA_WIN_H * GRID_W
    width = s_scr.shape[2]
    pair = 2 * GRID_W
    first = lax.broadcasted_iota(jnp.int32, (GRID_W, LANES), 1) < NA_HEAD_DIM
    nt = (((1,), (1,)), ((), ()))
    kc = kc_ref[0]
    vext[:, :LANES] = v_ref[0]
    vext[:, LANES:] = jnp.ones((vext.shape[0], LANES), BF16)
    vcext[:, :LANES] = vc_ref[0]
    vcext[:, LANES:] = jnp.ones((vcext.shape[0], LANES), BF16)

    def window(r):
        rs = jnp.clip(r - NA_WIN_H // 2, 0, n_rows - NA_WIN_H)
        return (pl.ds(pl.multiple_of(r * GRID_W, GRID_W), GRID_W),
                pl.ds(pl.multiple_of(rs * GRID_W, GRID_W), win), rs - r + (NA_WIN_H - 1))

    def score_stage(blk, slot):
        base = blk * NA_BLOCK
        stacked = []
        for k in range(NA_BLOCK):
            q = q_ref[0, window(base + k)[0], :]
            zero = jnp.zeros_like(q)
            stacked.append(jnp.concatenate([jnp.where(first, q, zero), jnp.where(first, zero, q)], axis=0))
        s_ctx = lax.dot_general(jnp.concatenate(stacked, axis=0), kc, nt, preferred_element_type=F32)
        s_scr[slot, :, win:] = s_ctx
        for k in range(NA_BLOCK):
            _, wrows, var = window(base + k)
            s_loc = lax.dot_general(stacked[k], k_ref[0, wrows, :], nt, preferred_element_type=F32)
            for hh in range(2):
                dst = slice(k * pair + hh * GRID_W, k * pair + (hh + 1) * GRID_W)
                sl = s_loc[hh * GRID_W:(hh + 1) * GRID_W] + bias_ref[hh, var]
                s_scr[slot, dst, :win] = sl
                mx = jnp.maximum(jnp.max(sl, axis=-1, keepdims=True),
                                 jnp.max(s_ctx[dst], axis=-1, keepdims=True))
                m_scr[slot, dst, :] = jnp.broadcast_to(mx, (GRID_W, LANES))

    def prob_stage(slot):
        for k in range(NA_BLOCK):
            dst = slice(k * pair, (k + 1) * pair)
            mx = m_scr[slot, dst, :]
            for j in range(width // LANES):
                cols = slice(j * LANES, (j + 1) * LANES)
                p_scr[slot, dst, cols] = jnp.exp(s_scr[slot, dst, cols] - mx).astype(BF16)

    def out_stage(blk, slot):
        oc_scr[...] = jnp.dot(p_scr[slot, :, win:], vcext[...], preferred_element_type=F32)
        for k in range(NA_BLOCK):
            rows, wrows, _ = window(blk * NA_BLOCK + k)
            dst = slice(k * pair, (k + 1) * pair)
            o = (jnp.dot(p_scr[slot, dst, :win], vext[wrows, :], preferred_element_type=F32)
                 + oc_scr[dst, :])
            num = jnp.where(first, o[:GRID_W, :LANES], o[GRID_W:, :LANES])
            den = jnp.where(first, o[:GRID_W, LANES:], o[GRID_W:, LANES:])
            o_ref[0, rows, :] = (num / den).astype(BF16)

    assert n_blk % 2 == 0 and n_blk >= 4
    score_stage(0, 0)
    score_stage(1, 1)
    prob_stage(0)

    def step(t, carry):
        blk = 2 * t
        out_stage(blk, 0)
        prob_stage(1)
        score_stage(blk + 2, 0)
        out_stage(blk + 1, 1)
        prob_stage(0)
        score_stage(blk + 3, 1)
        return carry

    lax.fori_loop(0, n_blk // 2 - 1, step, 0)
    out_stage(n_blk - 2, 0)
    prob_stage(1)
    out_stage(n_blk - 1, 1)


def _na(qkv, ckv, bias):
    b, n, _ = qkv.shape
    ctx = ckv.shape[1]
    pairs = NA_W // LANES
    win = NA_WIN_H * GRID_W
    stage_rows = NA_BLOCK * 2 * GRID_W
    blk = lambda t, off: pl.BlockSpec((1, t, LANES), lambda p, bi: (bi, 0, off + p))
    return pl.pallas_call(
        _na_kernel,
        grid=(pairs, b),
        in_specs=[blk(n, 0), blk(n, pairs), blk(n, 2 * pairs),
                  blk(ctx, 0), blk(ctx, pairs),
                  pl.BlockSpec((2, NA_WIN_H, GRID_W, win), lambda p, bi: (p, 0, 0, 0))],
        out_specs=pl.BlockSpec((1, n, LANES), lambda p, bi: (bi, 0, p)),
        out_shape=jax.ShapeDtypeStruct((b, n, NA_W), BF16),
        scratch_shapes=[pltpu.VMEM((n, 2 * LANES), BF16), pltpu.VMEM((ctx, 2 * LANES), BF16),
                        pltpu.VMEM((2, stage_rows, win + ctx), F32), pltpu.VMEM((2, stage_rows, LANES), F32),
                        pltpu.VMEM((2, stage_rows, win + ctx), BF16), pltpu.VMEM((stage_rows, 2 * LANES), F32)],
        compiler_params=_cparams(2),
        name="na",
    )(qkv, qkv, qkv, ckv, ckv, bias)


def _ret_kernel(lgf_ref, lgb_ref, q_ref, k_ref, v_ref, g_ref, ck_ref, cv_ref, o_ref,
                dmat, xi_f, xi_b, zeta_f, zeta_b, gl_f, gl_b, sb_scr, p_scr, o_acc):
    L = RET_CHUNK
    hd = pl.program_id(1)
    lgf = lgf_ref[hd]
    lgb = lgb_ref[hd]
    n_chunks = q_ref.shape[1] // L
    c_chunks = ck_ref.shape[1] // L

    li = lax.broadcasted_iota(jnp.int32, (L, L), 0).astype(F32)
    mi = lax.broadcasted_iota(jnp.int32, (L, L), 1).astype(F32)
    diff = li - mi
    dmat[...] = jnp.where(diff >= 0.0, jnp.exp(lgf * jnp.maximum(diff, 0.0)),
                          jnp.exp(lgb * jnp.maximum(-diff, 0.0)))
    zeta_f[...] = jnp.exp(lgf * (L - 1.0 - li))
    zeta_b[...] = jnp.exp(lgb * li)
    lv = lax.broadcasted_iota(jnp.int32, (L, RET_V_DIM), 0).astype(F32)
    xi_f[...] = jnp.exp(lgf * (lv + 1.0))
    xi_b[...] = jnp.exp(lgb * (L - lv))
    gl_f[...] = jnp.exp(jnp.full((L, RET_V_DIM), L, F32) * lgf)
    gl_b[...] = jnp.exp(jnp.full((L, RET_V_DIM), L, F32) * lgb)

    tn = (((0,), (0,)), ((), ()))
    nt = (((1,), (1,)), ((), ()))

    def advance(state, gl, zeta, k, v):
        kz = (k.astype(F32) * zeta[...]).astype(BF16)
        return gl[...] * state + lax.dot_general(kz, v, tn, preferred_element_type=F32)

    sf = jnp.zeros((RET_QK_DIM, RET_V_DIM), F32)
    sb = jnp.zeros((RET_QK_DIM, RET_V_DIM), F32)
    for c in range(c_chunks):
        sf = advance(sf, gl_f, zeta_f, ck_ref[0, c * L:(c + 1) * L, :], cv_ref[0, c * L:(c + 1) * L, :])
    for c in reversed(range(c_chunks)):
        sb = advance(sb, gl_b, zeta_b, ck_ref[0, c * L:(c + 1) * L, :], cv_ref[0, c * L:(c + 1) * L, :])

    chunk = lambda c: pl.ds(pl.multiple_of(c * L, L), L)

    def bwd_states(i, sb):
        cb = n_chunks - 1 - i
        sb_scr[cb] = sb.astype(BF16)
        return advance(sb, gl_b, zeta_b, k_ref[0, chunk(cb), :], v_ref[0, chunk(cb), :])

    lax.fori_loop(0, n_chunks, bwd_states, sb, unroll=RET_UNROLL)

    def probs(c):
        rows = chunk(c)
        a = lax.dot_general(q_ref[0, rows, :], k_ref[0, rows, :], nt, preferred_element_type=F32)
        p_scr[rows, :] = (a * dmat[...]).astype(BF16)

    def outs(c, sf):
        rows = chunk(c)
        q = q_ref[0, rows, :]
        v = v_ref[0, rows, :]
        o_acc[rows, :] = (jnp.dot(p_scr[rows, :], v, preferred_element_type=F32)
                          + jnp.dot(q, sf.astype(BF16), preferred_element_type=F32) * xi_f[...]
                          + jnp.dot(q, sb_scr[c], preferred_element_type=F32) * xi_b[...])
        return advance(sf, gl_f, zeta_f, k_ref[0, rows, :], v)

    def norm(c):
        rows = chunk(c)
        o = o_acc[rows, :]
        mu = jnp.mean(o, axis=-1, keepdims=True)
        d = o - mu
        var = jnp.mean(d * d, axis=-1, keepdims=True)
        o_ref[0, rows, :] = (d * lax.rsqrt(var + EPS) * g_ref[0, rows, :].astype(F32)).astype(BF16)

    n_blk = n_chunks // RET_BLOCK
    block = lambda t: [t * RET_BLOCK + i for i in range(RET_BLOCK)]

    def outs_block(t, sf):
        for c in block(t):
            sf = outs(c, sf)
        return sf

    for c in block(0):
        probs(c)
    sf = outs_block(0, sf)
    for c in block(1):
        probs(c)

    def step(t, sf):
        for c in block(t - 2):
            norm(c)
        sf = outs_block(t - 1, sf)
        for c in block(t):
            probs(c)
        return sf

    sf = lax.fori_loop(2, n_blk, step, sf)
    for c in block(n_blk - 2):
        norm(c)
    outs_block(n_blk - 1, sf)
    for c in block(n_blk - 1):
        norm(c)


def _ret(lgf, lgb, rqk, rv, rg, crk, crv):
    b, n, _ = rqk.shape
    ctx = crk.shape[1]
    L = RET_CHUNK
    smem = pl.BlockSpec(memory_space=pltpu.SMEM)
    qk = lambda t, off: pl.BlockSpec((1, t, RET_QK_DIM), lambda bi, h: (bi, 0, off + h))
    vv = lambda t: pl.BlockSpec((1, t, RET_V_DIM), lambda bi, h: (bi, 0, h))
    vm = lambda r, c: pltpu.VMEM((r, c), F32)
    st = pltpu.VMEM((n // L, RET_QK_DIM, RET_V_DIM), BF16)
    return pl.pallas_call(
        _ret_kernel,
        grid=(b, RET_HEADS),
        in_specs=[smem, smem, qk(n, 0), qk(n, RET_HEADS), vv(n), vv(n), qk(ctx, 0), vv(ctx)],
        out_specs=vv(n),
        out_shape=jax.ShapeDtypeStruct((b, n, RET_V_W), BF16),
        scratch_shapes=[vm(L, L), vm(L, RET_V_DIM), vm(L, RET_V_DIM), vm(L, L), vm(L, L),
                        vm(L, RET_V_DIM), vm(L, RET_V_DIM), st,
                        pltpu.VMEM((n, L), BF16), vm(n, RET_V_DIM)],
        compiler_params=_cparams(2),
        name="ret",
    )(lgf, lgb, rqk, rqk, rv, rg, crk, crv)


def _merge_kernel(x_ref, ona_ref, oret_ref, gate_ref, g1_ref, wna_ref, wret_ref, wout_ref, o_ref):
    a = jnp.dot(ona_ref[0], wna_ref[...], preferred_element_type=F32)
    r = jnp.dot(oret_ref[0], wret_ref[...], preferred_element_type=F32)
    merged = (gate_ref[0, :, :D_MODEL].astype(F32) * a
              + gate_ref[0, :, D_MODEL:].astype(F32) * r).astype(BF16)
    y = jnp.dot(merged, wout_ref[...], preferred_element_type=F32)
    o_ref[0] = x_ref[0] + g1_ref[0] * y


def _merge(x, o_na, o_ret, gates, g1, w_na, w_ret, w_out):
    b, n, _ = x.shape
    tm = TM_PROJ
    tok = lambda w: pl.BlockSpec((1, tm, w), lambda bi, i: (bi, i, 0))
    return pl.pallas_call(
        _merge_kernel,
        grid=(b, n // tm),
        in_specs=[tok(D_MODEL), tok(NA_W), tok(RET_V_W), tok(2 * D_MODEL),
                  pl.BlockSpec((1, 1, D_MODEL), lambda bi, i: (bi, 0, 0)),
                  _resident((NA_W, D_MODEL)), _resident((RET_V_W, D_MODEL)),
                  _resident((D_MODEL, D_MODEL))],
        out_specs=tok(D_MODEL),
        out_shape=jax.ShapeDtypeStruct((b, n, D_MODEL), F32),
        compiler_params=_cparams(2),
        name="merge",
    )(x, o_na, o_ret, gates, g1, w_na, w_ret, w_out)


def _ffn_kernel(x_ref, xp_ref, xn_ref, sh_ref, sc_ref, g2_ref, gn_ref, gf_ref,
                wup_ref, cw_ref, cb_ref, wdown_ref, o_ref, h_scr, u_scr, y_scr):
    i = pl.program_id(1)
    last = pl.num_programs(1) - 1
    tm = x_ref.shape[1]
    rows = tm + 2 * HALO
    n_ch = D_FF // FF_CHUNK
    x = x_ref[0]
    xa = jnp.concatenate([xp_ref[0], x, xn_ref[0]], axis=0)
    h = _rms_modulate(xa, gn_ref[...], sh_ref[0], sc_ref[0])
    ridx = lax.broadcasted_iota(jnp.int32, (rows, 1), 0)
    valid = jnp.logical_and(jnp.logical_or(ridx >= HALO, i > 0),
                            jnp.logical_or(ridx < tm + HALO, i < last))
    h_scr[...] = jnp.where(valid, h, 0.0).astype(BF16)

    def up(j, slot):
        for g in range(2):
            c0 = g * D_FF + j * FF_CHUNK
            u_scr[slot, g] = jnp.dot(h_scr[...], wup_ref[:, c0:c0 + FF_CHUNK], preferred_element_type=F32)

    def act(j, slot):
        def conv(g):
            c0 = g * D_FF + j * FF_CHUNK
            u = u_scr[slot, g]
            w = cw_ref[:, c0:c0 + FF_CHUNK]
            y = (pltpu.roll(u, 1, 0) * w[0:1] + u * w[1:2] + pltpu.roll(u, rows - 1, 0) * w[2:3]
                 + cb_ref[:, c0:c0 + FF_CHUNK])
            return y[HALO:HALO + tm]

        y_scr[:, j * FF_CHUNK:(j + 1) * FF_CHUNK] = (jax.nn.silu(conv(0)) * conv(1)).astype(BF16)

    def down(j0, j1):
        c0, c1 = j0 * FF_CHUNK, j1 * FF_CHUNK
        return jnp.dot(y_scr[:, c0:c1], wdown_ref[c0:c1, :], preferred_element_type=F32)

    up(0, 0)
    up(1, 1)
    acc = None
    for j in range(n_ch):
        act(j, j % 3)
        if j + 2 < n_ch:
            up(j + 2, (j + 2) % 3)
        if j % 3 == 0 and j > 0:
            part = down(j - 3, j)
            acc = part if acc is None else acc + part
    j0 = (n_ch - 1) // 3 * 3
    acc = acc + down(j0, n_ch)
    z = x + g2_ref[0] * acc
    o_ref[0] = z * lax.rsqrt(jnp.mean(z * z, axis=-1, keepdims=True) + EPS) * gf_ref[...]


def _ffn(x, sh2, sc2, g2, gain2, gain_f, w_up, conv_w, conv_b, w_down):
    b, n, _ = x.shape
    tm = TM_FFN
    per = tm // HALO
    n_halo = n // HALO
    rows = tm + 2 * HALO
    tok = pl.BlockSpec((1, tm, D_MODEL), lambda bi, i: (bi, i, 0))
    mod = pl.BlockSpec((1, 1, D_MODEL), lambda bi, i: (bi, 0, 0))
    prev = pl.BlockSpec((1, HALO, D_MODEL), lambda bi, i: (bi, jnp.maximum(i * per - 1, 0), 0))
    nxt = pl.BlockSpec((1, HALO, D_MODEL), lambda bi, i: (bi, jnp.minimum((i + 1) * per, n_halo - 1), 0))
    return pl.pallas_call(
        _ffn_kernel,
        grid=(b, n // tm),
        in_specs=[tok, prev, nxt, mod, mod, mod,
                  _resident((1, D_MODEL)), _resident((1, D_MODEL)),
                  _resident(w_up.shape), _resident(conv_w.shape),
                  _resident(conv_b.shape), _resident((D_FF, D_MODEL))],
        out_specs=tok,
        out_shape=jax.ShapeDtypeStruct((b, n, D_MODEL), F32),
        scratch_shapes=[pltpu.VMEM((rows, D_MODEL), BF16),
                        pltpu.VMEM((3, 2, rows, FF_CHUNK), F32),
                        pltpu.VMEM((tm, D_FF), BF16)],
        compiler_params=_cparams(2),
        name="ffn",
    )(x, x, x, sh2, sc2, g2, gain2, gain_f, w_up, conv_w, conv_b, w_down)


def _rope_tables(n):
    nf = RET_QK_DIM // 4
    inv = (ROPE_BASE ** (-np.arange(nf, dtype=np.float32) / nf)).astype(np.float32)
    t = np.arange(n)
    ang_r = (t // GRID_W).astype(np.float32)[:, None] * inv[None, :]
    ang_c = (t % GRID_W).astype(np.float32)[:, None] * inv[None, :]
    cos = np.concatenate([np.cos(ang_r)] * 2 + [np.cos(ang_c)] * 2, axis=-1)
    zero = np.zeros((n, nf), np.float32)
    s1 = np.concatenate([-np.sin(ang_r), zero, -np.sin(ang_c), zero], axis=-1)
    s2 = np.concatenate([zero, np.sin(ang_r), zero, np.sin(ang_c)], axis=-1)
    return tuple(jnp.asarray(a.astype(np.float32)) for a in (cos, s1, s2))


def _na_bias_table(rpb):
    w, kh, kw = GRID_W, NA_WIN_H, NA_WIN_W
    cols = np.arange(w)
    c_start = np.clip(cols - kw // 2, 0, w - kw)
    kcol = np.arange(w)
    valid = (kcol[None, :] >= c_start[:, None]) & (kcol[None, :] < c_start[:, None] + kw)
    dc = kcol[None, :] - cols[:, None] + (kw - 1)
    onehot = ((dc[:, :, None] == np.arange(2 * kw - 1)) & valid[:, :, None]).astype(np.float32)
    base = jnp.einsum('hrd,ckd->hcrk', rpb, onehot, precision=lax.Precision.HIGHEST)
    base = base + np.where(valid, 0.0, MASK_VALUE).astype(np.float32)[:, None, :]
    return jnp.stack([base[:, :, v:v + kh].reshape(rpb.shape[0], w, kh * w) for v in range(kh)], axis=1)


def kernel(x, c, ctx, c_ctx, w_ada, b_ada, norm1_g, w_in, na_rpb, ret_log_decay_fwd, ret_log_decay_bwd,
           w_proj_na, w_proj_ret, w_out, norm2_g, w_up, conv_w, conv_b, w_down, final_norm_g):
    b, n, d = x.shape
    assert w_ada.shape[0] == 1, "single layer"
    row = lambda v: v.reshape(1, -1)

    cc = jnp.concatenate([c, c_ctx[None], jnp.zeros((ADA_ROWS - b - 1, d), F32)], axis=0)
    mod = _ada(cc, w_ada[0], row(b_ada[0]))
    xmod = mod[:b].reshape(b, 1, 6, d)
    sh1, sc1, g1, sh2, sc2, g2 = (xmod[:, :, i] for i in range(6))
    cmod = mod[b:b + 1].reshape(1, 1, 6, d)
    csh1, csc1 = cmod[:, :, 0], cmod[:, :, 1]

    w_in_b = w_in[0].astype(BF16)
    cos, s1, s2 = _rope_tables(n)
    qkv, rqk, rv, rg, gates = _inproj(x, sh1, sc1, row(norm1_g[0]), w_in_b, cos, s1, s2)
    ckv, crk, crv = _inproj_ctx(ctx, csh1, csc1, row(norm1_g[0]), w_in_b)

    o_na = _na(qkv, ckv, _na_bias_table(na_rpb[0]))
    o_ret = _ret(ret_log_decay_fwd[0], ret_log_decay_bwd[0], rqk, rv, rg, crk, crv)

    x1 = _merge(x, o_na, o_ret, gates, g1, w_proj_na[0].astype(BF16), w_proj_ret[0].astype(BF16),
                w_out[0].astype(BF16))
    return _ffn(x1, sh2, sc2, g2, row(norm2_g[0]), row(final_norm_g), w_up[0].astype(BF16),
                conv_w[0], row(conv_b[0]), w_down[0].astype(BF16))
```

```python
import math

import numpy as np
import jax
import jax.numpy as jnp
from jax import lax
from jax.experimental import pallas as pl
from jax.experimental.pallas import tpu as pltpu

F32 = jnp.float32
BF16 = jnp.bfloat16

D_MODEL = 1024
GRID_W = 64
NA_HEADS = 8
NA_HEAD_DIM = 64
NA_WIN_H = 8
NA_WIN_W = 16
RET_HEADS = 4
RET_QK_DIM = 128
RET_V_DIM = 256
RET_CHUNK = 128
D_FF = int(math.ceil(8 * D_MODEL / 3 / 128)) * 128
CONV_W = 3
ROPE_BASE = 10000.0
EPS = 1e-6

NA_W = NA_HEADS * NA_HEAD_DIM
RET_QK_W = RET_HEADS * RET_QK_DIM
RET_V_W = RET_HEADS * RET_V_DIM
OFF_Q, OFF_K, OFF_V = 0, NA_W, 2 * NA_W
OFF_RQ = 3 * NA_W
OFF_RK = OFF_RQ + RET_QK_W
OFF_RV = OFF_RK + RET_QK_W
OFF_RG = OFF_RV + RET_V_W
OFF_GA = OFF_RG + RET_V_W
OFF_GB = OFF_GA + D_MODEL
IN_COLS = OFF_GB + D_MODEL

NA_SCALE = NA_HEAD_DIM ** -0.5
RK_SCALE = RET_QK_DIM ** -0.5
MASK_VALUE = -1e30

VMEM_LIMIT_BYTES = 56 * 1024 * 1024
LANES = 128
HALO = 8

TM_PROJ = 512
TM_FFN = 512
ADA_ROWS = 16
ADA_TN = 1536
FF_CHUNK = 256
PROJ_CHUNK = 512
NA_BLOCK = 2
RET_UNROLL = 32
RET_BLOCK = 16


def _cparams(n_axes):
    return pltpu.CompilerParams(dimension_semantics=("arbitrary",) * n_axes,
                                vmem_limit_bytes=VMEM_LIMIT_BYTES)


def _resident(shape):
    nd = len(shape)
    return pl.BlockSpec(shape, lambda *_: (0,) * nd, pipeline_mode=pl.Buffered(1))


def _rms_modulate(x, gain, shift, scale):
    y = x * lax.rsqrt(jnp.mean(x * x, axis=-1, keepdims=True) + EPS)
    return (y * gain) * (1.0 + scale) + shift


def _ada_kernel(c_ref, w_ref, b_ref, o_ref):
    s = jax.nn.silu(c_ref[...])
    o_ref[...] = jnp.dot(s.astype(BF16), w_ref[...].astype(BF16), preferred_element_type=F32) + b_ref[...]


def _ada(cc, w_ada, b_ada):
    n_out = w_ada.shape[1]
    return pl.pallas_call(
        _ada_kernel,
        grid=(n_out // ADA_TN,),
        in_specs=[pl.BlockSpec((ADA_ROWS, D_MODEL), lambda j: (0, 0)),
                  pl.BlockSpec((D_MODEL, ADA_TN), lambda j: (0, j)),
                  pl.BlockSpec((1, ADA_TN), lambda j: (0, j))],
        out_specs=pl.BlockSpec((ADA_ROWS, ADA_TN), lambda j: (0, j)),
        out_shape=jax.ShapeDtypeStruct((ADA_ROWS, n_out), F32),
        compiler_params=_cparams(1),
        name="ada",
    )(cc, w_ada, b_ada)


def _inproj_kernel(x_ref, sh_ref, sc_ref, g_ref, w_ref, cs_ref, s1_ref, s2_ref,
                   qkv_ref, rqk_ref, rv_ref, rg_ref, gate_ref):
    h = _rms_modulate(x_ref[0], g_ref[...], sh_ref[0], sc_ref[0]).astype(BF16)

    def proj(c0):
        return jnp.dot(h, w_ref[:, c0:c0 + PROJ_CHUNK], preferred_element_type=F32)

    for j in range(3 * NA_W // PROJ_CHUNK):
        r = proj(OFF_Q + j * PROJ_CHUNK)
        if j * PROJ_CHUNK < NA_W:
            r = r * NA_SCALE
        qkv_ref[0, :, j * PROJ_CHUNK:(j + 1) * PROJ_CHUNK] = r.astype(BF16)

    cs, s1, s2 = cs_ref[...], s1_ref[...], s2_ref[...]
    quarter = RET_QK_DIM // 4
    for j in range(2 * RET_QK_W // PROJ_CHUNK):
        r = proj(OFF_RQ + j * PROJ_CHUNK)
        for hh in range(PROJ_CHUNK // RET_QK_DIM):
            t = r[:, hh * RET_QK_DIM:(hh + 1) * RET_QK_DIM]
            t = (t * cs + pltpu.roll(t, RET_QK_DIM - quarter, 1) * s1
                 + pltpu.roll(t, quarter, 1) * s2)
            if j * PROJ_CHUNK >= RET_QK_W:
                t = t * RK_SCALE
            c0 = j * PROJ_CHUNK + hh * RET_QK_DIM
            rqk_ref[0, :, c0:c0 + RET_QK_DIM] = t.astype(BF16)

    for j in range(RET_V_W // PROJ_CHUNK):
        rv_ref[0, :, j * PROJ_CHUNK:(j + 1) * PROJ_CHUNK] = proj(OFF_RV + j * PROJ_CHUNK).astype(BF16)
    for j in range(RET_V_W // PROJ_CHUNK):
        r = proj(OFF_RG + j * PROJ_CHUNK)
        rg_ref[0, :, j * PROJ_CHUNK:(j + 1) * PROJ_CHUNK] = jax.nn.silu(r).astype(BF16)
    for j in range(2 * D_MODEL // PROJ_CHUNK):
        r = proj(OFF_GA + j * PROJ_CHUNK)
        gate_ref[0, :, j * PROJ_CHUNK:(j + 1) * PROJ_CHUNK] = jax.nn.sigmoid(r).astype(BF16)


def _inproj(x, shift, scale, gain, w_in, cs, s1, s2):
    b, t, _ = x.shape
    tm = TM_PROJ
    mod_map = lambda bi, i: (bi, 0, 0)
    tok = lambda w: pl.BlockSpec((1, tm, w), lambda bi, i: (bi, i, 0))
    tab = pl.BlockSpec((tm, RET_QK_DIM), lambda bi, i: (i, 0))
    out_w = (3 * NA_W, 2 * RET_QK_W, RET_V_W, RET_V_W, 2 * D_MODEL)
    return pl.pallas_call(
        _inproj_kernel,
        grid=(b, t // tm),
        in_specs=[tok(D_MODEL),
                  pl.BlockSpec((1, 1, D_MODEL), mod_map),
                  pl.BlockSpec((1, 1, D_MODEL), mod_map),
                  _resident((1, D_MODEL)),
                  _resident((D_MODEL, IN_COLS)),
                  tab, tab, tab],
        out_specs=[tok(w) for w in out_w],
        out_shape=[jax.ShapeDtypeStruct((b, t, w), BF16) for w in out_w],
        compiler_params=_cparams(2),
        name="inproj",
    )(x, shift, scale, gain, w_in, cs, s1, s2)


def _inproj_ctx_kernel(x_ref, sh_ref, sc_ref, g_ref, w_ref, kv_ref, rk_ref, rv_ref):
    h = _rms_modulate(x_ref[0], g_ref[...], sh_ref[0], sc_ref[0]).astype(BF16)

    def proj(c0):
        return jnp.dot(h, w_ref[:, c0:c0 + PROJ_CHUNK], preferred_element_type=F32)

    for j in range(2 * NA_W // PROJ_CHUNK):
        kv_ref[0, :, j * PROJ_CHUNK:(j + 1) * PROJ_CHUNK] = proj(OFF_K + j * PROJ_CHUNK).astype(BF16)
    for j in range(RET_QK_W // PROJ_CHUNK):
        rk_ref[0, :, j * PROJ_CHUNK:(j + 1) * PROJ_CHUNK] = (
            proj(OFF_RK + j * PROJ_CHUNK) * RK_SCALE).astype(BF16)
    for j in range(RET_V_W // PROJ_CHUNK):
        rv_ref[0, :, j * PROJ_CHUNK:(j + 1) * PROJ_CHUNK] = proj(OFF_RV + j * PROJ_CHUNK).astype(BF16)


def _inproj_ctx(ctx, shift, scale, gain, w_in):
    b, t, _ = ctx.shape
    tok = lambda w: pl.BlockSpec((1, t, w), lambda bi: (bi, 0, 0))
    mod = pl.BlockSpec((1, 1, D_MODEL), lambda bi: (0, 0, 0))
    out_w = (2 * NA_W, RET_QK_W, RET_V_W)
    return pl.pallas_call(
        _inproj_ctx_kernel,
        grid=(b,),
        in_specs=[tok(D_MODEL), mod, mod, _resident((1, D_MODEL)), _resident((D_MODEL, IN_COLS))],
        out_specs=[tok(w) for w in out_w],
        out_shape=[jax.ShapeDtypeStruct((b, t, w), BF16) for w in out_w],
        compiler_params=_cparams(1),
        name="inproj_ctx",
    )(ctx, shift, scale, gain, w_in)


def _na_kernel(q_ref, k_ref, v_ref, kc_ref, vc_ref, bias_ref, o_ref, vext, vcext, s_scr, m_scr, p_scr, oc_scr):
    n_rows = q_ref.shape[1] // GRID_W
    n_blk = n_rows // NA_BLOCK
    win = NA_WIN_H * GRID_W
    width = s_scr.shape[2]
    pair = 2 * GRID_W
    first = lax.broadcasted_iota(jnp.int32, (GRID_W, LANES), 1) < NA_HEAD_DIM
    nt = (((1,), (1,)), ((), ()))
    kc = kc_ref[0]
    vext[:, :LANES] = v_ref[0]
    vext[:, LANES:] = jnp.ones((vext.shape[0], LANES), BF16)
    vcext[:, :LANES] = vc_ref[0]
    vcext[:, LANES:] = jnp.ones((vcext.shape[0], LANES), BF16)

    def window(r):
        rs = jnp.clip(r - NA_WIN_H // 2, 0, n_rows - NA_WIN_H)
        return (pl.ds(pl.multiple_of(r * GRID_W, GRID_W), GRID_W),
                pl.ds(pl.multiple_of(rs * GRID_W, GRID_W), win), rs - r + (NA_WIN_H - 1))

    def score_stage(blk, slot):
        base = blk * NA_BLOCK
        stacked = []
        for k in range(NA_BLOCK):
            q = q_ref[0, window(base + k)[0], :]
            zero = jnp.zeros_like(q)
            stacked.append(jnp.concatenate([jnp.where(first, q, zero), jnp.where(first, zero, q)], axis=0))
        s_ctx = lax.dot_general(jnp.concatenate(stacked, axis=0), kc, nt, preferred_element_type=F32)
        s_scr[slot, :, win:] = s_ctx
        for k in range(NA_BLOCK):
            _, wrows, var = window(base + k)
            s_loc = lax.dot_general(stacked[k], k_ref[0, wrows, :], nt, preferred_element_type=F32)
            for hh in range(2):
                dst = slice(k * pair + hh * GRID_W, k * pair + (hh + 1) * GRID_W)
                sl = s_loc[hh * GRID_W:(hh + 1) * GRID_W] + bias_ref[hh, var]
                s_scr[slot, dst, :win] = sl
                mx = jnp.maximum(jnp.max(sl, axis=-1, keepdims=True),
                                 jnp.max(s_ctx[dst], axis=-1, keepdims=True))
                m_scr[slot, dst, :] = jnp.broadcast_to(mx, (GRID_W, LANES))

    def prob_stage(slot):
        for k in range(NA_BLOCK):
            dst = slice(k * pair, (k + 1) * pair)
            mx = m_scr[slot, dst, :]
            for j in range(width // LANES):
                cols = slice(j * LANES, (j + 1) * LANES)
                p_scr[slot, dst, cols] = jnp.exp(s_scr[slot, dst, cols] - mx).astype(BF16)

    def out_stage(blk, slot):
        oc_scr[...] = jnp.dot(p_scr[slot, :, win:], vcext[...], preferred_element_type=F32)
        for k in range(NA_BLOCK):
            rows, wrows, _ = window(blk * NA_BLOCK + k)
            dst = slice(k * pair, (k + 1) * pair)
            o = (jnp.dot(p_scr[slot, dst, :win], vext[wrows, :], preferred_element_type=F32)
                 + oc_scr[dst, :])
            num = jnp.where(first, o[:GRID_W, :LANES], o[GRID_W:, :LANES])
            den = jnp.where(first, o[:GRID_W, LANES:], o[GRID_W:, LANES:])
            o_ref[0, rows, :] = (num / den).astype(BF16)

    assert n_blk % 2 == 0 and n_blk >= 4
    score_stage(0, 0)
    score_stage(1, 1)
    prob_stage(0)

    def step(t, carry):
        blk = 2 * t
        out_stage(blk, 0)
        prob_stage(1)
        score_stage(blk + 2, 0)
        out_stage(blk + 1, 1)
        prob_stage(0)
        score_stage(blk + 3, 1)
        return carry

    lax.fori_loop(0, n_blk // 2 - 1, step, 0, unroll=True)
    out_stage(n_blk - 2, 0)
    prob_stage(1)
    out_stage(n_blk - 1, 1)


def _na(qkv, ckv, bias):
    b, n, _ = qkv.shape
    ctx = ckv.shape[1]
    pairs = NA_W // LANES
    win = NA_WIN_H * GRID_W
    stage_rows = NA_BLOCK * 2 * GRID_W
    blk = lambda t, off: pl.BlockSpec((1, t, LANES), lambda p, bi: (bi, 0, off + p))
    return pl.pallas_call(
        _na_kernel,
        grid=(pairs, b),
        in_specs=[blk(n, 0), blk(n, pairs), blk(n, 2 * pairs),
                  blk(ctx, 0), blk(ctx, pairs),
                  pl.BlockSpec((2, NA_WIN_H, GRID_W, win), lambda p, bi: (p, 0, 0, 0))],
        out_specs=pl.BlockSpec((1, n, LANES), lambda p, bi: (bi, 0, p)),
        out_shape=jax.ShapeDtypeStruct((b, n, NA_W), BF16),
        scratch_shapes=[pltpu.VMEM((n, 2 * LANES), BF16), pltpu.VMEM((ctx, 2 * LANES), BF16),
                        pltpu.VMEM((2, stage_rows, win + ctx), F32), pltpu.VMEM((2, stage_rows, LANES), F32),
                        pltpu.VMEM((2, stage_rows, win + ctx), BF16), pltpu.VMEM((stage_rows, 2 * LANES), F32)],
        compiler_params=_cparams(2),
        name="na",
    )(qkv, qkv, qkv, ckv, ckv, bias)


def _ret_kernel(lgf_ref, lgb_ref, q_ref, k_ref, v_ref, g_ref, ck_ref, cv_ref, o_ref,
                dmat, xi_f, xi_b, zeta_f, zeta_b, gl_f, gl_b, sb_scr, p_scr, o_acc):
    L = RET_CHUNK
    hd = pl.program_id(1)
    lgf = lgf_ref[hd]
    lgb = lgb_ref[hd]
    n_chunks = q_ref.shape[1] // L
    c_chunks = ck_ref.shape[1] // L

    li = lax.broadcasted_iota(jnp.int32, (L, L), 0).astype(F32)
    mi = lax.broadcasted_iota(jnp.int32, (L, L), 1).astype(F32)
    diff = li - mi
    dmat[...] = jnp.where(diff >= 0.0, jnp.exp(lgf * jnp.maximum(diff, 0.0)),
                          jnp.exp(lgb * jnp.maximum(-diff, 0.0)))
    zeta_f[...] = jnp.exp(lgf * (L - 1.0 - li))
    zeta_b[...] = jnp.exp(lgb * li)
    lv = lax.broadcasted_iota(jnp.int32, (L, RET_V_DIM), 0).astype(F32)
    xi_f[...] = jnp.exp(lgf * (lv + 1.0))
    xi_b[...] = jnp.exp(lgb * (L - lv))
    gl_f[...] = jnp.exp(jnp.full((L, RET_V_DIM), L, F32) * lgf)
    gl_b[...] = jnp.exp(jnp.full((L, RET_V_DIM), L, F32) * lgb)

    tn = (((0,), (0,)), ((), ()))
    nt = (((1,), (1,)), ((), ()))

    def advance(state, gl, zeta, k, v):
        kz = (k.astype(F32) * zeta[...]).astype(BF16)
        return gl[...] * state + lax.dot_general(kz, v, tn, preferred_element_type=F32)

    sf = jnp.zeros((RET_QK_DIM, RET_V_DIM), F32)
    sb = jnp.zeros((RET_QK_DIM, RET_V_DIM), F32)
    for c in range(c_chunks):
        sf = advance(sf, gl_f, zeta_f, ck_ref[0, c * L:(c + 1) * L, :], cv_ref[0, c * L:(c + 1) * L, :])
    for c in reversed(range(c_chunks)):
        sb = advance(sb, gl_b, zeta_b, ck_ref[0, c * L:(c + 1) * L, :], cv_ref[0, c * L:(c + 1) * L, :])

    chunk = lambda c: pl.ds(pl.multiple_of(c * L, L), L)

    def bwd_states(i, sb):
        cb = n_chunks - 1 - i
        sb_scr[cb] = sb.astype(BF16)
        return advance(sb, gl_b, zeta_b, k_ref[0, chunk(cb), :], v_ref[0, chunk(cb), :])

    lax.fori_loop(0, n_chunks, bwd_states, sb, unroll=RET_UNROLL)

    def probs(c):
        rows = chunk(c)
        a = lax.dot_general(q_ref[0, rows, :], k_ref[0, rows, :], nt, preferred_element_type=F32)
        p_scr[rows, :] = (a * dmat[...]).astype(BF16)

    def outs(c, sf):
        rows = chunk(c)
        q = q_ref[0, rows, :]
        v = v_ref[0, rows, :]
        o_acc[rows, :] = (jnp.dot(p_scr[rows, :], v, preferred_element_type=F32)
                          + jnp.dot(q, sf.astype(BF16), preferred_element_type=F32) * xi_f[...]
                          + jnp.dot(q, sb_scr[c], preferred_element_type=F32) * xi_b[...])
        return advance(sf, gl_f, zeta_f, k_ref[0, rows, :], v)

    def norm(c):
        rows = chunk(c)
        o = o_acc[rows, :]
        mu = jnp.mean(o, axis=-1, keepdims=True)
        d = o - mu
        var = jnp.mean(d * d, axis=-1, keepdims=True)
        o_ref[0, rows, :] = (d * lax.rsqrt(var + EPS) * g_ref[0, rows, :].astype(F32)).astype(BF16)

    n_blk = n_chunks // RET_BLOCK
    block = lambda t: [t * RET_BLOCK + i for i in range(RET_BLOCK)]

    def outs_block(t, sf):
        for c in block(t):
            sf = outs(c, sf)
        return sf

    for c in block(0):
        probs(c)
    sf = outs_block(0, sf)
    for c in block(1):
        probs(c)

    def step(t, sf):
        for c in block(t - 2):
            norm(c)
        sf = outs_block(t - 1, sf)
        for c in block(t):
            probs(c)
        return sf

    sf = lax.fori_loop(2, n_blk, step, sf, unroll=True)
    for c in block(n_blk - 2):
        norm(c)
    outs_block(n_blk - 1, sf)
    for c in block(n_blk - 1):
        norm(c)


def _ret(lgf, lgb, rqk, rv, rg, crk, crv):
    b, n, _ = rqk.shape
    ctx = crk.shape[1]
    L = RET_CHUNK
    smem = pl.BlockSpec(memory_space=pltpu.SMEM)
    qk = lambda t, off: pl.BlockSpec((1, t, RET_QK_DIM), lambda bi, h: (bi, 0, off + h))
    vv = lambda t: pl.BlockSpec((1, t, RET_V_DIM), lambda bi, h: (bi, 0, h))
    vm = lambda r, c: pltpu.VMEM((r, c), F32)
    st = pltpu.VMEM((n // L, RET_QK_DIM, RET_V_DIM), BF16)
    return pl.pallas_call(
        _ret_kernel,
        grid=(b, RET_HEADS),
        in_specs=[smem, smem, qk(n, 0), qk(n, RET_HEADS), vv(n), vv(n), qk(ctx, 0), vv(ctx)],
        out_specs=vv(n),
        out_shape=jax.ShapeDtypeStruct((b, n, RET_V_W), BF16),
        scratch_shapes=[vm(L, L), vm(L, RET_V_DIM), vm(L, RET_V_DIM), vm(L, L), vm(L, L),
                        vm(L, RET_V_DIM), vm(L, RET_V_DIM), st,
                        pltpu.VMEM((n, L), BF16), vm(n, RET_V_DIM)],
        compiler_params=_cparams(2),
        name="ret",
    )(lgf, lgb, rqk, rqk, rv, rg, crk, crv)


def _merge_kernel(x_ref, ona_ref, oret_ref, gate_ref, g1_ref, wna_ref, wret_ref, wout_ref, o_ref):
    a = jnp.dot(ona_ref[0], wna_ref[...], preferred_element_type=F32)
    r = jnp.dot(oret_ref[0], wret_ref[...], preferred_element_type=F32)
    merged = (gate_ref[0, :, :D_MODEL].astype(F32) * a
              + gate_ref[0, :, D_MODEL:].astype(F32) * r).astype(BF16)
    y = jnp.dot(merged, wout_ref[...], preferred_element_type=F32)
    o_ref[0] = x_ref[0] + g1_ref[0] * y


def _merge(x, o_na, o_ret, gates, g1, w_na, w_ret, w_out):
    b, n, _ = x.shape
    tm = TM_PROJ
    tok = lambda w: pl.BlockSpec((1, tm, w), lambda bi, i: (bi, i, 0))
    return pl.pallas_call(
        _merge_kernel,
        grid=(b, n // tm),
        in_specs=[tok(D_MODEL), tok(NA_W), tok(RET_V_W), tok(2 * D_MODEL),
                  pl.BlockSpec((1, 1, D_MODEL), lambda bi, i: (bi, 0, 0)),
                  _resident((NA_W, D_MODEL)), _resident((RET_V_W, D_MODEL)),
                  _resident((D_MODEL, D_MODEL))],
        out_specs=tok(D_MODEL),
        out_shape=jax.ShapeDtypeStruct((b, n, D_MODEL), F32),
        compiler_params=_cparams(2),
        name="merge",
    )(x, o_na, o_ret, gates, g1, w_na, w_ret, w_out)


def _ffn_kernel(x_ref, xp_ref, xn_ref, sh_ref, sc_ref, g2_ref, gn_ref, gf_ref,
                wup_ref, cw_ref, cb_ref, wdown_ref, o_ref, h_scr, u_scr, y_scr):
    i = pl.program_id(1)
    last = pl.num_programs(1) - 1
    tm = x_ref.shape[1]
    rows = tm + 2 * HALO
    n_ch = D_FF // FF_CHUNK
    x = x_ref[0]
    xa = jnp.concatenate([xp_ref[0], x, xn_ref[0]], axis=0)
    h = _rms_modulate(xa, gn_ref[...], sh_ref[0], sc_ref[0])
    ridx = lax.broadcasted_iota(jnp.int32, (rows, 1), 0)
    valid = jnp.logical_and(jnp.logical_or(ridx >= HALO, i > 0),
                            jnp.logical_or(ridx < tm + HALO, i < last))
    h_scr[...] = jnp.where(valid, h, 0.0).astype(BF16)

    def up(j, slot):
        for g in range(2):
            c0 = g * D_FF + j * FF_CHUNK
            u_scr[slot, g] = jnp.dot(h_scr[...], wup_ref[:, c0:c0 + FF_CHUNK], preferred_element_type=F32)

    def act(j, slot):
        def conv(g):
            c0 = g * D_FF + j * FF_CHUNK
            u = u_scr[slot, g]
            w = cw_ref[:, c0:c0 + FF_CHUNK]
            y = (pltpu.roll(u, 1, 0) * w[0:1] + u * w[1:2] + pltpu.roll(u, rows - 1, 0) * w[2:3]
                 + cb_ref[:, c0:c0 + FF_CHUNK])
            return y[HALO:HALO + tm]

        y_scr[:, j * FF_CHUNK:(j + 1) * FF_CHUNK] = (jax.nn.silu(conv(0)) * conv(1)).astype(BF16)

    def down(j0, j1):
        c0, c1 = j0 * FF_CHUNK, j1 * FF_CHUNK
        return jnp.dot(y_scr[:, c0:c1], wdown_ref[c0:c1, :], preferred_element_type=F32)

    up(0, 0)
    up(1, 1)
    acc = None
    for j in range(n_ch):
        act(j, j % 3)
        if j + 2 < n_ch:
            up(j + 2, (j + 2) % 3)
        if j % 3 == 0 and j > 0:
            part = down(j - 3, j)
            acc = part if acc is None else acc + part
    j0 = (n_ch - 1) // 3 * 3
    acc = acc + down(j0, n_ch)
    z = x + g2_ref[0] * acc
    o_ref[0] = z * lax.rsqrt(jnp.mean(z * z, axis=-1, keepdims=True) + EPS) * gf_ref[...]


def _ffn(x, sh2, sc2, g2, gain2, gain_f, w_up, conv_w, conv_b, w_down):
    b, n, _ = x.shape
    tm = TM_FFN
    per = tm // HALO
    n_halo = n // HALO
    rows = tm + 2 * HALO
    tok = pl.BlockSpec((1, tm, D_MODEL), lambda bi, i: (bi, i, 0))
    mod = pl.BlockSpec((1, 1, D_MODEL), lambda bi, i: (bi, 0, 0))
    prev = pl.BlockSpec((1, HALO, D_MODEL), lambda bi, i: (bi, jnp.maximum(i * per - 1, 0), 0))
    nxt = pl.BlockSpec((1, HALO, D_MODEL), lambda bi, i: (bi, jnp.minimum((i + 1) * per, n_halo - 1), 0))
    return pl.pallas_call(
        _ffn_kernel,
        grid=(b, n // tm),
        in_specs=[tok, prev, nxt, mod, mod, mod,
                  _resident((1, D_MODEL)), _resident((1, D_MODEL)),
                  _resident(w_up.shape), _resident(conv_w.shape),
                  _resident(conv_b.shape), _resident((D_FF, D_MODEL))],
        out_specs=tok,
        out_shape=jax.ShapeDtypeStruct((b, n, D_MODEL), F32),
        scratch_shapes=[pltpu.VMEM((rows, D_MODEL), BF16),
                        pltpu.VMEM((3, 2, rows, FF_CHUNK), F32),
                        pltpu.VMEM((tm, D_FF), BF16)],
        compiler_params=_cparams(2),
        name="ffn",
    )(x, x, x, sh2, sc2, g2, gain2, gain_f, w_up, conv_w, conv_b, w_down)


def _rope_tables(n):
    nf = RET_QK_DIM // 4
    inv = (ROPE_BASE ** (-np.arange(nf, dtype=np.float32) / nf)).astype(np.float32)
    t = np.arange(n)
    ang_r = (t // GRID_W).astype(np.float32)[:, None] * inv[None, :]
    ang_c = (t % GRID_W).astype(np.float32)[:, None] * inv[None, :]
    cos = np.concatenate([np.cos(ang_r)] * 2 + [np.cos(ang_c)] * 2, axis=-1)
    zero = np.zeros((n, nf), np.float32)
    s1 = np.concatenate([-np.sin(ang_r), zero, -np.sin(ang_c), zero], axis=-1)
    s2 = np.concatenate([zero, np.sin(ang_r), zero, np.sin(ang_c)], axis=-1)
    return tuple(jnp.asarray(a.astype(np.float32)) for a in (cos, s1, s2))


def _na_bias_table(rpb):
    w, kh, kw = GRID_W, NA_WIN_H, NA_WIN_W
    cols = np.arange(w)
    c_start = np.clip(cols - kw // 2, 0, w - kw)
    kcol = np.arange(w)
    valid = (kcol[None, :] >= c_start[:, None]) & (kcol[None, :] < c_start[:, None] + kw)
    dc = kcol[None, :] - cols[:, None] + (kw - 1)
    onehot = ((dc[:, :, None] == np.arange(2 * kw - 1)) & valid[:, :, None]).astype(np.float32)
    base = jnp.einsum('hrd,ckd->hcrk', rpb, onehot, precision=lax.Precision.HIGHEST)
    base = base + np.where(valid, 0.0, MASK_VALUE).astype(np.float32)[:, None, :]
    return jnp.stack([base[:, :, v:v + kh].reshape(rpb.shape[0], w, kh * w) for v in range(kh)], axis=1)


def kernel(x, c, ctx, c_ctx, w_ada, b_ada, norm1_g, w_in, na_rpb, ret_log_decay_fwd, ret_log_decay_bwd,
           w_proj_na, w_proj_ret, w_out, norm2_g, w_up, conv_w, conv_b, w_down, final_norm_g):
    b, n, d = x.shape
    assert w_ada.shape[0] == 1, "single layer"
    row = lambda v: v.reshape(1, -1)

    cc = jnp.concatenate([c, c_ctx[None], jnp.zeros((ADA_ROWS - b - 1, d), F32)], axis=0)
    mod = _ada(cc, w_ada[0], row(b_ada[0]))
    xmod = mod[:b].reshape(b, 1, 6, d)
    sh1, sc1, g1, sh2, sc2, g2 = (xmod[:, :, i] for i in range(6))
    cmod = mod[b:b + 1].reshape(1, 1, 6, d)
    csh1, csc1 = cmod[:, :, 0], cmod[:, :, 1]

    w_in_b = w_in[0].astype(BF16)
    cos, s1, s2 = _rope_tables(n)
    qkv, rqk, rv, rg, gates = _inproj(x, sh1, sc1, row(norm1_g[0]), w_in_b, cos, s1, s2)
    ckv, crk, crv = _inproj_ctx(ctx, csh1, csc1, row(norm1_g[0]), w_in_b)

    o_na = _na(qkv, ckv, _na_bias_table(na_rpb[0]))
    o_ret = _ret(ret_log_decay_fwd[0], ret_log_decay_bwd[0], rqk, rv, rg, crk, crv)

    x1 = _merge(x, o_na, o_ret, gates, g1, w_proj_na[0].astype(BF16), w_proj_ret[0].astype(BF16),
                w_out[0].astype(BF16))
    return _ffn(x1, sh2, sc2, g2, row(norm2_g[0]), row(final_norm_g), w_up[0].astype(BF16),
                conv_w[0], row(conv_b[0]), w_down[0].astype(BF16))
```

```python
import math

import numpy as np
import jax
import jax.numpy as jnp
from jax import lax
from jax.experimental import pallas as pl
from jax.experimental.pallas import tpu as pltpu

F32 = jnp.float32
BF16 = jnp.bfloat16

D_MODEL = 1024
GRID_W = 64
NA_HEADS = 8
NA_HEAD_DIM = 64
NA_WIN_H = 8
NA_WIN_W = 16
RET_HEADS = 4
RET_QK_DIM = 128
RET_V_DIM = 256
RET_CHUNK = 128
D_FF = int(math.ceil(8 * D_MODEL / 3 / 128)) * 128
CONV_W = 3
ROPE_BASE = 10000.0
EPS = 1e-6

NA_W = NA_HEADS * NA_HEAD_DIM
RET_QK_W = RET_HEADS * RET_QK_DIM
RET_V_W = RET_HEADS * RET_V_DIM
OFF_Q, OFF_K, OFF_V = 0, NA_W, 2 * NA_W
OFF_RQ = 3 * NA_W
OFF_RK = OFF_RQ + RET_QK_W
OFF_RV = OFF_RK + RET_QK_W
OFF_RG = OFF_RV + RET_V_W
OFF_GA = OFF_RG + RET_V_W
OFF_GB = OFF_GA + D_MODEL
IN_COLS = OFF_GB + D_MODEL

NA_SCALE = NA_HEAD_DIM ** -0.5
RK_SCALE = RET_QK_DIM ** -0.5
MASK_VALUE = -1e30

VMEM_LIMIT_BYTES = 56 * 1024 * 1024
LANES = 128
HALO = 8

TM_PROJ = 512
TM_MERGE = 1024
TM_FFN = 1024
ADA_ROWS = 16
ADA_TN = 1536
FF_CHUNK = 256
PROJ_CHUNK = 512
NA_BLOCK = 2
RET_UNROLL = 32
RET_BLOCK = 16


def _cparams(n_axes):
    return pltpu.CompilerParams(dimension_semantics=("arbitrary",) * n_axes,
                                vmem_limit_bytes=VMEM_LIMIT_BYTES)


def _resident(shape):
    nd = len(shape)
    return pl.BlockSpec(shape, lambda *_: (0,) * nd, pipeline_mode=pl.Buffered(1))


def _rms_modulate(x, gain, shift, scale):
    y = x * lax.rsqrt(jnp.mean(x * x, axis=-1, keepdims=True) + EPS)
    return (y * gain) * (1.0 + scale) + shift


def _ada_kernel(c_ref, w_ref, b_ref, o_ref):
    s = jax.nn.silu(c_ref[...])
    o_ref[...] = jnp.dot(s.astype(BF16), w_ref[...].astype(BF16), preferred_element_type=F32) + b_ref[...]


def _ada(cc, w_ada, b_ada):
    n_out = w_ada.shape[1]
    return pl.pallas_call(
        _ada_kernel,
        grid=(n_out // ADA_TN,),
        in_specs=[pl.BlockSpec((ADA_ROWS, D_MODEL), lambda j: (0, 0)),
                  pl.BlockSpec((D_MODEL, ADA_TN), lambda j: (0, j)),
                  pl.BlockSpec((1, ADA_TN), lambda j: (0, j))],
        out_specs=pl.BlockSpec((ADA_ROWS, ADA_TN), lambda j: (0, j)),
        out_shape=jax.ShapeDtypeStruct((ADA_ROWS, n_out), F32),
        compiler_params=_cparams(1),
        name="ada",
    )(cc, w_ada, b_ada)


def _rowproj_kernel(s_ref, w_ref, o_ref):
    o_ref[...] = jnp.dot(s_ref[...].astype(BF16), w_ref[...], preferred_element_type=F32)


def _rowproj(rows, w):
    n_out = w.shape[1]
    tn = n_out // 2
    return pl.pallas_call(
        _rowproj_kernel,
        grid=(n_out // tn,),
        in_specs=[pl.BlockSpec((ADA_ROWS, D_MODEL), lambda j: (0, 0)),
                  pl.BlockSpec((D_MODEL, tn), lambda j: (0, j))],
        out_specs=pl.BlockSpec((ADA_ROWS, tn), lambda j: (0, j)),
        out_shape=jax.ShapeDtypeStruct((ADA_ROWS, n_out), F32),
        compiler_params=_cparams(1),
        name="rowproj",
    )(rows, w)


def _inproj_kernel(x_ref, shw_ref, sc_ref, g_ref, w_ref, cs_ref, s1_ref, s2_ref,
                   qkv_ref, rqk_ref, rv_ref, rg_ref, gate_ref):
    x = x_ref[0]
    xa = (x * (g_ref[...] * (1.0 + sc_ref[0]))).astype(BF16)
    inv_rms = lax.rsqrt(jnp.mean(x * x, axis=-1, keepdims=True) + EPS)

    def proj(c0):
        return (jnp.dot(xa, w_ref[:, c0:c0 + PROJ_CHUNK], preferred_element_type=F32) * inv_rms
                + shw_ref[0, :, c0:c0 + PROJ_CHUNK])

    for j in range(2 * D_MODEL // PROJ_CHUNK):
        r = proj(OFF_GA + j * PROJ_CHUNK)
        gate_ref[0, :, j * PROJ_CHUNK:(j + 1) * PROJ_CHUNK] = jax.nn.sigmoid(r).astype(BF16)
    for j in range(RET_V_W // PROJ_CHUNK):
        r = proj(OFF_RG + j * PROJ_CHUNK)
        rg_ref[0, :, j * PROJ_CHUNK:(j + 1) * PROJ_CHUNK] = jax.nn.silu(r).astype(BF16)

    cs, s1, s2 = cs_ref[...], s1_ref[...], s2_ref[...]
    quarter = RET_QK_DIM // 4
    for j in range(2 * RET_QK_W // PROJ_CHUNK):
        r = proj(OFF_RQ + j * PROJ_CHUNK)
        for hh in range(PROJ_CHUNK // RET_QK_DIM):
            t = r[:, hh * RET_QK_DIM:(hh + 1) * RET_QK_DIM]
            t = (t * cs + pltpu.roll(t, RET_QK_DIM - quarter, 1) * s1
                 + pltpu.roll(t, quarter, 1) * s2)
            if j * PROJ_CHUNK >= RET_QK_W:
                t = t * RK_SCALE
            c0 = j * PROJ_CHUNK + hh * RET_QK_DIM
            rqk_ref[0, :, c0:c0 + RET_QK_DIM] = t.astype(BF16)

    for j in range(3 * NA_W // PROJ_CHUNK):
        r = proj(OFF_Q + j * PROJ_CHUNK)
        if j * PROJ_CHUNK < NA_W:
            r = r * NA_SCALE
        qkv_ref[0, :, j * PROJ_CHUNK:(j + 1) * PROJ_CHUNK] = r.astype(BF16)
    for j in range(RET_V_W // PROJ_CHUNK):
        rv_ref[0, :, j * PROJ_CHUNK:(j + 1) * PROJ_CHUNK] = proj(OFF_RV + j * PROJ_CHUNK).astype(BF16)


def _inproj(x, shift_w, scale, gain, w_in, cs, s1, s2):
    b, t, _ = x.shape
    tm = TM_PROJ
    mod_map = lambda bi, i: (bi, 0, 0)
    tok = lambda w: pl.BlockSpec((1, tm, w), lambda bi, i: (bi, i, 0))
    tab = pl.BlockSpec((tm, RET_QK_DIM), lambda bi, i: (i, 0))
    out_w = (3 * NA_W, 2 * RET_QK_W, RET_V_W, RET_V_W, 2 * D_MODEL)
    return pl.pallas_call(
        _inproj_kernel,
        grid=(b, t // tm),
        in_specs=[tok(D_MODEL),
                  pl.BlockSpec((1, 1, IN_COLS), mod_map),
                  pl.BlockSpec((1, 1, D_MODEL), mod_map),
                  _resident((1, D_MODEL)),
                  _resident((D_MODEL, IN_COLS)),
                  tab, tab, tab],
        out_specs=[tok(w) for w in out_w],
        out_shape=[jax.ShapeDtypeStruct((b, t, w), BF16) for w in out_w],
        compiler_params=_cparams(2),
        name="inproj",
    )(x, shift_w, scale, gain, w_in, cs, s1, s2)


def _inproj_ctx_kernel(x_ref, sh_ref, sc_ref, g_ref, w_ref, kv_ref, rk_ref, rv_ref):
    h = _rms_modulate(x_ref[0], g_ref[...], sh_ref[0], sc_ref[0]).astype(BF16)

    def proj(c0):
        return jnp.dot(h, w_ref[:, c0:c0 + PROJ_CHUNK], preferred_element_type=F32)

    for j in range(2 * NA_W // PROJ_CHUNK):
        kv_ref[0, :, j * PROJ_CHUNK:(j + 1) * PROJ_CHUNK] = proj(OFF_K + j * PROJ_CHUNK).astype(BF16)
    for j in range(RET_QK_W // PROJ_CHUNK):
        rk_ref[0, :, j * PROJ_CHUNK:(j + 1) * PROJ_CHUNK] = (
            proj(OFF_RK + j * PROJ_CHUNK) * RK_SCALE).astype(BF16)
    for j in range(RET_V_W // PROJ_CHUNK):
        rv_ref[0, :, j * PROJ_CHUNK:(j + 1) * PROJ_CHUNK] = proj(OFF_RV + j * PROJ_CHUNK).astype(BF16)


def _inproj_ctx(ctx, shift, scale, gain, w_in):
    b, t, _ = ctx.shape
    tok = lambda w: pl.BlockSpec((1, t, w), lambda bi: (bi, 0, 0))
    mod = pl.BlockSpec((1, 1, D_MODEL), lambda bi: (0, 0, 0))
    out_w = (2 * NA_W, RET_QK_W, RET_V_W)
    return pl.pallas_call(
        _inproj_ctx_kernel,
        grid=(b,),
        in_specs=[tok(D_MODEL), mod, mod, _resident((1, D_MODEL)), _resident((D_MODEL, IN_COLS))],
        out_specs=[tok(w) for w in out_w],
        out_shape=[jax.ShapeDtypeStruct((b, t, w), BF16) for w in out_w],
        compiler_params=_cparams(1),
        name="inproj_ctx",
    )(ctx, shift, scale, gain, w_in)


def _na_kernel(q_ref, k_ref, v_ref, kc_ref, vc_ref, bias_ref, o_ref, vext, vcext, s_scr, m_scr, p_scr, oc_scr):
    n_rows = q_ref.shape[1] // GRID_W
    n_blk = n_rows // NA_BLOCK
    win = NA_WIN_H * GRID_W
    width = s_scr.shape[2]
    pair = 2 * GRID_W
    first = lax.broadcasted_iota(jnp.int32, (GRID_W, LANES), 1) < NA_HEAD_DIM
    nt = (((1,), (1,)), ((), ()))
    kc = kc_ref[0]
    vext[:, :LANES] = v_ref[0]
    vext[:, LANES:] = jnp.ones((vext.shape[0], LANES), BF16)
    vcext[:, :LANES] = vc_ref[0]
    vcext[:, LANES:] = jnp.ones((vcext.shape[0], LANES), BF16)

    def window(r):
        rs = jnp.clip(r - NA_WIN_H // 2, 0, n_rows - NA_WIN_H)
        return (pl.ds(pl.multiple_of(r * GRID_W, GRID_W), GRID_W),
                pl.ds(pl.multiple_of(rs * GRID_W, GRID_W), win), rs - r + (NA_WIN_H - 1))

    def score_stage(blk, slot):
        base = blk * NA_BLOCK
        stacked = []
        for k in range(NA_BLOCK):
            q = q_ref[0, window(base + k)[0], :]
            zero = jnp.zeros_like(q)
            stacked.append(jnp.concatenate([jnp.where(first, q, zero), jnp.where(first, zero, q)], axis=0))
        s_ctx = lax.dot_general(jnp.concatenate(stacked, axis=0), kc, nt, preferred_element_type=F32)
        s_scr[slot, :, win:] = s_ctx
        for k in range(NA_BLOCK):
            _, wrows, var = window(base + k)
            s_loc = lax.dot_general(stacked[k], k_ref[0, wrows, :], nt, preferred_element_type=F32)
            for hh in range(2):
                dst = slice(k * pair + hh * GRID_W, k * pair + (hh + 1) * GRID_W)
                sl = s_loc[hh * GRID_W:(hh + 1) * GRID_W] + bias_ref[hh, var]
                s_scr[slot, dst, :win] = sl
                mx = jnp.maximum(jnp.max(sl, axis=-1, keepdims=True),
                                 jnp.max(s_ctx[dst], axis=-1, keepdims=True))
                m_scr[slot, dst, :] = jnp.broadcast_to(mx, (GRID_W, LANES))

    def prob_stage(slot):
        for k in range(NA_BLOCK):
            dst = slice(k * pair, (k + 1) * pair)
            mx = m_scr[slot, dst, :]
            for j in range(width // LANES):
                cols = slice(j * LANES, (j + 1) * LANES)
                p_scr[slot, dst, cols] = jnp.exp(s_scr[slot, dst, cols] - mx).astype(BF16)

    def out_stage(blk, slot):
        oc_scr[...] = jnp.dot(p_scr[slot, :, win:], vcext[...], preferred_element_type=F32)
        for k in range(NA_BLOCK):
            rows, wrows, _ = window(blk * NA_BLOCK + k)
            dst = slice(k * pair, (k + 1) * pair)
            o = (jnp.dot(p_scr[slot, dst, :win], vext[wrows, :], preferred_element_type=F32)
                 + oc_scr[dst, :])
            num = jnp.where(first, o[:GRID_W, :LANES], o[GRID_W:, :LANES])
            den = jnp.where(first, o[:GRID_W, LANES:], o[GRID_W:, LANES:])
            o_ref[0, rows, :] = (num / den).astype(BF16)

    assert n_blk % 2 == 0 and n_blk >= 4
    score_stage(0, 0)
    score_stage(1, 1)
    prob_stage(0)

    def step(t, carry):
        blk = 2 * t
        out_stage(blk, 0)
        prob_stage(1)
        score_stage(blk + 2, 0)
        out_stage(blk + 1, 1)
        prob_stage(0)
        score_stage(blk + 3, 1)
        return carry

    lax.fori_loop(0, n_blk // 2 - 1, step, 0, unroll=True)
    out_stage(n_blk - 2, 0)
    prob_stage(1)
    out_stage(n_blk - 1, 1)


def _na(qkv, ckv, bias):
    b, n, _ = qkv.shape
    ctx = ckv.shape[1]
    pairs = NA_W // LANES
    win = NA_WIN_H * GRID_W
    stage_rows = NA_BLOCK * 2 * GRID_W
    blk = lambda t, off: pl.BlockSpec((1, t, LANES), lambda p, bi: (bi, 0, off + p))
    return pl.pallas_call(
        _na_kernel,
        grid=(pairs, b),
        in_specs=[blk(n, 0), blk(n, pairs), blk(n, 2 * pairs),
                  blk(ctx, 0), blk(ctx, pairs),
                  pl.BlockSpec((2, NA_WIN_H, GRID_W, win), lambda p, bi: (p, 0, 0, 0))],
        out_specs=pl.BlockSpec((1, n, LANES), lambda p, bi: (bi, 0, p)),
        out_shape=jax.ShapeDtypeStruct((b, n, NA_W), BF16),
        scratch_shapes=[pltpu.VMEM((n, 2 * LANES), BF16), pltpu.VMEM((ctx, 2 * LANES), BF16),
                        pltpu.VMEM((2, stage_rows, win + ctx), F32), pltpu.VMEM((2, stage_rows, LANES), F32),
                        pltpu.VMEM((2, stage_rows, win + ctx), BF16), pltpu.VMEM((stage_rows, 2 * LANES), F32)],
        compiler_params=_cparams(2),
        name="na",
    )(qkv, qkv, qkv, ckv, ckv, bias)


def _ret_kernel(lgf_ref, lgb_ref, q_ref, k_ref, v_ref, g_ref, ck_ref, cv_ref, o_ref,
                dmat, xi_f, xi_b, zeta_f, zeta_b, gl_f, gl_b, sb_scr, p_scr, o_acc):
    L = RET_CHUNK
    hd = pl.program_id(1)
    lgf = lgf_ref[hd]
    lgb = lgb_ref[hd]
    n_chunks = q_ref.shape[1] // L
    c_chunks = ck_ref.shape[1] // L

    li = lax.broadcasted_iota(jnp.int32, (L, L), 0).astype(F32)
    mi = lax.broadcasted_iota(jnp.int32, (L, L), 1).astype(F32)
    diff = li - mi
    dmat[...] = jnp.where(diff >= 0.0, jnp.exp(lgf * jnp.maximum(diff, 0.0)),
                          jnp.exp(lgb * jnp.maximum(-diff, 0.0)))
    zeta_f[...] = jnp.exp(lgf * (L - 1.0 - li))
    zeta_b[...] = jnp.exp(lgb * li)
    lv = lax.broadcasted_iota(jnp.int32, (L, RET_V_DIM), 0).astype(F32)
    xi_f[...] = jnp.exp(lgf * (lv + 1.0))
    xi_b[...] = jnp.exp(lgb * (L - lv))
    gl_f[...] = jnp.exp(jnp.full((L, RET_V_DIM), L, F32) * lgf)
    gl_b[...] = jnp.exp(jnp.full((L, RET_V_DIM), L, F32) * lgb)

    tn = (((0,), (0,)), ((), ()))
    nt = (((1,), (1,)), ((), ()))

    def advance(state, gl, zeta, k, v):
        kz = (k.astype(F32) * zeta[...]).astype(BF16)
        return gl[...] * state + lax.dot_general(kz, v, tn, preferred_element_type=F32)

    sf = jnp.zeros((RET_QK_DIM, RET_V_DIM), F32)
    sb = jnp.zeros((RET_QK_DIM, RET_V_DIM), F32)
    for c in range(c_chunks):
        sf = advance(sf, gl_f, zeta_f, ck_ref[0, c * L:(c + 1) * L, :], cv_ref[0, c * L:(c + 1) * L, :])
    for c in reversed(range(c_chunks)):
        sb = advance(sb, gl_b, zeta_b, ck_ref[0, c * L:(c + 1) * L, :], cv_ref[0, c * L:(c + 1) * L, :])

    chunk = lambda c: pl.ds(pl.multiple_of(c * L, L), L)

    def bwd_states(i, sb):
        cb = n_chunks - 1 - i
        sb_scr[cb] = sb.astype(BF16)
        return advance(sb, gl_b, zeta_b, k_ref[0, chunk(cb), :], v_ref[0, chunk(cb), :])

    lax.fori_loop(0, n_chunks, bwd_states, sb, unroll=RET_UNROLL)

    def probs(c):
        rows = chunk(c)
        a = lax.dot_general(q_ref[0, rows, :], k_ref[0, rows, :], nt, preferred_element_type=F32)
        p_scr[rows, :] = (a * dmat[...]).astype(BF16)

    def outs(c, sf):
        rows = chunk(c)
        q = q_ref[0, rows, :]
        v = v_ref[0, rows, :]
        o_acc[rows, :] = (jnp.dot(p_scr[rows, :], v, preferred_element_type=F32)
                          + jnp.dot(q, sf.astype(BF16), preferred_element_type=F32) * xi_f[...]
                          + jnp.dot(q, sb_scr[c], preferred_element_type=F32) * xi_b[...])
        return advance(sf, gl_f, zeta_f, k_ref[0, rows, :], v)

    def norm(c):
        rows = chunk(c)
        o = o_acc[rows, :]
        mu = jnp.mean(o, axis=-1, keepdims=True)
        d = o - mu
        var = jnp.mean(d * d, axis=-1, keepdims=True)
        o_ref[0, rows, :] = (d * lax.rsqrt(var + EPS) * g_ref[0, rows, :].astype(F32)).astype(BF16)

    n_blk = n_chunks // RET_BLOCK
    block = lambda t: [t * RET_BLOCK + i for i in range(RET_BLOCK)]

    def outs_block(t, sf):
        for c in block(t):
            sf = outs(c, sf)
        return sf

    for c in block(0):
        probs(c)
    sf = outs_block(0, sf)
    for c in block(1):
        probs(c)

    def step(t, sf):
        for c in block(t - 2):
            norm(c)
        sf = outs_block(t - 1, sf)
        for c in block(t):
            probs(c)
        return sf

    sf = lax.fori_loop(2, n_blk, step, sf, unroll=True)
    for c in block(n_blk - 2):
        norm(c)
    outs_block(n_blk - 1, sf)
    for c in block(n_blk - 1):
        norm(c)


def _ret(lgf, lgb, rqk, rv, rg, crk, crv):
    b, n, _ = rqk.shape
    ctx = crk.shape[1]
    L = RET_CHUNK
    smem = pl.BlockSpec(memory_space=pltpu.SMEM)
    qk = lambda t, off: pl.BlockSpec((1, t, RET_QK_DIM), lambda bi, h: (bi, 0, off + h))
    vv = lambda t: pl.BlockSpec((1, t, RET_V_DIM), lambda bi, h: (bi, 0, h))
    vm = lambda r, c: pltpu.VMEM((r, c), F32)
    st = pltpu.VMEM((n // L, RET_QK_DIM, RET_V_DIM), BF16)
    return pl.pallas_call(
        _ret_kernel,
        grid=(b, RET_HEADS),
        in_specs=[smem, smem, qk(n, 0), qk(n, RET_HEADS), vv(n), vv(n), qk(ctx, 0), vv(ctx)],
        out_specs=vv(n),
        out_shape=jax.ShapeDtypeStruct((b, n, RET_V_W), BF16),
        scratch_shapes=[vm(L, L), vm(L, RET_V_DIM), vm(L, RET_V_DIM), vm(L, L), vm(L, L),
                        vm(L, RET_V_DIM), vm(L, RET_V_DIM), st,
                        pltpu.VMEM((n, L), BF16), vm(n, RET_V_DIM)],
        compiler_params=_cparams(2),
        name="ret",
    )(lgf, lgb, rqk, rqk, rv, rg, crk, crv)


def _merge_kernel(x_ref, ona_ref, oret_ref, gate_ref, g1_ref, wna_ref, wret_ref, wout_ref, o_ref):
    a = jnp.dot(ona_ref[0], wna_ref[...], preferred_element_type=F32)
    r = jnp.dot(oret_ref[0], wret_ref[...], preferred_element_type=F32)
    merged = (gate_ref[0, :, :D_MODEL].astype(F32) * a
              + gate_ref[0, :, D_MODEL:].astype(F32) * r).astype(BF16)
    y = jnp.dot(merged, wout_ref[...], preferred_element_type=F32)
    o_ref[0] = x_ref[0] + g1_ref[0] * y


def _merge(x, o_na, o_ret, gates, g1, w_na, w_ret, w_out):
    b, n, _ = x.shape
    tm = TM_MERGE
    tok = lambda w: pl.BlockSpec((1, tm, w), lambda bi, i: (bi, i, 0))
    return pl.pallas_call(
        _merge_kernel,
        grid=(b, n // tm),
        in_specs=[tok(D_MODEL), tok(NA_W), tok(RET_V_W), tok(2 * D_MODEL),
                  pl.BlockSpec((1, 1, D_MODEL), lambda bi, i: (bi, 0, 0)),
                  _resident((NA_W, D_MODEL)), _resident((RET_V_W, D_MODEL)),
                  _resident((D_MODEL, D_MODEL))],
        out_specs=tok(D_MODEL),
        out_shape=jax.ShapeDtypeStruct((b, n, D_MODEL), F32),
        compiler_params=_cparams(2),
        name="merge",
    )(x, o_na, o_ret, gates, g1, w_na, w_ret, w_out)


def _ffn_kernel(x_ref, xp_ref, xn_ref, sh_ref, sc_ref, g2_ref, gn_ref, gf_ref,
                wup_ref, cw_ref, cb_ref, wdown_ref, o_ref, h_scr, u_scr, y_scr):
    i = pl.program_id(1)
    last = pl.num_programs(1) - 1
    tm = x_ref.shape[1]
    rows = tm + 2 * HALO
    n_ch = D_FF // FF_CHUNK
    x = x_ref[0]
    xa = jnp.concatenate([xp_ref[0], x, xn_ref[0]], axis=0)
    h = _rms_modulate(xa, gn_ref[...], sh_ref[0], sc_ref[0])
    ridx = lax.broadcasted_iota(jnp.int32, (rows, 1), 0)
    valid = jnp.logical_and(jnp.logical_or(ridx >= HALO, i > 0),
                            jnp.logical_or(ridx < tm + HALO, i < last))
    h_scr[...] = jnp.where(valid, h, 0.0).astype(BF16)

    def up(j, slot):
        for g in range(2):
            c0 = g * D_FF + j * FF_CHUNK
            u_scr[slot, g] = jnp.dot(h_scr[...], wup_ref[:, c0:c0 + FF_CHUNK], preferred_element_type=F32)

    def act(j, slot):
        def conv(g):
            c0 = g * D_FF + j * FF_CHUNK
            u = u_scr[slot, g]
            w = cw_ref[:, c0:c0 + FF_CHUNK]
            y = (pltpu.roll(u, 1, 0) * w[0:1] + u * w[1:2] + pltpu.roll(u, rows - 1, 0) * w[2:3]
                 + cb_ref[:, c0:c0 + FF_CHUNK])
            return y[HALO:HALO + tm]

        y_scr[:, j * FF_CHUNK:(j + 1) * FF_CHUNK] = (jax.nn.silu(conv(0)) * conv(1)).astype(BF16)

    def down(j0, j1):
        c0, c1 = j0 * FF_CHUNK, j1 * FF_CHUNK
        return jnp.dot(y_scr[:, c0:c1], wdown_ref[c0:c1, :], preferred_element_type=F32)

    up(0, 0)
    up(1, 1)
    acc = None
    for j in range(n_ch):
        act(j, j % 3)
        if j + 2 < n_ch:
            up(j + 2, (j + 2) % 3)
        if j % 3 == 0 and j > 0:
            part = down(j - 3, j)
            acc = part if acc is None else acc + part
    j0 = (n_ch - 1) // 3 * 3
    acc = acc + down(j0, n_ch)
    z = x + g2_ref[0] * acc
    o_ref[0] = z * lax.rsqrt(jnp.mean(z * z, axis=-1, keepdims=True) + EPS) * gf_ref[...]


def _ffn(x, sh2, sc2, g2, gain2, gain_f, w_up, conv_w, conv_b, w_down):
    b, n, _ = x.shape
    tm = TM_FFN
    per = tm // HALO
    n_halo = n // HALO
    rows = tm + 2 * HALO
    tok = pl.BlockSpec((1, tm, D_MODEL), lambda bi, i: (bi, i, 0))
    mod = pl.BlockSpec((1, 1, D_MODEL), lambda bi, i: (bi, 0, 0))
    prev = pl.BlockSpec((1, HALO, D_MODEL), lambda bi, i: (bi, jnp.maximum(i * per - 1, 0), 0))
    nxt = pl.BlockSpec((1, HALO, D_MODEL), lambda bi, i: (bi, jnp.minimum((i + 1) * per, n_halo - 1), 0))
    return pl.pallas_call(
        _ffn_kernel,
        grid=(b, n // tm),
        in_specs=[tok, prev, nxt, mod, mod, mod,
                  _resident((1, D_MODEL)), _resident((1, D_MODEL)),
                  _resident(w_up.shape), _resident(conv_w.shape),
                  _resident(conv_b.shape), _resident((D_FF, D_MODEL))],
        out_specs=tok,
        out_shape=jax.ShapeDtypeStruct((b, n, D_MODEL), F32),
        scratch_shapes=[pltpu.VMEM((rows, D_MODEL), BF16),
                        pltpu.VMEM((3, 2, rows, FF_CHUNK), F32),
                        pltpu.VMEM((tm, D_FF), BF16)],
        compiler_params=_cparams(2),
        name="ffn",
    )(x, x, x, sh2, sc2, g2, gain2, gain_f, w_up, conv_w, conv_b, w_down)


def _rope_tables(n):
    nf = RET_QK_DIM // 4
    inv = (ROPE_BASE ** (-np.arange(nf, dtype=np.float32) / nf)).astype(np.float32)
    t = np.arange(n)
    ang_r = (t // GRID_W).astype(np.float32)[:, None] * inv[None, :]
    ang_c = (t % GRID_W).astype(np.float32)[:, None] * inv[None, :]
    cos = np.concatenate([np.cos(ang_r)] * 2 + [np.cos(ang_c)] * 2, axis=-1)
    zero = np.zeros((n, nf), np.float32)
    s1 = np.concatenate([-np.sin(ang_r), zero, -np.sin(ang_c), zero], axis=-1)
    s2 = np.concatenate([zero, np.sin(ang_r), zero, np.sin(ang_c)], axis=-1)
    return tuple(jnp.asarray(a.astype(np.float32)) for a in (cos, s1, s2))


def _na_bias_table(rpb):
    w, kh, kw = GRID_W, NA_WIN_H, NA_WIN_W
    cols = np.arange(w)
    c_start = np.clip(cols - kw // 2, 0, w - kw)
    kcol = np.arange(w)
    valid = (kcol[None, :] >= c_start[:, None]) & (kcol[None, :] < c_start[:, None] + kw)
    dc = kcol[None, :] - cols[:, None] + (kw - 1)
    onehot = ((dc[:, :, None] == np.arange(2 * kw - 1)) & valid[:, :, None]).astype(np.float32)
    base = jnp.einsum('hrd,ckd->hcrk', rpb, onehot, precision=lax.Precision.HIGHEST)
    base = base + np.where(valid, 0.0, MASK_VALUE).astype(np.float32)[:, None, :]
    return jnp.stack([base[:, :, v:v + kh].reshape(rpb.shape[0], w, kh * w) for v in range(kh)], axis=1)


def kernel(x, c, ctx, c_ctx, w_ada, b_ada, norm1_g, w_in, na_rpb, ret_log_decay_fwd, ret_log_decay_bwd,
           w_proj_na, w_proj_ret, w_out, norm2_g, w_up, conv_w, conv_b, w_down, final_norm_g):
    b, n, d = x.shape
    assert w_ada.shape[0] == 1, "single layer"
    row = lambda v: v.reshape(1, -1)

    cc = jnp.concatenate([c, c_ctx[None], jnp.zeros((ADA_ROWS - b - 1, d), F32)], axis=0)
    mod = _ada(cc, w_ada[0], row(b_ada[0]))
    xmod = mod[:b].reshape(b, 1, 6, d)
    sh1, sc1, g1, sh2, sc2, g2 = (xmod[:, :, i] for i in range(6))
    cmod = mod[b:b + 1].reshape(1, 1, 6, d)
    csh1, csc1 = cmod[:, :, 0], cmod[:, :, 1]

    w_in_b = w_in[0].astype(BF16)
    cos, s1, s2 = _rope_tables(n)
    sh1_w = _rowproj(mod[:, :d], w_in_b)[:b].reshape(b, 1, IN_COLS)
    qkv, rqk, rv, rg, gates = _inproj(x, sh1_w, sc1, row(norm1_g[0]), w_in_b, cos, s1, s2)
    ckv, crk, crv = _inproj_ctx(ctx, csh1, csc1, row(norm1_g[0]), w_in_b)

    o_na = _na(qkv, ckv, _na_bias_table(na_rpb[0]))
    o_ret = _ret(ret_log_decay_fwd[0], ret_log_decay_bwd[0], rqk, rv, rg, crk, crv)

    x1 = _merge(x, o_na, o_ret, gates, g1, w_proj_na[0].astype(BF16), w_proj_ret[0].astype(BF16),
                w_out[0].astype(BF16))
    return _ffn(x1, sh2, sc2, g2, row(norm2_g[0]), row(final_norm_g), w_up[0].astype(BF16),
                conv_w[0], row(conv_b[0]), w_down[0].astype(BF16))
```

```python
import math

import numpy as np
import jax
import jax.numpy as jnp
from jax import lax
from jax.experimental import pallas as pl
from jax.experimental.pallas import tpu as pltpu

F32 = jnp.float32
BF16 = jnp.bfloat16

D_MODEL = 1024
GRID_W = 64
NA_HEADS = 8
NA_HEAD_DIM = 64
NA_WIN_H = 8
NA_WIN_W = 16
RET_HEADS = 4
RET_QK_DIM = 128
RET_V_DIM = 256
RET_CHUNK = 128
D_FF = int(math.ceil(8 * D_MODEL / 3 / 128)) * 128
CONV_W = 3
ROPE_BASE = 10000.0
EPS = 1e-6

NA_W = NA_HEADS * NA_HEAD_DIM
RET_QK_W = RET_HEADS * RET_QK_DIM
RET_V_W = RET_HEADS * RET_V_DIM
OFF_Q, OFF_K, OFF_V = 0, NA_W, 2 * NA_W
OFF_RQ = 3 * NA_W
OFF_RK = OFF_RQ + RET_QK_W
OFF_RV = OFF_RK + RET_QK_W
OFF_RG = OFF_RV + RET_V_W
OFF_GA = OFF_RG + RET_V_W
OFF_GB = OFF_GA + D_MODEL
IN_COLS = OFF_GB + D_MODEL

NA_SCALE = NA_HEAD_DIM ** -0.5
RK_SCALE = RET_QK_DIM ** -0.5
MASK_VALUE = -1e30

VMEM_LIMIT_BYTES = 56 * 1024 * 1024
LANES = 128
HALO = 8

TM_PROJ = 512
TM_MERGE = 1024
TM_FFN = 512
ADA_ROWS = 16
ADA_TN = 1536
FF_CHUNK = 256
PROJ_CHUNK = 512
NA_BLOCK = 2
RET_UNROLL = 32
RET_BLOCK = 16


def _cparams(n_axes):
    return pltpu.CompilerParams(dimension_semantics=("arbitrary",) * n_axes,
                                vmem_limit_bytes=VMEM_LIMIT_BYTES)


def _resident(shape):
    nd = len(shape)
    return pl.BlockSpec(shape, lambda *_: (0,) * nd, pipeline_mode=pl.Buffered(1))


def _rms_modulate(x, gain, shift, scale):
    y = x * lax.rsqrt(jnp.mean(x * x, axis=-1, keepdims=True) + EPS)
    return (y * gain) * (1.0 + scale) + shift


def _ada_kernel(c_ref, w_ref, b_ref, o_ref):
    s = jax.nn.silu(c_ref[...])
    o_ref[...] = jnp.dot(s.astype(BF16), w_ref[...].astype(BF16), preferred_element_type=F32) + b_ref[...]


def _ada(cc, w_ada, b_ada):
    n_out = w_ada.shape[1]
    return pl.pallas_call(
        _ada_kernel,
        grid=(n_out // ADA_TN,),
        in_specs=[pl.BlockSpec((ADA_ROWS, D_MODEL), lambda j: (0, 0)),
                  pl.BlockSpec((D_MODEL, ADA_TN), lambda j: (0, j)),
                  pl.BlockSpec((1, ADA_TN), lambda j: (0, j))],
        out_specs=pl.BlockSpec((ADA_ROWS, ADA_TN), lambda j: (0, j)),
        out_shape=jax.ShapeDtypeStruct((ADA_ROWS, n_out), F32),
        compiler_params=_cparams(1),
        name="ada",
    )(cc, w_ada, b_ada)


def _rowproj_kernel(s_ref, w_ref, o_ref):
    o_ref[...] = jnp.dot(s_ref[...].astype(BF16), w_ref[...], preferred_element_type=F32)


def _rowproj(rows, w):
    n_out = w.shape[1]
    tn = n_out // 2
    return pl.pallas_call(
        _rowproj_kernel,
        grid=(n_out // tn,),
        in_specs=[pl.BlockSpec((ADA_ROWS, D_MODEL), lambda j: (0, 0)),
                  pl.BlockSpec((D_MODEL, tn), lambda j: (0, j))],
        out_specs=pl.BlockSpec((ADA_ROWS, tn), lambda j: (0, j)),
        out_shape=jax.ShapeDtypeStruct((ADA_ROWS, n_out), F32),
        compiler_params=_cparams(1),
        name="rowproj",
    )(rows, w)


def _inproj_kernel(x_ref, shw_ref, sc_ref, g_ref, w_ref, cs_ref, s1_ref, s2_ref,
                   qkv_ref, rqk_ref, rv_ref, rg_ref, gate_ref):
    x = x_ref[0]
    xa = (x * (g_ref[...] * (1.0 + sc_ref[0]))).astype(BF16)
    inv_rms = lax.rsqrt(jnp.mean(x * x, axis=-1, keepdims=True) + EPS)

    def proj(c0):
        return (jnp.dot(xa, w_ref[:, c0:c0 + PROJ_CHUNK], preferred_element_type=F32) * inv_rms
                + shw_ref[0, :, c0:c0 + PROJ_CHUNK])

    for j in range(2 * D_MODEL // PROJ_CHUNK):
        r = proj(OFF_GA + j * PROJ_CHUNK)
        gate_ref[0, :, j * PROJ_CHUNK:(j + 1) * PROJ_CHUNK] = jax.nn.sigmoid(r).astype(BF16)
    for j in range(RET_V_W // PROJ_CHUNK):
        r = proj(OFF_RG + j * PROJ_CHUNK)
        rg_ref[0, :, j * PROJ_CHUNK:(j + 1) * PROJ_CHUNK] = jax.nn.silu(r).astype(BF16)

    cs, s1, s2 = cs_ref[...], s1_ref[...], s2_ref[...]
    quarter = RET_QK_DIM // 4
    for j in range(2 * RET_QK_W // PROJ_CHUNK):
        r = proj(OFF_RQ + j * PROJ_CHUNK)
        for hh in range(PROJ_CHUNK // RET_QK_DIM):
            t = r[:, hh * RET_QK_DIM:(hh + 1) * RET_QK_DIM]
            t = (t * cs + pltpu.roll(t, RET_QK_DIM - quarter, 1) * s1
                 + pltpu.roll(t, quarter, 1) * s2)
            if j * PROJ_CHUNK >= RET_QK_W:
                t = t * RK_SCALE
            c0 = j * PROJ_CHUNK + hh * RET_QK_DIM
            rqk_ref[0, :, c0:c0 + RET_QK_DIM] = t.astype(BF16)

    for j in range(3 * NA_W // PROJ_CHUNK):
        r = proj(OFF_Q + j * PROJ_CHUNK)
        if j * PROJ_CHUNK < NA_W:
            r = r * NA_SCALE
        qkv_ref[0, :, j * PROJ_CHUNK:(j + 1) * PROJ_CHUNK] = r.astype(BF16)
    for j in range(RET_V_W // PROJ_CHUNK):
        rv_ref[0, :, j * PROJ_CHUNK:(j + 1) * PROJ_CHUNK] = proj(OFF_RV + j * PROJ_CHUNK).astype(BF16)


def _inproj(x, shift_w, scale, gain, w_in, cs, s1, s2):
    b, t, _ = x.shape
    tm = TM_PROJ
    mod_map = lambda bi, i: (bi, 0, 0)
    tok = lambda w: pl.BlockSpec((1, tm, w), lambda bi, i: (bi, i, 0))
    tab = pl.BlockSpec((tm, RET_QK_DIM), lambda bi, i: (i, 0))
    out_w = (3 * NA_W, 2 * RET_QK_W, RET_V_W, RET_V_W, 2 * D_MODEL)
    return pl.pallas_call(
        _inproj_kernel,
        grid=(b, t // tm),
        in_specs=[tok(D_MODEL),
                  pl.BlockSpec((1, 1, IN_COLS), mod_map),
                  pl.BlockSpec((1, 1, D_MODEL), mod_map),
                  _resident((1, D_MODEL)),
                  _resident((D_MODEL, IN_COLS)),
                  tab, tab, tab],
        out_specs=[tok(w) for w in out_w],
        out_shape=[jax.ShapeDtypeStruct((b, t, w), BF16) for w in out_w],
        compiler_params=_cparams(2),
        name="inproj",
    )(x, shift_w, scale, gain, w_in, cs, s1, s2)


def _inproj_ctx_kernel(x_ref, sh_ref, sc_ref, g_ref, w_ref, kv_ref, rk_ref, rv_ref):
    h = _rms_modulate(x_ref[0], g_ref[...], sh_ref[0], sc_ref[0]).astype(BF16)

    def proj(c0):
        return jnp.dot(h, w_ref[:, c0:c0 + PROJ_CHUNK], preferred_element_type=F32)

    for j in range(2 * NA_W // PROJ_CHUNK):
        kv_ref[0, :, j * PROJ_CHUNK:(j + 1) * PROJ_CHUNK] = proj(OFF_K + j * PROJ_CHUNK).astype(BF16)
    for j in range(RET_QK_W // PROJ_CHUNK):
        rk_ref[0, :, j * PROJ_CHUNK:(j + 1) * PROJ_CHUNK] = (
            proj(OFF_RK + j * PROJ_CHUNK) * RK_SCALE).astype(BF16)
    for j in range(RET_V_W // PROJ_CHUNK):
        rv_ref[0, :, j * PROJ_CHUNK:(j + 1) * PROJ_CHUNK] = proj(OFF_RV + j * PROJ_CHUNK).astype(BF16)


def _inproj_ctx(ctx, shift, scale, gain, w_in):
    b, t, _ = ctx.shape
    tok = lambda w: pl.BlockSpec((1, t, w), lambda bi: (bi, 0, 0))
    mod = pl.BlockSpec((1, 1, D_MODEL), lambda bi: (0, 0, 0))
    out_w = (2 * NA_W, RET_QK_W, RET_V_W)
    return pl.pallas_call(
        _inproj_ctx_kernel,
        grid=(b,),
        in_specs=[tok(D_MODEL), mod, mod, _resident((1, D_MODEL)), _resident((D_MODEL, IN_COLS))],
        out_specs=[tok(w) for w in out_w],
        out_shape=[jax.ShapeDtypeStruct((b, t, w), BF16) for w in out_w],
        compiler_params=_cparams(1),
        name="inproj_ctx",
    )(ctx, shift, scale, gain, w_in)


def _na_kernel(q_ref, k_ref, v_ref, kc_ref, vc_ref, bias_ref, o_ref, vext, vcext, s_scr, m_scr, p_scr, oc_scr):
    n_rows = q_ref.shape[1] // GRID_W
    n_blk = n_rows // NA_BLOCK
    win = NA_WIN_H * GRID_W
    width = s_scr.shape[2]
    pair = 2 * GRID_W
    first = lax.broadcasted_iota(jnp.int32, (GRID_W, LANES), 1) < NA_HEAD_DIM
    nt = (((1,), (1,)), ((), ()))
    kc = kc_ref[0]
    vext[:, :LANES] = v_ref[0]
    vext[:, LANES:] = jnp.ones((vext.shape[0], LANES), BF16)
    vcext[:, :LANES] = vc_ref[0]
    vcext[:, LANES:] = jnp.ones((vcext.shape[0], LANES), BF16)

    def window(r):
        rs = jnp.clip(r - NA_WIN_H // 2, 0, n_rows - NA_WIN_H)
        return (pl.ds(pl.multiple_of(r * GRID_W, GRID_W), GRID_W),
                pl.ds(pl.multiple_of(rs * GRID_W, GRID_W), win), rs - r + (NA_WIN_H - 1))

    def score_stage(blk, slot):
        base = blk * NA_BLOCK
        stacked = []
        for k in range(NA_BLOCK):
            q = q_ref[0, window(base + k)[0], :]
            zero = jnp.zeros_like(q)
            stacked.append(jnp.concatenate([jnp.where(first, q, zero), jnp.where(first, zero, q)], axis=0))
        s_ctx = lax.dot_general(jnp.concatenate(stacked, axis=0), kc, nt, preferred_element_type=F32)
        s_scr[slot, :, win:] = s_ctx
        for k in range(NA_BLOCK):
            _, wrows, var = window(base + k)
            s_loc = lax.dot_general(stacked[k], k_ref[0, wrows, :], nt, preferred_element_type=F32)
            for hh in range(2):
                dst = slice(k * pair + hh * GRID_W, k * pair + (hh + 1) * GRID_W)
                sl = s_loc[hh * GRID_W:(hh + 1) * GRID_W] + bias_ref[hh, var]
                s_scr[slot, dst, :win] = sl
                mx = jnp.maximum(jnp.max(sl, axis=-1, keepdims=True),
                                 jnp.max(s_ctx[dst], axis=-1, keepdims=True))
                m_scr[slot, dst, :] = jnp.broadcast_to(mx, (GRID_W, LANES))

    def prob_stage(slot):
        for k in range(NA_BLOCK):
            dst = slice(k * pair, (k + 1) * pair)
            mx = m_scr[slot, dst, :]
            for j in range(width // LANES):
                cols = slice(j * LANES, (j + 1) * LANES)
                p_scr[slot, dst, cols] = jnp.exp(s_scr[slot, dst, cols] - mx).astype(BF16)

    def out_stage(blk, slot):
        oc_scr[...] = jnp.dot(p_scr[slot, :, win:], vcext[...], preferred_element_type=F32)
        for k in range(NA_BLOCK):
            rows, wrows, _ = window(blk * NA_BLOCK + k)
            dst = slice(k * pair, (k + 1) * pair)
            o = (jnp.dot(p_scr[slot, dst, :win], vext[wrows, :], preferred_element_type=F32)
                 + oc_scr[dst, :])
            num = jnp.where(first, o[:GRID_W, :LANES], o[GRID_W:, :LANES])
            den = jnp.where(first, o[:GRID_W, LANES:], o[GRID_W:, LANES:])
            o_ref[0, rows, :] = (num / den).astype(BF16)

    assert n_blk % 2 == 0 and n_blk >= 4
    score_stage(0, 0)
    score_stage(1, 1)
    prob_stage(0)

    def step(t, carry):
        blk = 2 * t
        out_stage(blk, 0)
        prob_stage(1)
        score_stage(blk + 2, 0)
        out_stage(blk + 1, 1)
        prob_stage(0)
        score_stage(blk + 3, 1)
        return carry

    lax.fori_loop(0, n_blk // 2 - 1, step, 0, unroll=True)
    out_stage(n_blk - 2, 0)
    prob_stage(1)
    out_stage(n_blk - 1, 1)


def _na(qkv, ckv, bias):
    b, n, _ = qkv.shape
    ctx = ckv.shape[1]
    pairs = NA_W // LANES
    win = NA_WIN_H * GRID_W
    stage_rows = NA_BLOCK * 2 * GRID_W
    blk = lambda t, off: pl.BlockSpec((1, t, LANES), lambda p, bi: (bi, 0, off + p))
    return pl.pallas_call(
        _na_kernel,
        grid=(pairs, b),
        in_specs=[blk(n, 0), blk(n, pairs), blk(n, 2 * pairs),
                  blk(ctx, 0), blk(ctx, pairs),
                  pl.BlockSpec((2, NA_WIN_H, GRID_W, win), lambda p, bi: (p, 0, 0, 0))],
        out_specs=pl.BlockSpec((1, n, LANES), lambda p, bi: (bi, 0, p)),
        out_shape=jax.ShapeDtypeStruct((b, n, NA_W), BF16),
        scratch_shapes=[pltpu.VMEM((n, 2 * LANES), BF16), pltpu.VMEM((ctx, 2 * LANES), BF16),
                        pltpu.VMEM((2, stage_rows, win + ctx), F32), pltpu.VMEM((2, stage_rows, LANES), F32),
                        pltpu.VMEM((2, stage_rows, win + ctx), BF16), pltpu.VMEM((stage_rows, 2 * LANES), F32)],
        compiler_params=_cparams(2),
        name="na",
    )(qkv, qkv, qkv, ckv, ckv, bias)


def _ret_kernel(lgf_ref, lgb_ref, q_ref, k_ref, v_ref, g_ref, ck_ref, cv_ref, o_ref,
                dmat, xi_f, xi_b, zeta_f, zeta_b, gl_f, gl_b, sb_scr, p_scr, o_acc):
    L = RET_CHUNK
    hd = pl.program_id(1)
    lgf = lgf_ref[hd]
    lgb = lgb_ref[hd]
    n_chunks = q_ref.shape[1] // L
    c_chunks = ck_ref.shape[1] // L

    li = lax.broadcasted_iota(jnp.int32, (L, L), 0).astype(F32)
    mi = lax.broadcasted_iota(jnp.int32, (L, L), 1).astype(F32)
    diff = li - mi
    dmat[...] = jnp.where(diff >= 0.0, jnp.exp(lgf * jnp.maximum(diff, 0.0)),
                          jnp.exp(lgb * jnp.maximum(-diff, 0.0)))
    zeta_f[...] = jnp.exp(lgf * (L - 1.0 - li))
    zeta_b[...] = jnp.exp(lgb * li)
    lv = lax.broadcasted_iota(jnp.int32, (L, RET_V_DIM), 0).astype(F32)
    xi_f[...] = jnp.exp(lgf * (lv + 1.0))
    xi_b[...] = jnp.exp(lgb * (L - lv))
    gl_f[...] = jnp.exp(jnp.full((L, RET_V_DIM), L, F32) * lgf)
    gl_b[...] = jnp.exp(jnp.full((L, RET_V_DIM), L, F32) * lgb)

    tn = (((0,), (0,)), ((), ()))
    nt = (((1,), (1,)), ((), ()))

    def advance(state, gl, zeta, k, v):
        kz = (k.astype(F32) * zeta[...]).astype(BF16)
        return gl[...] * state + lax.dot_general(kz, v, tn, preferred_element_type=F32)

    sf = jnp.zeros((RET_QK_DIM, RET_V_DIM), F32)
    sb = jnp.zeros((RET_QK_DIM, RET_V_DIM), F32)
    for c in range(c_chunks):
        sf = advance(sf, gl_f, zeta_f, ck_ref[0, c * L:(c + 1) * L, :], cv_ref[0, c * L:(c + 1) * L, :])
    for c in reversed(range(c_chunks)):
        sb = advance(sb, gl_b, zeta_b, ck_ref[0, c * L:(c + 1) * L, :], cv_ref[0, c * L:(c + 1) * L, :])

    chunk = lambda c: pl.ds(pl.multiple_of(c * L, L), L)

    def bwd_states(i, sb):
        cb = n_chunks - 1 - i
        sb_scr[cb] = sb.astype(BF16)
        return advance(sb, gl_b, zeta_b, k_ref[0, chunk(cb), :], v_ref[0, chunk(cb), :])

    lax.fori_loop(0, n_chunks, bwd_states, sb, unroll=RET_UNROLL)

    def probs(c):
        rows = chunk(c)
        a = lax.dot_general(q_ref[0, rows, :], k_ref[0, rows, :], nt, preferred_element_type=F32)
        p_scr[rows, :] = (a * dmat[...]).astype(BF16)

    def outs(c, sf):
        rows = chunk(c)
        q = q_ref[0, rows, :]
        v = v_ref[0, rows, :]
        o_acc[rows, :] = (jnp.dot(p_scr[rows, :], v, preferred_element_type=F32)
                          + jnp.dot(q, sf.astype(BF16), preferred_element_type=F32) * xi_f[...]
                          + jnp.dot(q, sb_scr[c], preferred_element_type=F32) * xi_b[...])
        return advance(sf, gl_f, zeta_f, k_ref[0, rows, :], v)

    def norm(c):
        rows = chunk(c)
        o = o_acc[rows, :]
        mu = jnp.mean(o, axis=-1, keepdims=True)
        d = o - mu
        var = jnp.mean(d * d, axis=-1, keepdims=True)
        o_ref[0, rows, :] = (d * lax.rsqrt(var + EPS) * g_ref[0, rows, :].astype(F32)).astype(BF16)

    n_blk = n_chunks // RET_BLOCK
    block = lambda t: [t * RET_BLOCK + i for i in range(RET_BLOCK)]

    def outs_block(t, sf):
        for c in block(t):
            sf = outs(c, sf)
        return sf

    for c in block(0):
        probs(c)
    sf = outs_block(0, sf)
    for c in block(1):
        probs(c)

    def step(t, sf):
        for c in block(t - 2):
            norm(c)
        sf = outs_block(t - 1, sf)
        for c in block(t):
            probs(c)
        return sf

    sf = lax.fori_loop(2, n_blk, step, sf, unroll=True)
    for c in block(n_blk - 2):
        norm(c)
    outs_block(n_blk - 1, sf)
    for c in block(n_blk - 1):
        norm(c)


def _ret(lgf, lgb, rqk, rv, rg, crk, crv):
    b, n, _ = rqk.shape
    ctx = crk.shape[1]
    L = RET_CHUNK
    smem = pl.BlockSpec(memory_space=pltpu.SMEM)
    qk = lambda t, off: pl.BlockSpec((1, t, RET_QK_DIM), lambda bi, h: (bi, 0, off + h))
    vv = lambda t: pl.BlockSpec((1, t, RET_V_DIM), lambda bi, h: (bi, 0, h))
    vm = lambda r, c: pltpu.VMEM((r, c), F32)
    st = pltpu.VMEM((n // L, RET_QK_DIM, RET_V_DIM), BF16)
    return pl.pallas_call(
        _ret_kernel,
        grid=(b, RET_HEADS),
        in_specs=[smem, smem, qk(n, 0), qk(n, RET_HEADS), vv(n), vv(n), qk(ctx, 0), vv(ctx)],
        out_specs=vv(n),
        out_shape=jax.ShapeDtypeStruct((b, n, RET_V_W), BF16),
        scratch_shapes=[vm(L, L), vm(L, RET_V_DIM), vm(L, RET_V_DIM), vm(L, L), vm(L, L),
                        vm(L, RET_V_DIM), vm(L, RET_V_DIM), st,
                        pltpu.VMEM((n, L), BF16), vm(n, RET_V_DIM)],
        compiler_params=_cparams(2),
        name="ret",
    )(lgf, lgb, rqk, rqk, rv, rg, crk, crv)


def _merge_kernel(x_ref, ona_ref, oret_ref, gate_ref, g1_ref, wna_ref, wret_ref, wout_ref, o_ref):
    a = jnp.dot(ona_ref[0], wna_ref[...], preferred_element_type=F32)
    r = jnp.dot(oret_ref[0], wret_ref[...], preferred_element_type=F32)
    merged = (gate_ref[0, :, :D_MODEL].astype(F32) * a
              + gate_ref[0, :, D_MODEL:].astype(F32) * r).astype(BF16)
    y = jnp.dot(merged, wout_ref[...], preferred_element_type=F32)
    o_ref[0] = x_ref[0] + g1_ref[0] * y


def _merge(x, o_na, o_ret, gates, g1, w_na, w_ret, w_out):
    b, n, _ = x.shape
    tm = TM_MERGE
    tok = lambda w: pl.BlockSpec((1, tm, w), lambda bi, i: (bi, i, 0))
    return pl.pallas_call(
        _merge_kernel,
        grid=(b, n // tm),
        in_specs=[tok(D_MODEL), tok(NA_W), tok(RET_V_W), tok(2 * D_MODEL),
                  pl.BlockSpec((1, 1, D_MODEL), lambda bi, i: (bi, 0, 0)),
                  _resident((NA_W, D_MODEL)), _resident((RET_V_W, D_MODEL)),
                  _resident((D_MODEL, D_MODEL))],
        out_specs=tok(D_MODEL),
        out_shape=jax.ShapeDtypeStruct((b, n, D_MODEL), F32),
        compiler_params=_cparams(2),
        name="merge",
    )(x, o_na, o_ret, gates, g1, w_na, w_ret, w_out)


def _ffn_kernel(x_ref, xp_ref, xn_ref, sh_ref, sc_ref, g2_ref, gn_ref, gf_ref,
                wup_ref, cw_ref, cb_ref, wdown_ref, o_ref, h_scr, u_scr, y_scr):
    i = pl.program_id(1)
    last = pl.num_programs(1) - 1
    tm = x_ref.shape[1]
    rows = tm + 2 * HALO
    n_ch = D_FF // FF_CHUNK
    x = x_ref[0]
    xa = jnp.concatenate([xp_ref[0], x, xn_ref[0]], axis=0)
    h = _rms_modulate(xa, gn_ref[...], sh_ref[0], sc_ref[0])
    ridx = lax.broadcasted_iota(jnp.int32, (rows, 1), 0)
    valid = jnp.logical_and(jnp.logical_or(ridx >= HALO, i > 0),
                            jnp.logical_or(ridx < tm + HALO, i < last))
    h_scr[...] = jnp.where(valid, h, 0.0).astype(BF16)

    def up(j, slot):
        for g in range(2):
            c0 = g * D_FF + j * FF_CHUNK
            u_scr[slot, g] = jnp.dot(h_scr[...], wup_ref[:, c0:c0 + FF_CHUNK], preferred_element_type=F32)

    def act(j, slot):
        def conv(g):
            c0 = g * D_FF + j * FF_CHUNK
            u = u_scr[slot, g]
            w = cw_ref[:, c0:c0 + FF_CHUNK]
            y = (pltpu.roll(u, 1, 0) * w[0:1] + u * w[1:2] + pltpu.roll(u, rows - 1, 0) * w[2:3]
                 + cb_ref[:, c0:c0 + FF_CHUNK])
            return y[HALO:HALO + tm]

        y_scr[:, j * FF_CHUNK:(j + 1) * FF_CHUNK] = (jax.nn.silu(conv(0)) * conv(1)).astype(BF16)

    def down(j0, j1):
        c0, c1 = j0 * FF_CHUNK, j1 * FF_CHUNK
        return jnp.dot(y_scr[:, c0:c1], wdown_ref[c0:c1, :], preferred_element_type=F32)

    up(0, 0)
    up(1, 1)
    acc = None
    for j in range(n_ch):
        act(j, j % 3)
        if j + 2 < n_ch:
            up(j + 2, (j + 2) % 3)
        if j % 3 == 0 and j > 0:
            part = down(j - 3, j)
            acc = part if acc is None else acc + part
    j0 = (n_ch - 1) // 3 * 3
    acc = acc + down(j0, n_ch)
    z = x + g2_ref[0] * acc
    o_ref[0] = z * lax.rsqrt(jnp.mean(z * z, axis=-1, keepdims=True) + EPS) * gf_ref[...]


def _ffn(x, sh2, sc2, g2, gain2, gain_f, w_up, conv_w, conv_b, w_down):
    b, n, _ = x.shape
    tm = TM_FFN
    per = tm // HALO
    n_halo = n // HALO
    rows = tm + 2 * HALO
    tok = pl.BlockSpec((1, tm, D_MODEL), lambda bi, i: (bi, i, 0))
    mod = pl.BlockSpec((1, 1, D_MODEL), lambda bi, i: (bi, 0, 0))
    prev = pl.BlockSpec((1, HALO, D_MODEL), lambda bi, i: (bi, jnp.maximum(i * per - 1, 0), 0))
    nxt = pl.BlockSpec((1, HALO, D_MODEL), lambda bi, i: (bi, jnp.minimum((i + 1) * per, n_halo - 1), 0))
    return pl.pallas_call(
        _ffn_kernel,
        grid=(b, n // tm),
        in_specs=[tok, prev, nxt, mod, mod, mod,
                  _resident((1, D_MODEL)), _resident((1, D_MODEL)),
                  _resident(w_up.shape), _resident(conv_w.shape),
                  _resident(conv_b.shape), _resident((D_FF, D_MODEL))],
        out_specs=tok,
        out_shape=jax.ShapeDtypeStruct((b, n, D_MODEL), F32),
        scratch_shapes=[pltpu.VMEM((rows, D_MODEL), BF16),
                        pltpu.VMEM((3, 2, rows, FF_CHUNK), F32),
                        pltpu.VMEM((tm, D_FF), BF16)],
        compiler_params=_cparams(2),
        name="ffn",
    )(x, x, x, sh2, sc2, g2, gain2, gain_f, w_up, conv_w, conv_b, w_down)


def _rope_tables(n):
    nf = RET_QK_DIM // 4
    inv = (ROPE_BASE ** (-np.arange(nf, dtype=np.float32) / nf)).astype(np.float32)
    t = np.arange(n)
    ang_r = (t // GRID_W).astype(np.float32)[:, None] * inv[None, :]
    ang_c = (t % GRID_W).astype(np.float32)[:, None] * inv[None, :]
    cos = np.concatenate([np.cos(ang_r)] * 2 + [np.cos(ang_c)] * 2, axis=-1)
    zero = np.zeros((n, nf), np.float32)
    s1 = np.concatenate([-np.sin(ang_r), zero, -np.sin(ang_c), zero], axis=-1)
    s2 = np.concatenate([zero, np.sin(ang_r), zero, np.sin(ang_c)], axis=-1)
    return tuple(jnp.asarray(a.astype(np.float32)) for a in (cos, s1, s2))


def _na_bias_table(rpb):
    w, kh, kw = GRID_W, NA_WIN_H, NA_WIN_W
    cols = np.arange(w)
    c_start = np.clip(cols - kw // 2, 0, w - kw)
    kcol = np.arange(w)
    valid = (kcol[None, :] >= c_start[:, None]) & (kcol[None, :] < c_start[:, None] + kw)
    dc = kcol[None, :] - cols[:, None] + (kw - 1)
    onehot = ((dc[:, :, None] == np.arange(2 * kw - 1)) & valid[:, :, None]).astype(np.float32)
    base = jnp.einsum('hrd,ckd->hcrk', rpb, onehot, precision=lax.Precision.HIGHEST)
    base = base + np.where(valid, 0.0, MASK_VALUE).astype(np.float32)[:, None, :]
    return jnp.stack([base[:, :, v:v + kh].reshape(rpb.shape[0], w, kh * w) for v in range(kh)], axis=1)


def kernel(x, c, ctx, c_ctx, w_ada, b_ada, norm1_g, w_in, na_rpb, ret_log_decay_fwd, ret_log_decay_bwd,
           w_proj_na, w_proj_ret, w_out, norm2_g, w_up, conv_w, conv_b, w_down, final_norm_g):
    b, n, d = x.shape
    assert w_ada.shape[0] == 1, "single layer"
    row = lambda v: v.reshape(1, -1)

    cc = jnp.concatenate([c, c_ctx[None], jnp.zeros((ADA_ROWS - b - 1, d), F32)], axis=0)
    mod = _ada(cc, w_ada[0], row(b_ada[0]))
    xmod = mod[:b].reshape(b, 1, 6, d)
    sh1, sc1, g1, sh2, sc2, g2 = (xmod[:, :, i] for i in range(6))
    cmod = mod[b:b + 1].reshape(1, 1, 6, d)
    csh1, csc1 = cmod[:, :, 0], cmod[:, :, 1]

    w_in_b = w_in[0].astype(BF16)
    cos, s1, s2 = _rope_tables(n)
    sh1_w = _rowproj(mod[:, :d], w_in_b)[:b].reshape(b, 1, IN_COLS)
    qkv, rqk, rv, rg, gates = _inproj(x, sh1_w, sc1, row(norm1_g[0]), w_in_b, cos, s1, s2)
    ckv, crk, crv = _inproj_ctx(ctx, csh1, csc1, row(norm1_g[0]), w_in_b)

    o_na = _na(qkv, ckv, _na_bias_table(na_rpb[0]))
    o_ret = _ret(ret_log_decay_fwd[0], ret_log_decay_bwd[0], rqk, rv, rg, crk, crv)

    x1 = _merge(x, o_na, o_ret, gates, g1, w_proj_na[0].astype(BF16), w_proj_ret[0].astype(BF16),
                w_out[0].astype(BF16))
    return _ffn(x1, sh2, sc2, g2, row(norm2_g[0]), row(final_norm_g), w_up[0].astype(BF16),
                conv_w[0], row(conv_b[0]), w_down[0].astype(BF16))
```

```python
import math

import numpy as np
import jax
import jax.numpy as jnp
from jax import lax
from jax.experimental import pallas as pl
from jax.experimental.pallas import tpu as pltpu

F32 = jnp.float32
BF16 = jnp.bfloat16

D_MODEL = 1024
GRID_W = 64
NA_HEADS = 8
NA_HEAD_DIM = 64
NA_WIN_H = 8
NA_WIN_W = 16
RET_HEADS = 4
RET_QK_DIM = 128
RET_V_DIM = 256
RET_CHUNK = 128
D_FF = int(math.ceil(8 * D_MODEL / 3 / 128)) * 128
CONV_W = 3
ROPE_BASE = 10000.0
EPS = 1e-6

NA_W = NA_HEADS * NA_HEAD_DIM
RET_QK_W = RET_HEADS * RET_QK_DIM
RET_V_W = RET_HEADS * RET_V_DIM
OFF_Q, OFF_K, OFF_V = 0, NA_W, 2 * NA_W
OFF_RQ = 3 * NA_W
OFF_RK = OFF_RQ + RET_QK_W
OFF_RV = OFF_RK + RET_QK_W
OFF_RG = OFF_RV + RET_V_W
OFF_GA = OFF_RG + RET_V_W
OFF_GB = OFF_GA + D_MODEL
IN_COLS = OFF_GB + D_MODEL

NA_SCALE = NA_HEAD_DIM ** -0.5
RK_SCALE = RET_QK_DIM ** -0.5
MASK_VALUE = -1e30

VMEM_LIMIT_BYTES = 56 * 1024 * 1024
LANES = 128
HALO = 8

TM_PROJ = 512
TM_MERGE = 1024
TM_FFN = 512
ADA_ROWS = 16
ADA_TN = 1536
FF_CHUNK = 256
PROJ_CHUNK = 512
NA_BLOCK = 2
RET_UNROLL = 32
RET_BLOCK = 16


def _cparams(n_axes):
    return pltpu.CompilerParams(dimension_semantics=("arbitrary",) * n_axes,
                                vmem_limit_bytes=VMEM_LIMIT_BYTES)


def _resident(shape):
    nd = len(shape)
    return pl.BlockSpec(shape, lambda *_: (0,) * nd, pipeline_mode=pl.Buffered(1))


def _rms_modulate(x, gain, shift, scale):
    y = x * lax.rsqrt(jnp.mean(x * x, axis=-1, keepdims=True) + EPS)
    return (y * gain) * (1.0 + scale) + shift


def _ada_kernel(c_ref, w_ref, b_ref, o_ref):
    s = jax.nn.silu(c_ref[...])
    o_ref[...] = jnp.dot(s.astype(BF16), w_ref[...].astype(BF16), preferred_element_type=F32) + b_ref[...]


def _ada(cc, w_ada, b_ada):
    n_out = w_ada.shape[1]
    return pl.pallas_call(
        _ada_kernel,
        grid=(n_out // ADA_TN,),
        in_specs=[pl.BlockSpec((ADA_ROWS, D_MODEL), lambda j: (0, 0)),
                  pl.BlockSpec((D_MODEL, ADA_TN), lambda j: (0, j)),
                  pl.BlockSpec((1, ADA_TN), lambda j: (0, j))],
        out_specs=pl.BlockSpec((ADA_ROWS, ADA_TN), lambda j: (0, j)),
        out_shape=jax.ShapeDtypeStruct((ADA_ROWS, n_out), F32),
        compiler_params=_cparams(1),
        name="ada",
    )(cc, w_ada, b_ada)


def _rowproj_kernel(s_ref, w_ref, o_ref):
    o_ref[...] = jnp.dot(s_ref[...].astype(BF16), w_ref[...], preferred_element_type=F32)


def _rowproj(rows, w):
    n_out = w.shape[1]
    tn = n_out // 2
    return pl.pallas_call(
        _rowproj_kernel,
        grid=(n_out // tn,),
        in_specs=[pl.BlockSpec((ADA_ROWS, D_MODEL), lambda j: (0, 0)),
                  pl.BlockSpec((D_MODEL, tn), lambda j: (0, j))],
        out_specs=pl.BlockSpec((ADA_ROWS, tn), lambda j: (0, j)),
        out_shape=jax.ShapeDtypeStruct((ADA_ROWS, n_out), F32),
        compiler_params=_cparams(1),
        name="rowproj",
    )(rows, w)


def _inproj_kernel(x_ref, shw_ref, sc_ref, g_ref, w_ref, cs_ref, s1_ref, s2_ref,
                   qkv_ref, rqk_ref, rv_ref, rg_ref, gate_ref):
    x = x_ref[0]
    xa = (x * (g_ref[...] * (1.0 + sc_ref[0]))).astype(BF16)
    inv_rms = lax.rsqrt(jnp.mean(x * x, axis=-1, keepdims=True) + EPS)

    def proj(c0):
        return (jnp.dot(xa, w_ref[:, c0:c0 + PROJ_CHUNK], preferred_element_type=F32) * inv_rms
                + shw_ref[0, :, c0:c0 + PROJ_CHUNK])

    def store_groups(ref, j, val):
        width = ref.shape[3]
        per = PROJ_CHUNK // width
        for gg in range(per):
            ref[0, j * per + gg] = val[:, gg * width:(gg + 1) * width].astype(BF16)

    for j in range(2 * D_MODEL // PROJ_CHUNK):
        r = proj(OFF_GA + j * PROJ_CHUNK)
        gate_ref[0, :, j * PROJ_CHUNK:(j + 1) * PROJ_CHUNK] = jax.nn.sigmoid(r).astype(BF16)
    for j in range(RET_V_W // PROJ_CHUNK):
        r = proj(OFF_RG + j * PROJ_CHUNK)
        store_groups(rg_ref, j, jax.nn.silu(r))

    cs, s1, s2 = cs_ref[...], s1_ref[...], s2_ref[...]
    quarter = RET_QK_DIM // 4
    for j in range(2 * RET_QK_W // PROJ_CHUNK):
        r = proj(OFF_RQ + j * PROJ_CHUNK)
        for hh in range(PROJ_CHUNK // RET_QK_DIM):
            t = r[:, hh * RET_QK_DIM:(hh + 1) * RET_QK_DIM]
            t = (t * cs + pltpu.roll(t, RET_QK_DIM - quarter, 1) * s1
                 + pltpu.roll(t, quarter, 1) * s2)
            if j * PROJ_CHUNK >= RET_QK_W:
                t = t * RK_SCALE
            rqk_ref[0, j * (PROJ_CHUNK // RET_QK_DIM) + hh] = t.astype(BF16)

    for j in range(3 * NA_W // PROJ_CHUNK):
        r = proj(OFF_Q + j * PROJ_CHUNK)
        if j * PROJ_CHUNK < NA_W:
            r = r * NA_SCALE
        store_groups(qkv_ref, j, r)
    for j in range(RET_V_W // PROJ_CHUNK):
        store_groups(rv_ref, j, proj(OFF_RV + j * PROJ_CHUNK))


def _inproj(x, shift_w, scale, gain, w_in, cs, s1, s2):
    b, t, _ = x.shape
    tm = TM_PROJ
    mod_map = lambda bi, i: (bi, 0, 0)
    tok = lambda w: pl.BlockSpec((1, tm, w), lambda bi, i: (bi, i, 0))
    heads = lambda g, w: pl.BlockSpec((1, g, tm, w), lambda bi, i: (bi, 0, i, 0))
    tab = pl.BlockSpec((tm, RET_QK_DIM), lambda bi, i: (i, 0))
    out_gw = ((3 * NA_W // LANES, LANES), (2 * RET_HEADS, RET_QK_DIM), (RET_HEADS, RET_V_DIM),
              (RET_HEADS, RET_V_DIM))
    return pl.pallas_call(
        _inproj_kernel,
        grid=(b, t // tm),
        in_specs=[tok(D_MODEL),
                  pl.BlockSpec((1, 1, IN_COLS), mod_map),
                  pl.BlockSpec((1, 1, D_MODEL), mod_map),
                  _resident((1, D_MODEL)),
                  _resident((D_MODEL, IN_COLS)),
                  tab, tab, tab],
        out_specs=[heads(g, w) for g, w in out_gw] + [tok(2 * D_MODEL)],
        out_shape=([jax.ShapeDtypeStruct((b, g, t, w), BF16) for g, w in out_gw]
                   + [jax.ShapeDtypeStruct((b, t, 2 * D_MODEL), BF16)]),
        compiler_params=_cparams(2),
        name="inproj",
    )(x, shift_w, scale, gain, w_in, cs, s1, s2)


def _inproj_ctx_kernel(x_ref, sh_ref, sc_ref, g_ref, w_ref, kv_ref, rk_ref, rv_ref):
    h = _rms_modulate(x_ref[0], g_ref[...], sh_ref[0], sc_ref[0]).astype(BF16)

    def proj(c0):
        return jnp.dot(h, w_ref[:, c0:c0 + PROJ_CHUNK], preferred_element_type=F32)

    def store_groups(ref, j, val):
        width = ref.shape[3]
        per = PROJ_CHUNK // width
        for gg in range(per):
            ref[0, j * per + gg] = val[:, gg * width:(gg + 1) * width].astype(BF16)

    for j in range(2 * NA_W // PROJ_CHUNK):
        store_groups(kv_ref, j, proj(OFF_K + j * PROJ_CHUNK))
    for j in range(RET_QK_W // PROJ_CHUNK):
        store_groups(rk_ref, j, proj(OFF_RK + j * PROJ_CHUNK) * RK_SCALE)
    for j in range(RET_V_W // PROJ_CHUNK):
        store_groups(rv_ref, j, proj(OFF_RV + j * PROJ_CHUNK))


def _inproj_ctx(ctx, shift, scale, gain, w_in):
    b, t, _ = ctx.shape
    tok = lambda w: pl.BlockSpec((1, t, w), lambda bi: (bi, 0, 0))
    heads = lambda g, w: pl.BlockSpec((1, g, t, w), lambda bi: (bi, 0, 0, 0))
    mod = pl.BlockSpec((1, 1, D_MODEL), lambda bi: (0, 0, 0))
    out_gw = ((2 * NA_W // LANES, LANES), (RET_HEADS, RET_QK_DIM), (RET_HEADS, RET_V_DIM))
    return pl.pallas_call(
        _inproj_ctx_kernel,
        grid=(b,),
        in_specs=[tok(D_MODEL), mod, mod, _resident((1, D_MODEL)), _resident((D_MODEL, IN_COLS))],
        out_specs=[heads(g, w) for g, w in out_gw],
        out_shape=[jax.ShapeDtypeStruct((b, g, t, w), BF16) for g, w in out_gw],
        compiler_params=_cparams(1),
        name="inproj_ctx",
    )(ctx, shift, scale, gain, w_in)


def _na_kernel(q_ref, k_ref, v_ref, kc_ref, vc_ref, bias_ref, o_ref, vext, vcext, s_scr, m_scr, p_scr, oc_scr):
    n_rows = q_ref.shape[2] // GRID_W
    n_blk = n_rows // NA_BLOCK
    win = NA_WIN_H * GRID_W
    width = s_scr.shape[2]
    pair = 2 * GRID_W
    first = lax.broadcasted_iota(jnp.int32, (GRID_W, LANES), 1) < NA_HEAD_DIM
    nt = (((1,), (1,)), ((), ()))
    kc = kc_ref[0, 0]
    vext[:, :LANES] = v_ref[0, 0]
    vext[:, LANES:] = jnp.ones((vext.shape[0], LANES), BF16)
    vcext[:, :LANES] = vc_ref[0, 0]
    vcext[:, LANES:] = jnp.ones((vcext.shape[0], LANES), BF16)

    def window(r):
        rs = jnp.clip(r - NA_WIN_H // 2, 0, n_rows - NA_WIN_H)
        return (pl.ds(pl.multiple_of(r * GRID_W, GRID_W), GRID_W),
                pl.ds(pl.multiple_of(rs * GRID_W, GRID_W), win), rs - r + (NA_WIN_H - 1))

    def score_stage(blk, slot):
        base = blk * NA_BLOCK
        stacked = []
        for k in range(NA_BLOCK):
            q = q_ref[0, 0, window(base + k)[0], :]
            zero = jnp.zeros_like(q)
            stacked.append(jnp.concatenate([jnp.where(first, q, zero), jnp.where(first, zero, q)], axis=0))
        s_ctx = lax.dot_general(jnp.concatenate(stacked, axis=0), kc, nt, preferred_element_type=F32)
        s_scr[slot, :, win:] = s_ctx
        for k in range(NA_BLOCK):
            _, wrows, var = window(base + k)
            s_loc = lax.dot_general(stacked[k], k_ref[0, 0, wrows, :], nt, preferred_element_type=F32)
            for hh in range(2):
                dst = slice(k * pair + hh * GRID_W, k * pair + (hh + 1) * GRID_W)
                sl = s_loc[hh * GRID_W:(hh + 1) * GRID_W] + bias_ref[hh, var]
                s_scr[slot, dst, :win] = sl
                mx = jnp.maximum(jnp.max(sl, axis=-1, keepdims=True),
                                 jnp.max(s_ctx[dst], axis=-1, keepdims=True))
                m_scr[slot, dst, :] = jnp.broadcast_to(mx, (GRID_W, LANES))

    def prob_stage(slot):
        for k in range(NA_BLOCK):
            dst = slice(k * pair, (k + 1) * pair)
            mx = m_scr[slot, dst, :]
            for j in range(width // LANES):
                cols = slice(j * LANES, (j + 1) * LANES)
                p_scr[slot, dst, cols] = jnp.exp(s_scr[slot, dst, cols] - mx).astype(BF16)

    def out_stage(blk, slot):
        oc_scr[...] = jnp.dot(p_scr[slot, :, win:], vcext[...], preferred_element_type=F32)
        for k in range(NA_BLOCK):
            rows, wrows, _ = window(blk * NA_BLOCK + k)
            dst = slice(k * pair, (k + 1) * pair)
            o = (jnp.dot(p_scr[slot, dst, :win], vext[wrows, :], preferred_element_type=F32)
                 + oc_scr[dst, :])
            num = jnp.where(first, o[:GRID_W, :LANES], o[GRID_W:, :LANES])
            den = jnp.where(first, o[:GRID_W, LANES:], o[GRID_W:, LANES:])
            o_ref[0, 0, rows, :] = (num / den).astype(BF16)

    assert n_blk % 2 == 0 and n_blk >= 4
    score_stage(0, 0)
    score_stage(1, 1)
    prob_stage(0)

    def step(t, carry):
        blk = 2 * t
        out_stage(blk, 0)
        prob_stage(1)
        score_stage(blk + 2, 0)
        out_stage(blk + 1, 1)
        prob_stage(0)
        score_stage(blk + 3, 1)
        return carry

    lax.fori_loop(0, n_blk // 2 - 1, step, 0, unroll=True)
    out_stage(n_blk - 2, 0)
    prob_stage(1)
    out_stage(n_blk - 1, 1)


def _na(qkv, ckv, bias):
    b, _, n, _ = qkv.shape
    ctx = ckv.shape[2]
    pairs = NA_W // LANES
    win = NA_WIN_H * GRID_W
    stage_rows = NA_BLOCK * 2 * GRID_W
    blk = lambda t, off: pl.BlockSpec((1, 1, t, LANES), lambda p, bi: (bi, off + p, 0, 0))
    return pl.pallas_call(
        _na_kernel,
        grid=(pairs, b),
        in_specs=[blk(n, 0), blk(n, pairs), blk(n, 2 * pairs),
                  blk(ctx, 0), blk(ctx, pairs),
                  pl.BlockSpec((2, NA_WIN_H, GRID_W, win), lambda p, bi: (p, 0, 0, 0))],
        out_specs=pl.BlockSpec((1, 1, n, LANES), lambda p, bi: (bi, p, 0, 0)),
        out_shape=jax.ShapeDtypeStruct((b, pairs, n, LANES), BF16),
        scratch_shapes=[pltpu.VMEM((n, 2 * LANES), BF16), pltpu.VMEM((ctx, 2 * LANES), BF16),
                        pltpu.VMEM((2, stage_rows, win + ctx), F32), pltpu.VMEM((2, stage_rows, LANES), F32),
                        pltpu.VMEM((2, stage_rows, win + ctx), BF16), pltpu.VMEM((stage_rows, 2 * LANES), F32)],
        compiler_params=_cparams(2),
        name="na",
    )(qkv, qkv, qkv, ckv, ckv, bias)


def _ret_kernel(lgf_ref, lgb_ref, q_ref, k_ref, v_ref, g_ref, ck_ref, cv_ref, o_ref,
                dmat, xi_f, xi_b, zeta_f, zeta_b, gl_f, gl_b, sb_scr, p_scr, o_acc):
    L = RET_CHUNK
    hd = pl.program_id(1)
    lgf = lgf_ref[hd]
    lgb = lgb_ref[hd]
    n_chunks = q_ref.shape[2] // L
    c_chunks = ck_ref.shape[2] // L

    li = lax.broadcasted_iota(jnp.int32, (L, L), 0).astype(F32)
    mi = lax.broadcasted_iota(jnp.int32, (L, L), 1).astype(F32)
    diff = li - mi
    dmat[...] = jnp.where(diff >= 0.0, jnp.exp(lgf * jnp.maximum(diff, 0.0)),
                          jnp.exp(lgb * jnp.maximum(-diff, 0.0)))
    zeta_f[...] = jnp.exp(lgf * (L - 1.0 - li))
    zeta_b[...] = jnp.exp(lgb * li)
    lv = lax.broadcasted_iota(jnp.int32, (L, RET_V_DIM), 0).astype(F32)
    xi_f[...] = jnp.exp(lgf * (lv + 1.0))
    xi_b[...] = jnp.exp(lgb * (L - lv))
    gl_f[...] = jnp.exp(jnp.full((L, RET_V_DIM), L, F32) * lgf)
    gl_b[...] = jnp.exp(jnp.full((L, RET_V_DIM), L, F32) * lgb)

    tn = (((0,), (0,)), ((), ()))
    nt = (((1,), (1,)), ((), ()))

    def advance(state, gl, zeta, k, v):
        kz = (k.astype(F32) * zeta[...]).astype(BF16)
        return gl[...] * state + lax.dot_general(kz, v, tn, preferred_element_type=F32)

    sf = jnp.zeros((RET_QK_DIM, RET_V_DIM), F32)
    sb = jnp.zeros((RET_QK_DIM, RET_V_DIM), F32)
    for c in range(c_chunks):
        sf = advance(sf, gl_f, zeta_f, ck_ref[0, 0, c * L:(c + 1) * L, :], cv_ref[0, 0, c * L:(c + 1) * L, :])
    for c in reversed(range(c_chunks)):
        sb = advance(sb, gl_b, zeta_b, ck_ref[0, 0, c * L:(c + 1) * L, :], cv_ref[0, 0, c * L:(c + 1) * L, :])

    chunk = lambda c: pl.ds(pl.multiple_of(c * L, L), L)

    def bwd_states(i, sb):
        cb = n_chunks - 1 - i
        sb_scr[cb] = sb.astype(BF16)
        return advance(sb, gl_b, zeta_b, k_ref[0, 0, chunk(cb), :], v_ref[0, 0, chunk(cb), :])

    lax.fori_loop(0, n_chunks, bwd_states, sb, unroll=RET_UNROLL)

    def probs(c):
        rows = chunk(c)
        a = lax.dot_general(q_ref[0, 0, rows, :], k_ref[0, 0, rows, :], nt, preferred_element_type=F32)
        p_scr[rows, :] = (a * dmat[...]).astype(BF16)

    def outs(c, sf):
        rows = chunk(c)
        q = q_ref[0, 0, rows, :]
        v = v_ref[0, 0, rows, :]
        o_acc[rows, :] = (jnp.dot(p_scr[rows, :], v, preferred_element_type=F32)
                          + jnp.dot(q, sf.astype(BF16), preferred_element_type=F32) * xi_f[...]
                          + jnp.dot(q, sb_scr[c], preferred_element_type=F32) * xi_b[...])
        return advance(sf, gl_f, zeta_f, k_ref[0, 0, rows, :], v)

    def norm(c):
        rows = chunk(c)
        o = o_acc[rows, :]
        mu = jnp.mean(o, axis=-1, keepdims=True)
        d = o - mu
        var = jnp.mean(d * d, axis=-1, keepdims=True)
        o_ref[0, 0, rows, :] = (d * lax.rsqrt(var + EPS) * g_ref[0, 0, rows, :].astype(F32)).astype(BF16)

    n_blk = n_chunks // RET_BLOCK
    block = lambda t: [t * RET_BLOCK + i for i in range(RET_BLOCK)]

    def outs_block(t, sf):
        for c in block(t):
            sf = outs(c, sf)
        return sf

    for c in block(0):
        probs(c)
    sf = outs_block(0, sf)
    for c in block(1):
        probs(c)

    def step(t, sf):
        for c in block(t - 2):
            norm(c)
        sf = outs_block(t - 1, sf)
        for c in block(t):
            probs(c)
        return sf

    sf = lax.fori_loop(2, n_blk, step, sf, unroll=True)
    for c in block(n_blk - 2):
        norm(c)
    outs_block(n_blk - 1, sf)
    for c in block(n_blk - 1):
        norm(c)


def _ret(lgf, lgb, rqk, rv, rg, crk, crv):
    b, _, n, _ = rqk.shape
    ctx = crk.shape[2]
    L = RET_CHUNK
    smem = pl.BlockSpec(memory_space=pltpu.SMEM)
    qk = lambda t, off: pl.BlockSpec((1, 1, t, RET_QK_DIM), lambda bi, h: (bi, off + h, 0, 0))
    vv = lambda t: pl.BlockSpec((1, 1, t, RET_V_DIM), lambda bi, h: (bi, h, 0, 0))
    vm = lambda r, c: pltpu.VMEM((r, c), F32)
    st = pltpu.VMEM((n // L, RET_QK_DIM, RET_V_DIM), BF16)
    return pl.pallas_call(
        _ret_kernel,
        grid=(b, RET_HEADS),
        in_specs=[smem, smem, qk(n, 0), qk(n, RET_HEADS), vv(n), vv(n), qk(ctx, 0), vv(ctx)],
        out_specs=vv(n),
        out_shape=jax.ShapeDtypeStruct((b, RET_HEADS, n, RET_V_DIM), BF16),
        scratch_shapes=[vm(L, L), vm(L, RET_V_DIM), vm(L, RET_V_DIM), vm(L, L), vm(L, L),
                        vm(L, RET_V_DIM), vm(L, RET_V_DIM), st,
                        pltpu.VMEM((n, L), BF16), vm(n, RET_V_DIM)],
        compiler_params=_cparams(2),
        name="ret",
    )(lgf, lgb, rqk, rqk, rv, rg, crk, crv)


def _merge_kernel(x_ref, ona_ref, oret_ref, gate_ref, g1_ref, wna_ref, wret_ref, wout_ref, o_ref):
    ona = jnp.concatenate([ona_ref[0, j] for j in range(ona_ref.shape[1])], axis=1)
    oret = jnp.concatenate([oret_ref[0, j] for j in range(oret_ref.shape[1])], axis=1)
    a = jnp.dot(ona, wna_ref[...], preferred_element_type=F32)
    r = jnp.dot(oret, wret_ref[...], preferred_element_type=F32)
    merged = (gate_ref[0, :, :D_MODEL].astype(F32) * a
              + gate_ref[0, :, D_MODEL:].astype(F32) * r).astype(BF16)
    y = jnp.dot(merged, wout_ref[...], preferred_element_type=F32)
    o_ref[0] = x_ref[0] + g1_ref[0] * y


def _merge(x, o_na, o_ret, gates, g1, w_na, w_ret, w_out):
    b, n, _ = x.shape
    tm = TM_MERGE
    tok = lambda w: pl.BlockSpec((1, tm, w), lambda bi, i: (bi, i, 0))
    return pl.pallas_call(
        _merge_kernel,
        grid=(b, n // tm),
        in_specs=[tok(D_MODEL),
                  pl.BlockSpec((1, NA_W // LANES, tm, LANES), lambda bi, i: (bi, 0, i, 0)),
                  pl.BlockSpec((1, RET_HEADS, tm, RET_V_DIM), lambda bi, i: (bi, 0, i, 0)),
                  tok(2 * D_MODEL),
                  pl.BlockSpec((1, 1, D_MODEL), lambda bi, i: (bi, 0, 0)),
                  _resident((NA_W, D_MODEL)), _resident((RET_V_W, D_MODEL)),
                  _resident((D_MODEL, D_MODEL))],
        out_specs=tok(D_MODEL),
        out_shape=jax.ShapeDtypeStruct((b, n, D_MODEL), F32),
        compiler_params=_cparams(2),
        name="merge",
    )(x, o_na, o_ret, gates, g1, w_na, w_ret, w_out)


def _ffn_kernel(x_ref, xp_ref, xn_ref, sh_ref, sc_ref, g2_ref, gn_ref, gf_ref,
                wup_ref, cw_ref, cb_ref, wdown_ref, o_ref, h_scr, u_scr, y_scr):
    i = pl.program_id(1)
    last = pl.num_programs(1) - 1
    tm = x_ref.shape[1]
    rows = tm + 2 * HALO
    n_ch = D_FF // FF_CHUNK
    x = x_ref[0]
    xa = jnp.concatenate([xp_ref[0], x, xn_ref[0]], axis=0)
    h = _rms_modulate(xa, gn_ref[...], sh_ref[0], sc_ref[0])
    ridx = lax.broadcasted_iota(jnp.int32, (rows, 1), 0)
    valid = jnp.logical_and(jnp.logical_or(ridx >= HALO, i > 0),
                            jnp.logical_or(ridx < tm + HALO, i < last))
    h_scr[...] = jnp.where(valid, h, 0.0).astype(BF16)

    def up(j, slot):
        for g in range(2):
            c0 = g * D_FF + j * FF_CHUNK
            u_scr[slot, g] = jnp.dot(h_scr[...], wup_ref[:, c0:c0 + FF_CHUNK], preferred_element_type=F32)

    def act(j, slot):
        def conv(g):
            c0 = g * D_FF + j * FF_CHUNK
            u = u_scr[slot, g]
            w = cw_ref[:, c0:c0 + FF_CHUNK]
            y = (pltpu.roll(u, 1, 0) * w[0:1] + u * w[1:2] + pltpu.roll(u, rows - 1, 0) * w[2:3]
                 + cb_ref[:, c0:c0 + FF_CHUNK])
            return y[HALO:HALO + tm]

        y_scr[:, j * FF_CHUNK:(j + 1) * FF_CHUNK] = (jax.nn.silu(conv(0)) * conv(1)).astype(BF16)

    def down(j0, j1):
        c0, c1 = j0 * FF_CHUNK, j1 * FF_CHUNK
        return jnp.dot(y_scr[:, c0:c1], wdown_ref[c0:c1, :], preferred_element_type=F32)

    up(0, 0)
    up(1, 1)
    acc = None
    for j in range(n_ch):
        act(j, j % 3)
        if j + 2 < n_ch:
            up(j + 2, (j + 2) % 3)
        if j % 3 == 0 and j > 0:
            part = down(j - 3, j)
            acc = part if acc is None else acc + part
    j0 = (n_ch - 1) // 3 * 3
    acc = acc + down(j0, n_ch)
    z = x + g2_ref[0] * acc
    o_ref[0] = z * lax.rsqrt(jnp.mean(z * z, axis=-1, keepdims=True) + EPS) * gf_ref[...]


def _ffn(x, sh2, sc2, g2, gain2, gain_f, w_up, conv_w, conv_b, w_down):
    b, n, _ = x.shape
    tm = TM_FFN
    per = tm // HALO
    n_halo = n // HALO
    rows = tm + 2 * HALO
    tok = pl.BlockSpec((1, tm, D_MODEL), lambda bi, i: (bi, i, 0))
    mod = pl.BlockSpec((1, 1, D_MODEL), lambda bi, i: (bi, 0, 0))
    prev = pl.BlockSpec((1, HALO, D_MODEL), lambda bi, i: (bi, jnp.maximum(i * per - 1, 0), 0))
    nxt = pl.BlockSpec((1, HALO, D_MODEL), lambda bi, i: (bi, jnp.minimum((i + 1) * per, n_halo - 1), 0))
    return pl.pallas_call(
        _ffn_kernel,
        grid=(b, n // tm),
        in_specs=[tok, prev, nxt, mod, mod, mod,
                  _resident((1, D_MODEL)), _resident((1, D_MODEL)),
                  _resident(w_up.shape), _resident(conv_w.shape),
                  _resident(conv_b.shape), _resident((D_FF, D_MODEL))],
        out_specs=tok,
        out_shape=jax.ShapeDtypeStruct((b, n, D_MODEL), F32),
        scratch_shapes=[pltpu.VMEM((rows, D_MODEL), BF16),
                        pltpu.VMEM((3, 2, rows, FF_CHUNK), F32),
                        pltpu.VMEM((tm, D_FF), BF16)],
        compiler_params=_cparams(2),
        name="ffn",
    )(x, x, x, sh2, sc2, g2, gain2, gain_f, w_up, conv_w, conv_b, w_down)


def _rope_tables(n):
    nf = RET_QK_DIM // 4
    inv = (ROPE_BASE ** (-np.arange(nf, dtype=np.float32) / nf)).astype(np.float32)
    t = np.arange(n)
    ang_r = (t // GRID_W).astype(np.float32)[:, None] * inv[None, :]
    ang_c = (t % GRID_W).astype(np.float32)[:, None] * inv[None, :]
    cos = np.concatenate([np.cos(ang_r)] * 2 + [np.cos(ang_c)] * 2, axis=-1)
    zero = np.zeros((n, nf), np.float32)
    s1 = np.concatenate([-np.sin(ang_r), zero, -np.sin(ang_c), zero], axis=-1)
    s2 = np.concatenate([zero, np.sin(ang_r), zero, np.sin(ang_c)], axis=-1)
    return tuple(jnp.asarray(a.astype(np.float32)) for a in (cos, s1, s2))


def _na_bias_table(rpb):
    w, kh, kw = GRID_W, NA_WIN_H, NA_WIN_W
    cols = np.arange(w)
    c_start = np.clip(cols - kw // 2, 0, w - kw)
    kcol = np.arange(w)
    valid = (kcol[None, :] >= c_start[:, None]) & (kcol[None, :] < c_start[:, None] + kw)
    dc = kcol[None, :] - cols[:, None] + (kw - 1)
    onehot = ((dc[:, :, None] == np.arange(2 * kw - 1)) & valid[:, :, None]).astype(np.float32)
    base = jnp.einsum('hrd,ckd->hcrk', rpb, onehot, precision=lax.Precision.HIGHEST)
    base = base + np.where(valid, 0.0, MASK_VALUE).astype(np.float32)[:, None, :]
    return jnp.stack([base[:, :, v:v + kh].reshape(rpb.shape[0], w, kh * w) for v in range(kh)], axis=1)


def kernel(x, c, ctx, c_ctx, w_ada, b_ada, norm1_g, w_in, na_rpb, ret_log_decay_fwd, ret_log_decay_bwd,
           w_proj_na, w_proj_ret, w_out, norm2_g, w_up, conv_w, conv_b, w_down, final_norm_g):
    b, n, d = x.shape
    assert w_ada.shape[0] == 1, "single layer"
    row = lambda v: v.reshape(1, -1)

    cc = jnp.concatenate([c, c_ctx[None], jnp.zeros((ADA_ROWS - b - 1, d), F32)], axis=0)
    mod = _ada(cc, w_ada[0], row(b_ada[0]))
    xmod = mod[:b].reshape(b, 1, 6, d)
    sh1, sc1, g1, sh2, sc2, g2 = (xmod[:, :, i] for i in range(6))
    cmod = mod[b:b + 1].reshape(1, 1, 6, d)
    csh1, csc1 = cmod[:, :, 0], cmod[:, :, 1]

    w_in_b = w_in[0].astype(BF16)
    cos, s1, s2 = _rope_tables(n)
    sh1_w = _rowproj(mod[:, :d], w_in_b)[:b].reshape(b, 1, IN_COLS)
    qkv, rqk, rv, rg, gates = _inproj(x, sh1_w, sc1, row(norm1_g[0]), w_in_b, cos, s1, s2)
    ckv, crk, crv = _inproj_ctx(ctx, csh1, csc1, row(norm1_g[0]), w_in_b)

    o_na = _na(qkv, ckv, _na_bias_table(na_rpb[0]))
    o_ret = _ret(ret_log_decay_fwd[0], ret_log_decay_bwd[0], rqk, rv, rg, crk, crv)

    x1 = _merge(x, o_na, o_ret, gates, g1, w_proj_na[0].astype(BF16), w_proj_ret[0].astype(BF16),
                w_out[0].astype(BF16))
    return _ffn(x1, sh2, sc2, g2, row(norm2_g[0]), row(final_norm_g), w_up[0].astype(BF16),
                conv_w[0], row(conv_b[0]), w_down[0].astype(BF16))
```

```python
import math

import numpy as np
import jax
import jax.numpy as jnp
from jax import lax
from jax.experimental import pallas as pl
from jax.experimental.pallas import tpu as pltpu

F32 = jnp.float32
BF16 = jnp.bfloat16

D_MODEL = 1024
GRID_W = 64
NA_HEADS = 8
NA_HEAD_DIM = 64
NA_WIN_H = 8
NA_WIN_W = 16
RET_HEADS = 4
RET_QK_DIM = 128
RET_V_DIM = 256
RET_CHUNK = 128
D_FF = int(math.ceil(8 * D_MODEL / 3 / 128)) * 128
CONV_W = 3
ROPE_BASE = 10000.0
EPS = 1e-6

NA_W = NA_HEADS * NA_HEAD_DIM
RET_QK_W = RET_HEADS * RET_QK_DIM
RET_V_W = RET_HEADS * RET_V_DIM
OFF_Q, OFF_K, OFF_V = 0, NA_W, 2 * NA_W
OFF_RQ = 3 * NA_W
OFF_RK = OFF_RQ + RET_QK_W
OFF_RV = OFF_RK + RET_QK_W
OFF_RG = OFF_RV + RET_V_W
OFF_GA = OFF_RG + RET_V_W
OFF_GB = OFF_GA + D_MODEL
IN_COLS = OFF_GB + D_MODEL

NA_SCALE = NA_HEAD_DIM ** -0.5
RK_SCALE = RET_QK_DIM ** -0.5
MASK_VALUE = -1e30

VMEM_LIMIT_BYTES = 56 * 1024 * 1024
LANES = 128
HALO = 8

TM_PROJ = 512
TM_MERGE = 1024
TM_FFN = 512
ADA_ROWS = 16
ADA_TN = 1536
FF_CHUNK = 256
FF_SLOTS = 3
DOWN_EVERY = 5
PROJ_CHUNK = 512
NA_BLOCK = 2
RET_BLOCK = 16


def _cparams(n_axes):
    return pltpu.CompilerParams(dimension_semantics=("arbitrary",) * n_axes,
                                vmem_limit_bytes=VMEM_LIMIT_BYTES)


def _resident(shape):
    nd = len(shape)
    return pl.BlockSpec(shape, lambda *_: (0,) * nd, pipeline_mode=pl.Buffered(1))


def _mod_spec(k, row=None):
    if row is None:
        return pl.BlockSpec((1, 1, D_MODEL), lambda bi, *_: (bi, 0, k))
    return pl.BlockSpec((1, 1, D_MODEL), lambda *_: (row, 0, k))


def _rms_modulate(x, gain, shift, scale):
    y = x * lax.rsqrt(jnp.mean(x * x, axis=-1, keepdims=True) + EPS)
    return (y * gain) * (1.0 + scale) + shift


def _ada_kernel(c_ref, w_ref, b_ref, o_ref):
    s = jax.nn.silu(c_ref[...])
    o_ref[:, 0, :] = jnp.dot(s.astype(BF16), w_ref[...].astype(BF16), preferred_element_type=F32) + b_ref[...]


def _ada(cc, w_ada, b_ada):
    n_out = w_ada.shape[1]
    return pl.pallas_call(
        _ada_kernel,
        grid=(n_out // ADA_TN,),
        in_specs=[pl.BlockSpec((ADA_ROWS, D_MODEL), lambda j: (0, 0)),
                  pl.BlockSpec((D_MODEL, ADA_TN), lambda j: (0, j)),
                  pl.BlockSpec((1, ADA_TN), lambda j: (0, j))],
        out_specs=pl.BlockSpec((ADA_ROWS, 1, ADA_TN), lambda j: (0, 0, j)),
        out_shape=jax.ShapeDtypeStruct((ADA_ROWS, 1, n_out), F32),
        compiler_params=_cparams(1),
        name="ada",
    )(cc, w_ada, b_ada)


def _rowproj_kernel(s_ref, w_ref, o_ref):
    o_ref[:, 0, :] = jnp.dot(s_ref[:, 0, :].astype(BF16), w_ref[...], preferred_element_type=F32)


def _rowproj(rows, w):
    n_out = w.shape[1]
    tn = n_out // 2
    return pl.pallas_call(
        _rowproj_kernel,
        grid=(n_out // tn,),
        in_specs=[pl.BlockSpec((ADA_ROWS, 1, D_MODEL), lambda j: (0, 0, 0)),
                  pl.BlockSpec((D_MODEL, tn), lambda j: (0, j))],
        out_specs=pl.BlockSpec((ADA_ROWS, 1, tn), lambda j: (0, 0, j)),
        out_shape=jax.ShapeDtypeStruct((ADA_ROWS, 1, n_out), F32),
        compiler_params=_cparams(1),
        name="rowproj",
    )(rows, w)


def _inproj_kernel(x_ref, shw_ref, sc_ref, g_ref, w_ref, cs_ref, s1_ref, s2_ref,
                   qkv_ref, rqk_ref, rv_ref, rg_ref, gate_ref):
    x = x_ref[0]
    xa = (x * (g_ref[...] * (1.0 + sc_ref[0]))).astype(BF16)
    inv_rms = lax.rsqrt(jnp.mean(x * x, axis=-1, keepdims=True) + EPS)

    def proj(c0):
        return (jnp.dot(xa, w_ref[:, c0:c0 + PROJ_CHUNK], preferred_element_type=F32) * inv_rms
                + shw_ref[0, :, c0:c0 + PROJ_CHUNK])

    for j in range(2 * D_MODEL // PROJ_CHUNK):
        r = proj(OFF_GA + j * PROJ_CHUNK)
        gate_ref[0, :, j * PROJ_CHUNK:(j + 1) * PROJ_CHUNK] = jax.nn.sigmoid(r).astype(BF16)
    for j in range(RET_V_W // PROJ_CHUNK):
        r = proj(OFF_RG + j * PROJ_CHUNK)
        rg_ref[0, :, j * PROJ_CHUNK:(j + 1) * PROJ_CHUNK] = jax.nn.silu(r).astype(BF16)

    cs, s1, s2 = cs_ref[...], s1_ref[...], s2_ref[...]
    quarter = RET_QK_DIM // 4
    for j in range(2 * RET_QK_W // PROJ_CHUNK):
        r = proj(OFF_RQ + j * PROJ_CHUNK)
        for hh in range(PROJ_CHUNK // RET_QK_DIM):
            t = r[:, hh * RET_QK_DIM:(hh + 1) * RET_QK_DIM]
            t = (t * cs + pltpu.roll(t, RET_QK_DIM - quarter, 1) * s1
                 + pltpu.roll(t, quarter, 1) * s2)
            if j * PROJ_CHUNK >= RET_QK_W:
                t = t * RK_SCALE
            c0 = j * PROJ_CHUNK + hh * RET_QK_DIM
            rqk_ref[0, :, c0:c0 + RET_QK_DIM] = t.astype(BF16)

    for j in range(3 * NA_W // PROJ_CHUNK):
        r = proj(OFF_Q + j * PROJ_CHUNK)
        if j * PROJ_CHUNK < NA_W:
            r = r * NA_SCALE
        qkv_ref[0, :, j * PROJ_CHUNK:(j + 1) * PROJ_CHUNK] = r.astype(BF16)
    for j in range(RET_V_W // PROJ_CHUNK):
        rv_ref[0, :, j * PROJ_CHUNK:(j + 1) * PROJ_CHUNK] = proj(OFF_RV + j * PROJ_CHUNK).astype(BF16)


def _inproj(x, shift_w, mod, gain, w_in, cs, s1, s2):
    b, t, _ = x.shape
    tm = TM_PROJ
    tok = lambda w: pl.BlockSpec((1, tm, w), lambda bi, i: (bi, i, 0))
    tab = pl.BlockSpec((tm, RET_QK_DIM), lambda bi, i: (i, 0))
    out_w = (3 * NA_W, 2 * RET_QK_W, RET_V_W, RET_V_W, 2 * D_MODEL)
    return pl.pallas_call(
        _inproj_kernel,
        grid=(b, t // tm),
        in_specs=[tok(D_MODEL),
                  pl.BlockSpec((1, 1, IN_COLS), lambda bi, i: (bi, 0, 0)),
                  _mod_spec(1),
                  _resident((1, D_MODEL)),
                  _resident((D_MODEL, IN_COLS)),
                  tab, tab, tab],
        out_specs=[tok(w) for w in out_w],
        out_shape=[jax.ShapeDtypeStruct((b, t, w), BF16) for w in out_w],
        compiler_params=_cparams(2),
        name="inproj",
    )(x, shift_w, mod, gain, w_in, cs, s1, s2)


def _inproj_ctx_kernel(x_ref, sh_ref, sc_ref, g_ref, w_ref, kv_ref, rk_ref, rv_ref):
    h = _rms_modulate(x_ref[0], g_ref[...], sh_ref[0], sc_ref[0]).astype(BF16)

    def proj(c0):
        return jnp.dot(h, w_ref[:, c0:c0 + PROJ_CHUNK], preferred_element_type=F32)

    for j in range(2 * NA_W // PROJ_CHUNK):
        kv_ref[0, :, j * PROJ_CHUNK:(j + 1) * PROJ_CHUNK] = proj(OFF_K + j * PROJ_CHUNK).astype(BF16)
    for j in range(RET_QK_W // PROJ_CHUNK):
        rk_ref[0, :, j * PROJ_CHUNK:(j + 1) * PROJ_CHUNK] = (
            proj(OFF_RK + j * PROJ_CHUNK) * RK_SCALE).astype(BF16)
    for j in range(RET_V_W // PROJ_CHUNK):
        rv_ref[0, :, j * PROJ_CHUNK:(j + 1) * PROJ_CHUNK] = proj(OFF_RV + j * PROJ_CHUNK).astype(BF16)


def _inproj_ctx(ctx, mod, mod_row, gain, w_in):
    b, t, _ = ctx.shape
    tok = lambda w: pl.BlockSpec((1, t, w), lambda bi: (bi, 0, 0))
    out_w = (2 * NA_W, RET_QK_W, RET_V_W)
    return pl.pallas_call(
        _inproj_ctx_kernel,
        grid=(b,),
        in_specs=[tok(D_MODEL), _mod_spec(0, mod_row), _mod_spec(1, mod_row),
                  _resident((1, D_MODEL)), _resident((D_MODEL, IN_COLS))],
        out_specs=[tok(w) for w in out_w],
        out_shape=[jax.ShapeDtypeStruct((b, t, w), BF16) for w in out_w],
        compiler_params=_cparams(1),
        name="inproj_ctx",
    )(ctx, mod, mod, gain, w_in)


def _na_kernel(q_ref, k_ref, v_ref, kc_ref, vc_ref, bias_ref, o_ref, vext, vcext, s_scr, m_scr, p_scr, oc_scr):
    n_rows = q_ref.shape[1] // GRID_W
    n_blk = n_rows // NA_BLOCK
    win = NA_WIN_H * GRID_W
    width = s_scr.shape[2]
    pair = 2 * GRID_W
    first = lax.broadcasted_iota(jnp.int32, (GRID_W, LANES), 1) < NA_HEAD_DIM
    nt = (((1,), (1,)), ((), ()))
    kc = kc_ref[0]
    vext[:, :LANES] = v_ref[0]
    vext[:, LANES:] = jnp.ones((vext.shape[0], LANES), BF16)
    vcext[:, :LANES] = vc_ref[0]
    vcext[:, LANES:] = jnp.ones((vcext.shape[0], LANES), BF16)

    def window(r):
        rs = min(max(r - NA_WIN_H // 2, 0), n_rows - NA_WIN_H)
        return (slice(r * GRID_W, (r + 1) * GRID_W), slice(rs * GRID_W, rs * GRID_W + win),
                rs - r + (NA_WIN_H - 1))

    def score_stage(blk, slot):
        base = blk * NA_BLOCK
        stacked = []
        for k in range(NA_BLOCK):
            q = q_ref[0, window(base + k)[0], :]
            zero = jnp.zeros_like(q)
            stacked.append(jnp.concatenate([jnp.where(first, q, zero), jnp.where(first, zero, q)], axis=0))
        s_ctx = lax.dot_general(jnp.concatenate(stacked, axis=0), kc, nt, preferred_element_type=F32)
        s_scr[slot, :, win:] = s_ctx
        for k in range(NA_BLOCK):
            _, wrows, var = window(base + k)
            s_loc = lax.dot_general(stacked[k], k_ref[0, wrows, :], nt, preferred_element_type=F32)
            for hh in range(2):
                dst = slice(k * pair + hh * GRID_W, k * pair + (hh + 1) * GRID_W)
                sl = s_loc[hh * GRID_W:(hh + 1) * GRID_W] + bias_ref[hh, var]
                s_scr[slot, dst, :win] = sl
                mx = jnp.maximum(jnp.max(sl, axis=-1, keepdims=True),
                                 jnp.max(s_ctx[dst], axis=-1, keepdims=True))
                m_scr[slot, dst, :] = jnp.broadcast_to(mx, (GRID_W, LANES))

    def prob_stage(slot):
        for k in range(NA_BLOCK):
            dst = slice(k * pair, (k + 1) * pair)
            mx = m_scr[slot, dst, :]
            for j in range(width // LANES):
                cols = slice(j * LANES, (j + 1) * LANES)
                p_scr[slot, dst, cols] = jnp.exp(s_scr[slot, dst, cols] - mx).astype(BF16)

    def out_stage(blk, slot):
        oc_scr[...] = jnp.dot(p_scr[slot, :, win:], vcext[...], preferred_element_type=F32)
        for k in range(NA_BLOCK):
            rows, wrows, _ = window(blk * NA_BLOCK + k)
            dst = slice(k * pair, (k + 1) * pair)
            o = (jnp.dot(p_scr[slot, dst, :win], vext[wrows, :], preferred_element_type=F32)
                 + oc_scr[dst, :])
            num = jnp.where(first, o[:GRID_W, :LANES], o[GRID_W:, :LANES])
            den = jnp.where(first, o[:GRID_W, LANES:], o[GRID_W:, LANES:])
            o_ref[0, rows, :] = (num / den).astype(BF16)

    assert n_blk % 2 == 0 and n_blk >= 4
    score_stage(0, 0)
    score_stage(1, 1)
    prob_stage(0)

    for blk in range(0, n_blk - 2, 2):
        out_stage(blk, 0)
        prob_stage(1)
        score_stage(blk + 2, 0)
        out_stage(blk + 1, 1)
        prob_stage(0)
        score_stage(blk + 3, 1)
    out_stage(n_blk - 2, 0)
    prob_stage(1)
    out_stage(n_blk - 1, 1)


def _na(qkv, ckv, bias):
    b, n, _ = qkv.shape
    ctx = ckv.shape[1]
    pairs = NA_W // LANES
    win = NA_WIN_H * GRID_W
    stage_rows = NA_BLOCK * 2 * GRID_W
    blk = lambda t, off: pl.BlockSpec((1, t, LANES), lambda p, bi: (bi, 0, off + p))
    return pl.pallas_call(
        _na_kernel,
        grid=(pairs, b),
        in_specs=[blk(n, 0), blk(n, pairs), blk(n, 2 * pairs),
                  blk(ctx, 0), blk(ctx, pairs),
                  pl.BlockSpec((2, NA_WIN_H, GRID_W, win), lambda p, bi: (p, 0, 0, 0))],
        out_specs=pl.BlockSpec((1, n, LANES), lambda p, bi: (bi, 0, p)),
        out_shape=jax.ShapeDtypeStruct((b, n, NA_W), BF16),
        scratch_shapes=[pltpu.VMEM((n, 2 * LANES), BF16), pltpu.VMEM((ctx, 2 * LANES), BF16),
                        pltpu.VMEM((2, stage_rows, win + ctx), F32), pltpu.VMEM((2, stage_rows, LANES), F32),
                        pltpu.VMEM((2, stage_rows, win + ctx), BF16), pltpu.VMEM((stage_rows, 2 * LANES), F32)],
        compiler_params=_cparams(2),
        name="na",
    )(qkv, qkv, qkv, ckv, ckv, bias)


def _ret_kernel(lgf_ref, lgb_ref, q_ref, k_ref, v_ref, g_ref, ck_ref, cv_ref, o_ref,
                dmat, xi_f, xi_b, zeta_f, zeta_b, gl_f, gl_b, sb_scr, p_scr, o_acc):
    L = RET_CHUNK
    hd = pl.program_id(1)
    lgf = lgf_ref[hd]
    lgb = lgb_ref[hd]
    n_chunks = q_ref.shape[1] // L
    c_chunks = ck_ref.shape[1] // L

    li = lax.broadcasted_iota(jnp.int32, (L, L), 0).astype(F32)
    mi = lax.broadcasted_iota(jnp.int32, (L, L), 1).astype(F32)
    diff = li - mi
    dmat[...] = jnp.where(diff >= 0.0, jnp.exp(lgf * jnp.maximum(diff, 0.0)),
                          jnp.exp(lgb * jnp.maximum(-diff, 0.0)))
    zeta_f[...] = jnp.exp(lgf * (L - 1.0 - li))
    zeta_b[...] = jnp.exp(lgb * li)
    lv = lax.broadcasted_iota(jnp.int32, (L, RET_V_DIM), 0).astype(F32)
    xi_f[...] = jnp.exp(lgf * (lv + 1.0))
    xi_b[...] = jnp.exp(lgb * (L - lv))
    gl_f[...] = jnp.exp(jnp.full((L, RET_V_DIM), L, F32) * lgf)
    gl_b[...] = jnp.exp(jnp.full((L, RET_V_DIM), L, F32) * lgb)

    tn = (((0,), (0,)), ((), ()))
    nt = (((1,), (1,)), ((), ()))

    def advance(state, gl, zeta, k, v):
        kz = (k.astype(F32) * zeta[...]).astype(BF16)
        return gl[...] * state + lax.dot_general(kz, v, tn, preferred_element_type=F32)

    sf = jnp.zeros((RET_QK_DIM, RET_V_DIM), F32)
    sb = jnp.zeros((RET_QK_DIM, RET_V_DIM), F32)
    for c in range(c_chunks):
        sf = advance(sf, gl_f, zeta_f, ck_ref[0, c * L:(c + 1) * L, :], cv_ref[0, c * L:(c + 1) * L, :])
    for c in reversed(range(c_chunks)):
        sb = advance(sb, gl_b, zeta_b, ck_ref[0, c * L:(c + 1) * L, :], cv_ref[0, c * L:(c + 1) * L, :])

    chunk = lambda c: slice(c * L, (c + 1) * L)

    for cb in reversed(range(n_chunks)):
        sb_scr[cb] = sb.astype(BF16)
        sb = advance(sb, gl_b, zeta_b, k_ref[0, chunk(cb), :], v_ref[0, chunk(cb), :])

    def probs(c):
        rows = chunk(c)
        a = lax.dot_general(q_ref[0, rows, :], k_ref[0, rows, :], nt, preferred_element_type=F32)
        p_scr[rows, :] = (a * dmat[...]).astype(BF16)

    def outs(c, sf):
        rows = chunk(c)
        q = q_ref[0, rows, :]
        v = v_ref[0, rows, :]
        o_acc[rows, :] = (jnp.dot(p_scr[rows, :], v, preferred_element_type=F32)
                          + jnp.dot(q, sf.astype(BF16), preferred_element_type=F32) * xi_f[...]
                          + jnp.dot(q, sb_scr[c], preferred_element_type=F32) * xi_b[...])
        return advance(sf, gl_f, zeta_f, k_ref[0, rows, :], v)

    def norm(c):
        rows = chunk(c)
        o = o_acc[rows, :]
        mu = jnp.mean(o, axis=-1, keepdims=True)
        d = o - mu
        var = jnp.mean(d * d, axis=-1, keepdims=True)
        o_ref[0, rows, :] = (d * lax.rsqrt(var + EPS) * g_ref[0, rows, :].astype(F32)).astype(BF16)

    n_blk = n_chunks // RET_BLOCK
    block = lambda t: [t * RET_BLOCK + i for i in range(RET_BLOCK)]

    def outs_block(t, sf):
        for c in block(t):
            sf = outs(c, sf)
        return sf

    for c in block(0):
        probs(c)
    sf = outs_block(0, sf)
    for c in block(1):
        probs(c)

    for t in range(2, n_blk):
        for c in block(t - 2):
            norm(c)
        sf = outs_block(t - 1, sf)
        for c in block(t):
            probs(c)
    for c in block(n_blk - 2):
        norm(c)
    outs_block(n_blk - 1, sf)
    for c in block(n_blk - 1):
        norm(c)


def _ret(lgf, lgb, rqk, rv, rg, crk, crv):
    b, n, _ = rqk.shape
    ctx = crk.shape[1]
    L = RET_CHUNK
    smem = pl.BlockSpec(memory_space=pltpu.SMEM)
    qk = lambda t, off: pl.BlockSpec((1, t, RET_QK_DIM), lambda bi, h: (bi, 0, off + h))
    vv = lambda t: pl.BlockSpec((1, t, RET_V_DIM), lambda bi, h: (bi, 0, h))
    vm = lambda r, c: pltpu.VMEM((r, c), F32)
    st = pltpu.VMEM((n // L, RET_QK_DIM, RET_V_DIM), BF16)
    return pl.pallas_call(
        _ret_kernel,
        grid=(b, RET_HEADS),
        in_specs=[smem, smem, qk(n, 0), qk(n, RET_HEADS), vv(n), vv(n), qk(ctx, 0), vv(ctx)],
        out_specs=vv(n),
        out_shape=jax.ShapeDtypeStruct((b, n, RET_V_W), BF16),
        scratch_shapes=[vm(L, L), vm(L, RET_V_DIM), vm(L, RET_V_DIM), vm(L, L), vm(L, L),
                        vm(L, RET_V_DIM), vm(L, RET_V_DIM), st,
                        pltpu.VMEM((n, L), BF16), vm(n, RET_V_DIM)],
        compiler_params=_cparams(2),
        name="ret",
    )(lgf, lgb, rqk, rqk, rv, rg, crk, crv)


def _merge_kernel(x_ref, ona_ref, oret_ref, gate_ref, g1_ref, wna_ref, wret_ref, wout_ref, o_ref):
    a = jnp.dot(ona_ref[0], wna_ref[...], preferred_element_type=F32)
    r = jnp.dot(oret_ref[0], wret_ref[...], preferred_element_type=F32)
    merged = (gate_ref[0, :, :D_MODEL].astype(F32) * a
              + gate_ref[0, :, D_MODEL:].astype(F32) * r).astype(BF16)
    y = jnp.dot(merged, wout_ref[...], preferred_element_type=F32)
    o_ref[0] = x_ref[0] + g1_ref[0] * y


def _merge(x, o_na, o_ret, gates, mod, w_na, w_ret, w_out):
    b, n, _ = x.shape
    tm = TM_MERGE
    tok = lambda w: pl.BlockSpec((1, tm, w), lambda bi, i: (bi, i, 0))
    return pl.pallas_call(
        _merge_kernel,
        grid=(b, n // tm),
        in_specs=[tok(D_MODEL), tok(NA_W), tok(RET_V_W), tok(2 * D_MODEL), _mod_spec(2),
                  _resident((NA_W, D_MODEL)), _resident((RET_V_W, D_MODEL)),
                  _resident((D_MODEL, D_MODEL))],
        out_specs=tok(D_MODEL),
        out_shape=jax.ShapeDtypeStruct((b, n, D_MODEL), F32),
        compiler_params=_cparams(2),
        name="merge",
    )(x, o_na, o_ret, gates, mod, w_na, w_ret, w_out)


def _ffn_kernel(x_ref, xp_ref, xn_ref, sh_ref, sc_ref, g2_ref, gn_ref, gf_ref,
                wup_ref, cw_ref, cb_ref, wdown_ref, o_ref, h_scr, u_scr, y_scr):
    i = pl.program_id(1)
    last = pl.num_programs(1) - 1
    tm = x_ref.shape[1]
    rows = tm + 2 * HALO
    n_ch = D_FF // FF_CHUNK
    x = x_ref[0]
    xa = jnp.concatenate([xp_ref[0], x, xn_ref[0]], axis=0)
    h = _rms_modulate(xa, gn_ref[...], sh_ref[0], sc_ref[0])
    ridx = lax.broadcasted_iota(jnp.int32, (rows, 1), 0)
    valid = jnp.logical_and(jnp.logical_or(ridx >= HALO, i > 0),
                            jnp.logical_or(ridx < tm + HALO, i < last))
    h_scr[...] = jnp.where(valid, h, 0.0).astype(BF16)

    def up(j, slot):
        for g in range(2):
            c0 = g * D_FF + j * FF_CHUNK
            u_scr[slot, g] = jnp.dot(h_scr[...], wup_ref[:, c0:c0 + FF_CHUNK], preferred_element_type=F32)

    def act(j, slot):
        def conv(g):
            c0 = g * D_FF + j * FF_CHUNK
            u = u_scr[slot, g]
            w = cw_ref[:, c0:c0 + FF_CHUNK]
            y = (pltpu.roll(u, 1, 0) * w[0:1] + u * w[1:2] + pltpu.roll(u, rows - 1, 0) * w[2:3]
                 + cb_ref[:, c0:c0 + FF_CHUNK])
            return y[HALO:HALO + tm]

        y_scr[:, j * FF_CHUNK:(j + 1) * FF_CHUNK] = (jax.nn.silu(conv(0)) * conv(1)).astype(BF16)

    def down(j0, j1):
        c0, c1 = j0 * FF_CHUNK, j1 * FF_CHUNK
        return jnp.dot(y_scr[:, c0:c1], wdown_ref[c0:c1, :], preferred_element_type=F32)

    slots = u_scr.shape[0]
    ahead = slots - 1
    for j in range(ahead):
        up(j, j)
    acc = None
    for j in range(n_ch):
        act(j, j % slots)
        if j + ahead < n_ch:
            up(j + ahead, (j + ahead) % slots)
        if j % DOWN_EVERY == 0 and j > 0:
            part = down(j - DOWN_EVERY, j)
            acc = part if acc is None else acc + part
    j0 = (n_ch - 1) // DOWN_EVERY * DOWN_EVERY
    acc = acc + down(j0, n_ch)
    z = x + g2_ref[0] * acc
    o_ref[0] = z * lax.rsqrt(jnp.mean(z * z, axis=-1, keepdims=True) + EPS) * gf_ref[...]


def _ffn(x, mod, gain2, gain_f, w_up, conv_w, conv_b, w_down):
    b, n, _ = x.shape
    tm = TM_FFN
    per = tm // HALO
    n_halo = n // HALO
    rows = tm + 2 * HALO
    tok = pl.BlockSpec((1, tm, D_MODEL), lambda bi, i: (bi, i, 0))
    prev = pl.BlockSpec((1, HALO, D_MODEL), lambda bi, i: (bi, jnp.maximum(i * per - 1, 0), 0))
    nxt = pl.BlockSpec((1, HALO, D_MODEL), lambda bi, i: (bi, jnp.minimum((i + 1) * per, n_halo - 1), 0))
    return pl.pallas_call(
        _ffn_kernel,
        grid=(b, n // tm),
        in_specs=[tok, prev, nxt, _mod_spec(3), _mod_spec(4), _mod_spec(5),
                  _resident((1, D_MODEL)), _resident((1, D_MODEL)),
                  _resident(w_up.shape), _resident(conv_w.shape),
                  _resident(conv_b.shape), _resident((D_FF, D_MODEL))],
        out_specs=tok,
        out_shape=jax.ShapeDtypeStruct((b, n, D_MODEL), F32),
        scratch_shapes=[pltpu.VMEM((rows, D_MODEL), BF16),
                        pltpu.VMEM((FF_SLOTS, 2, rows, FF_CHUNK), F32),
                        pltpu.VMEM((tm, D_FF), BF16)],
        compiler_params=_cparams(2),
        name="ffn",
    )(x, x, x, mod, mod, mod, gain2, gain_f, w_up, conv_w, conv_b, w_down)


def _rope_tables(n):
    nf = RET_QK_DIM // 4
    inv = (ROPE_BASE ** (-np.arange(nf, dtype=np.float32) / nf)).astype(np.float32)
    t = np.arange(n)
    ang_r = (t // GRID_W).astype(np.float32)[:, None] * inv[None, :]
    ang_c = (t % GRID_W).astype(np.float32)[:, None] * inv[None, :]
    cos = np.concatenate([np.cos(ang_r)] * 2 + [np.cos(ang_c)] * 2, axis=-1)
    zero = np.zeros((n, nf), np.float32)
    s1 = np.concatenate([-np.sin(ang_r), zero, -np.sin(ang_c), zero], axis=-1)
    s2 = np.concatenate([zero, np.sin(ang_r), zero, np.sin(ang_c)], axis=-1)
    return tuple(jnp.asarray(a.astype(np.float32)) for a in (cos, s1, s2))


def _na_bias_table(rpb):
    w, kh, kw = GRID_W, NA_WIN_H, NA_WIN_W
    cols = np.arange(w)
    c_start = np.clip(cols - kw // 2, 0, w - kw)
    kcol = np.arange(w)
    valid = (kcol[None, :] >= c_start[:, None]) & (kcol[None, :] < c_start[:, None] + kw)
    dc = kcol[None, :] - cols[:, None] + (kw - 1)
    onehot = ((dc[:, :, None] == np.arange(2 * kw - 1)) & valid[:, :, None]).astype(np.float32)
    base = jnp.einsum('hrd,ckd->hcrk', rpb, onehot, precision=lax.Precision.HIGHEST)
    base = base + np.where(valid, 0.0, MASK_VALUE).astype(np.float32)[:, None, :]
    return jnp.stack([base[:, :, v:v + kh].reshape(rpb.shape[0], w, kh * w) for v in range(kh)], axis=1)


def kernel(x, c, ctx, c_ctx, w_ada, b_ada, norm1_g, w_in, na_rpb, ret_log_decay_fwd, ret_log_decay_bwd,
           w_proj_na, w_proj_ret, w_out, norm2_g, w_up, conv_w, conv_b, w_down, final_norm_g):
    b, n, d = x.shape
    assert w_ada.shape[0] == 1, "single layer: the per-layer parameter arrays are used as (1, N) rows"

    cc = jnp.concatenate([c, c_ctx[None], jnp.zeros((ADA_ROWS - b - 1, d), F32)], axis=0)
    mod = _ada(cc, w_ada[0], b_ada)

    w_in_b = w_in[0].astype(BF16)
    cos, s1, s2 = _rope_tables(n)
    sh1_w = _rowproj(mod, w_in_b)
    qkv, rqk, rv, rg, gates = _inproj(x, sh1_w, mod, norm1_g, w_in_b, cos, s1, s2)
    ckv, crk, crv = _inproj_ctx(ctx, mod, b, norm1_g, w_in_b)

    o_na = _na(qkv, ckv, _na_bias_table(na_rpb[0]))
    o_ret = _ret(ret_log_decay_fwd[0], ret_log_decay_bwd[0], rqk, rv, rg, crk, crv)

    x1 = _merge(x, o_na, o_ret, gates, mod, w_proj_na[0].astype(BF16), w_proj_ret[0].astype(BF16),
                w_out[0].astype(BF16))
    return _ffn(x1, mod, norm2_g, final_norm_g.reshape(1, d), w_up[0].astype(BF16),
                conv_w[0], conv_b, w_down[0].astype(BF16))
```

```python
import math

import numpy as np
import jax
import jax.numpy as jnp
from jax import lax
from jax.experimental import pallas as pl
from jax.experimental.pallas import tpu as pltpu

F32 = jnp.float32
BF16 = jnp.bfloat16

D_MODEL = 1024
GRID_W = 64
NA_HEADS = 8
NA_HEAD_DIM = 64
NA_WIN_H = 8
NA_WIN_W = 16
RET_HEADS = 4
RET_QK_DIM = 128
RET_V_DIM = 256
RET_CHUNK = 128
D_FF = int(math.ceil(8 * D_MODEL / 3 / 128)) * 128
CONV_W = 3
ROPE_BASE = 10000.0
EPS = 1e-6

NA_W = NA_HEADS * NA_HEAD_DIM
RET_QK_W = RET_HEADS * RET_QK_DIM
RET_V_W = RET_HEADS * RET_V_DIM
OFF_Q, OFF_K, OFF_V = 0, NA_W, 2 * NA_W
OFF_RQ = 3 * NA_W
OFF_RK = OFF_RQ + RET_QK_W
OFF_RV = OFF_RK + RET_QK_W
OFF_RG = OFF_RV + RET_V_W
OFF_GA = OFF_RG + RET_V_W
OFF_GB = OFF_GA + D_MODEL
IN_COLS = OFF_GB + D_MODEL

NA_SCALE = NA_HEAD_DIM ** -0.5
RK_SCALE = RET_QK_DIM ** -0.5
MASK_VALUE = -1e30

VMEM_LIMIT_BYTES = 56 * 1024 * 1024
LANES = 128
HALO = 8

TM_PROJ = 512
TM_MERGE = 1024
TM_FFN = 512
ADA_ROWS = 16
ADA_TN = 1536
FF_CHUNK = 256
FF_SLOTS = 3
DOWN_EVERY = 5
PROJ_CHUNK = 512
NA_BLOCK = 2
RET_BLOCK = 16


def _cparams(n_axes):
    return pltpu.CompilerParams(dimension_semantics=("arbitrary",) * n_axes,
                                vmem_limit_bytes=VMEM_LIMIT_BYTES)


def _resident(shape):
    nd = len(shape)
    return pl.BlockSpec(shape, lambda *_: (0,) * nd, pipeline_mode=pl.Buffered(1))


def _mod_spec(k, row=None):
    if row is None:
        return pl.BlockSpec((1, 1, D_MODEL), lambda bi, *_: (bi, 0, k))
    return pl.BlockSpec((1, 1, D_MODEL), lambda *_: (row, 0, k))


def _rms_modulate(x, gain, shift, scale):
    y = x * lax.rsqrt(jnp.mean(x * x, axis=-1, keepdims=True) + EPS)
    return (y * gain) * (1.0 + scale) + shift


def _ada_kernel(c_ref, w_ref, b_ref, o_ref):
    s = jax.nn.silu(c_ref[...])
    o_ref[:, 0, :] = jnp.dot(s.astype(BF16), w_ref[...].astype(BF16), preferred_element_type=F32) + b_ref[...]


def _ada(cc, w_ada, b_ada):
    n_out = w_ada.shape[1]
    return pl.pallas_call(
        _ada_kernel,
        grid=(n_out // ADA_TN,),
        in_specs=[pl.BlockSpec((ADA_ROWS, D_MODEL), lambda j: (0, 0)),
                  pl.BlockSpec((D_MODEL, ADA_TN), lambda j: (0, j)),
                  pl.BlockSpec((1, ADA_TN), lambda j: (0, j))],
        out_specs=pl.BlockSpec((ADA_ROWS, 1, ADA_TN), lambda j: (0, 0, j)),
        out_shape=jax.ShapeDtypeStruct((ADA_ROWS, 1, n_out), F32),
        compiler_params=_cparams(1),
        name="ada",
    )(cc, w_ada, b_ada)


def _rowproj_kernel(s_ref, w_ref, o_ref):
    o_ref[:, 0, :] = jnp.dot(s_ref[:, 0, :].astype(BF16), w_ref[...], preferred_element_type=F32)


def _rowproj(rows, w):
    n_out = w.shape[1]
    tn = n_out // 2
    return pl.pallas_call(
        _rowproj_kernel,
        grid=(n_out // tn,),
        in_specs=[pl.BlockSpec((ADA_ROWS, 1, D_MODEL), lambda j: (0, 0, 0)),
                  pl.BlockSpec((D_MODEL, tn), lambda j: (0, j))],
        out_specs=pl.BlockSpec((ADA_ROWS, 1, tn), lambda j: (0, 0, j)),
        out_shape=jax.ShapeDtypeStruct((ADA_ROWS, 1, n_out), F32),
        compiler_params=_cparams(1),
        name="rowproj",
    )(rows, w)


def _inproj_kernel(x_ref, shw_ref, sc_ref, g_ref, w_ref, cs_ref, s1_ref, s2_ref,
                   qkv_ref, rqk_ref, rv_ref, rg_ref, gate_ref):
    x = x_ref[0]
    xa = (x * (g_ref[...] * (1.0 + sc_ref[0]))).astype(BF16)
    inv_rms = lax.rsqrt(jnp.mean(x * x, axis=-1, keepdims=True) + EPS)

    def proj(c0):
        return (jnp.dot(xa, w_ref[:, c0:c0 + PROJ_CHUNK], preferred_element_type=F32) * inv_rms
                + shw_ref[0, :, c0:c0 + PROJ_CHUNK])

    for j in range(2 * D_MODEL // PROJ_CHUNK):
        r = proj(OFF_GA + j * PROJ_CHUNK)
        gate_ref[0, :, j * PROJ_CHUNK:(j + 1) * PROJ_CHUNK] = jax.nn.sigmoid(r).astype(BF16)
    for j in range(RET_V_W // PROJ_CHUNK):
        r = proj(OFF_RG + j * PROJ_CHUNK)
        rg_ref[0, :, j * PROJ_CHUNK:(j + 1) * PROJ_CHUNK] = jax.nn.silu(r).astype(BF16)

    cs, s1, s2 = cs_ref[...], s1_ref[...], s2_ref[...]
    quarter = RET_QK_DIM // 4
    for j in range(2 * RET_QK_W // PROJ_CHUNK):
        r = proj(OFF_RQ + j * PROJ_CHUNK)
        for hh in range(PROJ_CHUNK // RET_QK_DIM):
            t = r[:, hh * RET_QK_DIM:(hh + 1) * RET_QK_DIM]
            t = (t * cs + pltpu.roll(t, RET_QK_DIM - quarter, 1) * s1
                 + pltpu.roll(t, quarter, 1) * s2)
            if j * PROJ_CHUNK >= RET_QK_W:
                t = t * RK_SCALE
            c0 = j * PROJ_CHUNK + hh * RET_QK_DIM
            rqk_ref[0, :, c0:c0 + RET_QK_DIM] = t.astype(BF16)

    for j in range(3 * NA_W // PROJ_CHUNK):
        r = proj(OFF_Q + j * PROJ_CHUNK)
        if j * PROJ_CHUNK < NA_W:
            r = r * NA_SCALE
        qkv_ref[0, :, j * PROJ_CHUNK:(j + 1) * PROJ_CHUNK] = r.astype(BF16)
    for j in range(RET_V_W // PROJ_CHUNK):
        rv_ref[0, :, j * PROJ_CHUNK:(j + 1) * PROJ_CHUNK] = proj(OFF_RV + j * PROJ_CHUNK).astype(BF16)


def _inproj(x, shift_w, mod, gain, w_in, cs, s1, s2):
    b, t, _ = x.shape
    tm = TM_PROJ
    tok = lambda w: pl.BlockSpec((1, tm, w), lambda bi, i: (bi, i, 0))
    tab = pl.BlockSpec((tm, RET_QK_DIM), lambda bi, i: (i, 0))
    out_w = (3 * NA_W, 2 * RET_QK_W, RET_V_W, RET_V_W, 2 * D_MODEL)
    return pl.pallas_call(
        _inproj_kernel,
        grid=(b, t // tm),
        in_specs=[tok(D_MODEL),
                  pl.BlockSpec((1, 1, IN_COLS), lambda bi, i: (bi, 0, 0)),
                  _mod_spec(1),
                  _resident((1, D_MODEL)),
                  _resident((D_MODEL, IN_COLS)),
                  tab, tab, tab],
        out_specs=[tok(w) for w in out_w],
        out_shape=[jax.ShapeDtypeStruct((b, t, w), BF16) for w in out_w],
        compiler_params=_cparams(2),
        name="inproj",
    )(x, shift_w, mod, gain, w_in, cs, s1, s2)


def _inproj_ctx_kernel(x_ref, sh_ref, sc_ref, g_ref, w_ref, kv_ref, rk_ref, rv_ref):
    h = _rms_modulate(x_ref[0], g_ref[...], sh_ref[0], sc_ref[0]).astype(BF16)

    def proj(c0):
        return jnp.dot(h, w_ref[:, c0:c0 + PROJ_CHUNK], preferred_element_type=F32)

    for j in range(2 * NA_W // PROJ_CHUNK):
        kv_ref[0, :, j * PROJ_CHUNK:(j + 1) * PROJ_CHUNK] = proj(OFF_K + j * PROJ_CHUNK).astype(BF16)
    for j in range(RET_QK_W // PROJ_CHUNK):
        rk_ref[0, :, j * PROJ_CHUNK:(j + 1) * PROJ_CHUNK] = (
            proj(OFF_RK + j * PROJ_CHUNK) * RK_SCALE).astype(BF16)
    for j in range(RET_V_W // PROJ_CHUNK):
        rv_ref[0, :, j * PROJ_CHUNK:(j + 1) * PROJ_CHUNK] = proj(OFF_RV + j * PROJ_CHUNK).astype(BF16)


def _inproj_ctx(ctx, mod, mod_row, gain, w_in):
    b, t, _ = ctx.shape
    tok = lambda w: pl.BlockSpec((1, t, w), lambda bi: (bi, 0, 0))
    out_w = (2 * NA_W, RET_QK_W, RET_V_W)
    return pl.pallas_call(
        _inproj_ctx_kernel,
        grid=(b,),
        in_specs=[tok(D_MODEL), _mod_spec(0, mod_row), _mod_spec(1, mod_row),
                  _resident((1, D_MODEL)), _resident((D_MODEL, IN_COLS))],
        out_specs=[tok(w) for w in out_w],
        out_shape=[jax.ShapeDtypeStruct((b, t, w), BF16) for w in out_w],
        compiler_params=_cparams(1),
        name="inproj_ctx",
    )(ctx, mod, mod, gain, w_in)


def _na_kernel(q_ref, k_ref, v_ref, kc_ref, vc_ref, bias_ref, o_ref, vext, vcext, s_scr, m_scr, p_scr, oc_scr):
    n_rows = q_ref.shape[1] // GRID_W
    n_blk = n_rows // NA_BLOCK
    win = NA_WIN_H * GRID_W
    width = s_scr.shape[2]
    pair = 2 * GRID_W
    first = lax.broadcasted_iota(jnp.int32, (GRID_W, LANES), 1) < NA_HEAD_DIM
    nt = (((1,), (1,)), ((), ()))
    kc = kc_ref[0]
    vext[:, :LANES] = v_ref[0]
    vext[:, LANES:] = jnp.ones((vext.shape[0], LANES), BF16)
    vcext[:, :LANES] = vc_ref[0]
    vcext[:, LANES:] = jnp.ones((vcext.shape[0], LANES), BF16)

    def window(r):
        rs = min(max(r - NA_WIN_H // 2, 0), n_rows - NA_WIN_H)
        return (slice(r * GRID_W, (r + 1) * GRID_W), slice(rs * GRID_W, rs * GRID_W + win),
                rs - r + (NA_WIN_H - 1))

    def score_stage(blk, slot):
        base = blk * NA_BLOCK
        stacked = []
        for k in range(NA_BLOCK):
            q = q_ref[0, window(base + k)[0], :]
            zero = jnp.zeros_like(q)
            stacked.append(jnp.concatenate([jnp.where(first, q, zero), jnp.where(first, zero, q)], axis=0))
        s_ctx = lax.dot_general(jnp.concatenate(stacked, axis=0), kc, nt, preferred_element_type=F32)
        s_scr[slot, :, win:] = s_ctx
        for k in range(NA_BLOCK):
            _, wrows, var = window(base + k)
            s_loc = lax.dot_general(stacked[k], k_ref[0, wrows, :], nt, preferred_element_type=F32)
            for hh in range(2):
                dst = slice(k * pair + hh * GRID_W, k * pair + (hh + 1) * GRID_W)
                sl = s_loc[hh * GRID_W:(hh + 1) * GRID_W] + bias_ref[hh, var]
                s_scr[slot, dst, :win] = sl
                mx = jnp.maximum(jnp.max(sl, axis=-1, keepdims=True),
                                 jnp.max(s_ctx[dst], axis=-1, keepdims=True))
                m_scr[slot, dst, :] = jnp.broadcast_to(mx, (GRID_W, LANES))

    def prob_stage(slot):
        for k in range(NA_BLOCK):
            dst = slice(k * pair, (k + 1) * pair)
            mx = m_scr[slot, dst, :]
            for j in range(width // LANES):
                cols = slice(j * LANES, (j + 1) * LANES)
                p_scr[slot, dst, cols] = jnp.exp(s_scr[slot, dst, cols] - mx).astype(BF16)

    def out_stage(blk, slot):
        oc_scr[...] = jnp.dot(p_scr[slot, :, win:], vcext[...], preferred_element_type=F32)
        for k in range(NA_BLOCK):
            rows, wrows, _ = window(blk * NA_BLOCK + k)
            dst = slice(k * pair, (k + 1) * pair)
            o = (jnp.dot(p_scr[slot, dst, :win], vext[wrows, :], preferred_element_type=F32)
                 + oc_scr[dst, :])
            num = jnp.where(first, o[:GRID_W, :LANES], o[GRID_W:, :LANES])
            den = jnp.where(first, o[:GRID_W, LANES:], o[GRID_W:, LANES:])
            o_ref[0, rows, :] = (num / den).astype(BF16)

    assert n_blk % 2 == 0 and n_blk >= 4
    score_stage(0, 0)
    score_stage(1, 1)
    prob_stage(0)

    for blk in range(0, n_blk - 2, 2):
        out_stage(blk, 0)
        prob_stage(1)
        score_stage(blk + 2, 0)
        out_stage(blk + 1, 1)
        prob_stage(0)
        score_stage(blk + 3, 1)
    out_stage(n_blk - 2, 0)
    prob_stage(1)
    out_stage(n_blk - 1, 1)


def _na(qkv, ckv, bias):
    b, n, _ = qkv.shape
    ctx = ckv.shape[1]
    pairs = NA_W // LANES
    win = NA_WIN_H * GRID_W
    stage_rows = NA_BLOCK * 2 * GRID_W
    blk = lambda t, off: pl.BlockSpec((1, t, LANES), lambda p, bi: (bi, 0, off + p))
    return pl.pallas_call(
        _na_kernel,
        grid=(pairs, b),
        in_specs=[blk(n, 0), blk(n, pairs), blk(n, 2 * pairs),
                  blk(ctx, 0), blk(ctx, pairs),
                  pl.BlockSpec((2, NA_WIN_H, GRID_W, win), lambda p, bi: (p, 0, 0, 0))],
        out_specs=pl.BlockSpec((1, n, LANES), lambda p, bi: (bi, 0, p)),
        out_shape=jax.ShapeDtypeStruct((b, n, NA_W), BF16),
        scratch_shapes=[pltpu.VMEM((n, 2 * LANES), BF16), pltpu.VMEM((ctx, 2 * LANES), BF16),
                        pltpu.VMEM((2, stage_rows, win + ctx), F32), pltpu.VMEM((2, stage_rows, LANES), F32),
                        pltpu.VMEM((2, stage_rows, win + ctx), BF16), pltpu.VMEM((stage_rows, 2 * LANES), F32)],
        compiler_params=_cparams(2),
        name="na",
    )(qkv, qkv, qkv, ckv, ckv, bias)


def _ret_kernel(lgf_ref, lgb_ref, q_ref, k_ref, v_ref, g_ref, ck_ref, cv_ref, o_ref,
                dmat, xi_f, xi_b, zeta_f, zeta_b, gl_f, gl_b, sb_scr, p_scr, o_acc):
    L = RET_CHUNK
    hd = pl.program_id(1)
    lgf = lgf_ref[hd]
    lgb = lgb_ref[hd]
    n_chunks = q_ref.shape[1] // L
    c_chunks = ck_ref.shape[1] // L

    li = lax.broadcasted_iota(jnp.int32, (L, L), 0).astype(F32)
    mi = lax.broadcasted_iota(jnp.int32, (L, L), 1).astype(F32)
    diff = li - mi
    dmat[...] = jnp.where(diff >= 0.0, jnp.exp(lgf * jnp.maximum(diff, 0.0)),
                          jnp.exp(lgb * jnp.maximum(-diff, 0.0)))
    zeta_f[...] = jnp.exp(lgf * (L - 1.0 - li))
    zeta_b[...] = jnp.exp(lgb * li)
    lv = lax.broadcasted_iota(jnp.int32, (L, RET_V_DIM), 0).astype(F32)
    xi_f[...] = jnp.exp(lgf * (lv + 1.0))
    xi_b[...] = jnp.exp(lgb * (L - lv))
    gl_f[...] = jnp.exp(jnp.full((L, RET_V_DIM), L, F32) * lgf)
    gl_b[...] = jnp.exp(jnp.full((L, RET_V_DIM), L, F32) * lgb)

    tn = (((0,), (0,)), ((), ()))
    nt = (((1,), (1,)), ((), ()))

    def advance(state, gl, zeta, k, v):
        kz = (k.astype(F32) * zeta[...]).astype(BF16)
        return gl[...] * state + lax.dot_general(kz, v, tn, preferred_element_type=F32)

    sf = jnp.zeros((RET_QK_DIM, RET_V_DIM), F32)
    sb = jnp.zeros((RET_QK_DIM, RET_V_DIM), F32)
    for c in range(c_chunks):
        sf = advance(sf, gl_f, zeta_f, ck_ref[0, c * L:(c + 1) * L, :], cv_ref[0, c * L:(c + 1) * L, :])
    for c in reversed(range(c_chunks)):
        sb = advance(sb, gl_b, zeta_b, ck_ref[0, c * L:(c + 1) * L, :], cv_ref[0, c * L:(c + 1) * L, :])

    chunk = lambda c: slice(c * L, (c + 1) * L)

    for cb in reversed(range(n_chunks)):
        sb_scr[cb] = sb.astype(BF16)
        sb = advance(sb, gl_b, zeta_b, k_ref[0, chunk(cb), :], v_ref[0, chunk(cb), :])

    def probs(c):
        rows = chunk(c)
        a = lax.dot_general(q_ref[0, rows, :], k_ref[0, rows, :], nt, preferred_element_type=F32)
        p_scr[rows, :] = (a * dmat[...]).astype(BF16)

    def outs(c, sf):
        rows = chunk(c)
        q = q_ref[0, rows, :]
        v = v_ref[0, rows, :]
        o_acc[rows, :] = (jnp.dot(p_scr[rows, :], v, preferred_element_type=F32)
                          + jnp.dot(q, sf.astype(BF16), preferred_element_type=F32) * xi_f[...]
                          + jnp.dot(q, sb_scr[c], preferred_element_type=F32) * xi_b[...])
        return advance(sf, gl_f, zeta_f, k_ref[0, rows, :], v)

    def norm(c):
        rows = chunk(c)
        o = o_acc[rows, :]
        mu = jnp.mean(o, axis=-1, keepdims=True)
        d = o - mu
        var = jnp.mean(d * d, axis=-1, keepdims=True)
        o_ref[0, rows, :] = (d * lax.rsqrt(var + EPS) * g_ref[0, rows, :].astype(F32)).astype(BF16)

    n_blk = n_chunks // RET_BLOCK
    block = lambda t: [t * RET_BLOCK + i for i in range(RET_BLOCK)]

    def outs_block(t, sf):
        for c in block(t):
            sf = outs(c, sf)
        return sf

    for c in block(0):
        probs(c)
    sf = outs_block(0, sf)
    for c in block(1):
        probs(c)

    for t in range(2, n_blk):
        for c in block(t - 2):
            norm(c)
        sf = outs_block(t - 1, sf)
        for c in block(t):
            probs(c)
    for c in block(n_blk - 2):
        norm(c)
    outs_block(n_blk - 1, sf)
    for c in block(n_blk - 1):
        norm(c)


def _ret(lgf, lgb, rqk, rv, rg, crk, crv):
    b, n, _ = rqk.shape
    ctx = crk.shape[1]
    L = RET_CHUNK
    smem = pl.BlockSpec(memory_space=pltpu.SMEM)
    qk = lambda t, off: pl.BlockSpec((1, t, RET_QK_DIM), lambda bi, h: (bi, 0, off + h))
    vv = lambda t: pl.BlockSpec((1, t, RET_V_DIM), lambda bi, h: (bi, 0, h))
    vm = lambda r, c: pltpu.VMEM((r, c), F32)
    st = pltpu.VMEM((n // L, RET_QK_DIM, RET_V_DIM), BF16)
    return pl.pallas_call(
        _ret_kernel,
        grid=(b, RET_HEADS),
        in_specs=[smem, smem, qk(n, 0), qk(n, RET_HEADS), vv(n), vv(n), qk(ctx, 0), vv(ctx)],
        out_specs=vv(n),
        out_shape=jax.ShapeDtypeStruct((b, n, RET_V_W), BF16),
        scratch_shapes=[vm(L, L), vm(L, RET_V_DIM), vm(L, RET_V_DIM), vm(L, L), vm(L, L),
                        vm(L, RET_V_DIM), vm(L, RET_V_DIM), st,
                        pltpu.VMEM((n, L), BF16), vm(n, RET_V_DIM)],
        compiler_params=_cparams(2),
        name="ret",
    )(lgf, lgb, rqk, rqk, rv, rg, crk, crv)


def _merge_kernel(x_ref, ona_ref, oret_ref, gate_ref, g1_ref, wna_ref, wret_ref, wout_ref, o_ref):
    a = jnp.dot(ona_ref[0], wna_ref[...], preferred_element_type=F32)
    r = jnp.dot(oret_ref[0], wret_ref[...], preferred_element_type=F32)
    merged = (gate_ref[0, :, :D_MODEL].astype(F32) * a
              + gate_ref[0, :, D_MODEL:].astype(F32) * r).astype(BF16)
    y = jnp.dot(merged, wout_ref[...], preferred_element_type=F32)
    o_ref[0] = x_ref[0] + g1_ref[0] * y


def _merge(x, o_na, o_ret, gates, mod, w_na, w_ret, w_out):
    b, n, _ = x.shape
    tm = TM_MERGE
    tok = lambda w: pl.BlockSpec((1, tm, w), lambda bi, i: (bi, i, 0))
    return pl.pallas_call(
        _merge_kernel,
        grid=(b, n // tm),
        in_specs=[tok(D_MODEL), tok(NA_W), tok(RET_V_W), tok(2 * D_MODEL), _mod_spec(2),
                  _resident((NA_W, D_MODEL)), _resident((RET_V_W, D_MODEL)),
                  _resident((D_MODEL, D_MODEL))],
        out_specs=tok(D_MODEL),
        out_shape=jax.ShapeDtypeStruct((b, n, D_MODEL), F32),
        compiler_params=_cparams(2),
        name="merge",
    )(x, o_na, o_ret, gates, mod, w_na, w_ret, w_out)


def _ffn_kernel(x_ref, xp_ref, xn_ref, sh_ref, sc_ref, g2_ref, gn_ref, gf_ref,
                wup_ref, cw_ref, cb_ref, wdown_ref, o_ref, h_scr, u_scr, y_scr):
    i = pl.program_id(1)
    last = pl.num_programs(1) - 1
    tm = x_ref.shape[1]
    rows = tm + 2 * HALO
    n_ch = D_FF // FF_CHUNK
    x = x_ref[0]
    xa = jnp.concatenate([xp_ref[0], x, xn_ref[0]], axis=0)
    h = _rms_modulate(xa, gn_ref[...], sh_ref[0], sc_ref[0])
    ridx = lax.broadcasted_iota(jnp.int32, (rows, 1), 0)
    valid = jnp.logical_and(jnp.logical_or(ridx >= HALO, i > 0),
                            jnp.logical_or(ridx < tm + HALO, i < last))
    h_scr[...] = jnp.where(valid, h, 0.0).astype(BF16)

    def up(j, slot):
        for g in range(2):
            c0 = g * D_FF + j * FF_CHUNK
            u_scr[slot, g] = jnp.dot(h_scr[...], wup_ref[:, c0:c0 + FF_CHUNK], preferred_element_type=F32)

    def act(j, slot):
        def conv(g):
            c0 = g * D_FF + j * FF_CHUNK
            u = u_scr[slot, g]
            w = cw_ref[:, c0:c0 + FF_CHUNK]
            y = (pltpu.roll(u, 1, 0) * w[0:1] + u * w[1:2] + pltpu.roll(u, rows - 1, 0) * w[2:3]
                 + cb_ref[:, c0:c0 + FF_CHUNK])
            return y[HALO:HALO + tm]

        y_scr[:, j * FF_CHUNK:(j + 1) * FF_CHUNK] = (jax.nn.silu(conv(0)) * conv(1)).astype(BF16)

    def down(j0, j1):
        c0, c1 = j0 * FF_CHUNK, j1 * FF_CHUNK
        return jnp.dot(y_scr[:, c0:c1], wdown_ref[c0:c1, :], preferred_element_type=F32)

    slots = u_scr.shape[0]
    ahead = slots - 1
    for j in range(ahead):
        up(j, j)
    acc = None
    for j in range(n_ch):
        act(j, j % slots)
        if j + ahead < n_ch:
            up(j + ahead, (j + ahead) % slots)
        if j % DOWN_EVERY == 0 and j > 0:
            part = down(j - DOWN_EVERY, j)
            acc = part if acc is None else acc + part
    j0 = (n_ch - 1) // DOWN_EVERY * DOWN_EVERY
    acc = acc + down(j0, n_ch)
    z = x + g2_ref[0] * acc
    o_ref[0] = z * lax.rsqrt(jnp.mean(z * z, axis=-1, keepdims=True) + EPS) * gf_ref[...]


def _ffn(x, mod, gain2, gain_f, w_up, conv_w, conv_b, w_down):
    b, n, _ = x.shape
    tm = TM_FFN
    per = tm // HALO
    n_halo = n // HALO
    rows = tm + 2 * HALO
    tok = pl.BlockSpec((1, tm, D_MODEL), lambda bi, i: (bi, i, 0))
    prev = pl.BlockSpec((1, HALO, D_MODEL), lambda bi, i: (bi, jnp.maximum(i * per - 1, 0), 0))
    nxt = pl.BlockSpec((1, HALO, D_MODEL), lambda bi, i: (bi, jnp.minimum((i + 1) * per, n_halo - 1), 0))
    return pl.pallas_call(
        _ffn_kernel,
        grid=(b, n // tm),
        in_specs=[tok, prev, nxt, _mod_spec(3), _mod_spec(4), _mod_spec(5),
                  _resident((1, D_MODEL)), _resident((1, D_MODEL)),
                  _resident(w_up.shape), _resident(conv_w.shape),
                  _resident(conv_b.shape), _resident((D_FF, D_MODEL))],
        out_specs=tok,
        out_shape=jax.ShapeDtypeStruct((b, n, D_MODEL), F32),
        scratch_shapes=[pltpu.VMEM((rows, D_MODEL), BF16),
                        pltpu.VMEM((FF_SLOTS, 2, rows, FF_CHUNK), F32),
                        pltpu.VMEM((tm, D_FF), BF16)],
        compiler_params=_cparams(2),
        name="ffn",
    )(x, x, x, mod, mod, mod, gain2, gain_f, w_up, conv_w, conv_b, w_down)


def _rope_tables(n):
    nf = RET_QK_DIM // 4
    inv = (ROPE_BASE ** (-np.arange(nf, dtype=np.float32) / nf)).astype(np.float32)
    t = np.arange(n)
    ang_r = (t // GRID_W).astype(np.float32)[:, None] * inv[None, :]
    ang_c = (t % GRID_W).astype(np.float32)[:, None] * inv[None, :]
    cos = np.concatenate([np.cos(ang_r)] * 2 + [np.cos(ang_c)] * 2, axis=-1)
    zero = np.zeros((n, nf), np.float32)
    s1 = np.concatenate([-np.sin(ang_r), zero, -np.sin(ang_c), zero], axis=-1)
    s2 = np.concatenate([zero, np.sin(ang_r), zero, np.sin(ang_c)], axis=-1)
    return tuple(jnp.asarray(a.astype(np.float32)) for a in (cos, s1, s2))


def _na_bias_table(rpb):
    w, kh, kw = GRID_W, NA_WIN_H, NA_WIN_W
    cols = np.arange(w)
    c_start = np.clip(cols - kw // 2, 0, w - kw)
    kcol = np.arange(w)
    valid = (kcol[None, :] >= c_start[:, None]) & (kcol[None, :] < c_start[:, None] + kw)
    dc = kcol[None, :] - cols[:, None] + (kw - 1)
    onehot = ((dc[:, :, None] == np.arange(2 * kw - 1)) & valid[:, :, None]).astype(np.float32)
    base = jnp.einsum('hrd,ckd->hcrk', rpb, onehot, precision=lax.Precision.HIGHEST)
    base = base + np.where(valid, 0.0, MASK_VALUE).astype(np.float32)[:, None, :]
    flat = base.reshape(rpb.shape[0], w, (2 * kh - 1) * w)
    return jnp.stack([flat[:, :, v * w:(v + kh) * w] for v in range(kh)], axis=1)


def kernel(x, c, ctx, c_ctx, w_ada, b_ada, norm1_g, w_in, na_rpb, ret_log_decay_fwd, ret_log_decay_bwd,
           w_proj_na, w_proj_ret, w_out, norm2_g, w_up, conv_w, conv_b, w_down, final_norm_g):
    b, n, d = x.shape
    assert w_ada.shape[0] == 1, "single layer: the per-layer parameter arrays are used as (1, N) rows"

    cc = jnp.concatenate([c, c_ctx[None], jnp.zeros((ADA_ROWS - b - 1, d), F32)], axis=0)
    mod = _ada(cc, w_ada[0], b_ada)

    w_in_b = w_in[0].astype(BF16)
    cos, s1, s2 = _rope_tables(n)
    sh1_w = _rowproj(mod, w_in_b)
    qkv, rqk, rv, rg, gates = _inproj(x, sh1_w, mod, norm1_g, w_in_b, cos, s1, s2)
    ckv, crk, crv = _inproj_ctx(ctx, mod, b, norm1_g, w_in_b)

    o_na = _na(qkv, ckv, _na_bias_table(na_rpb[0]))
    o_ret = _ret(ret_log_decay_fwd[0], ret_log_decay_bwd[0], rqk, rv, rg, crk, crv)

    x1 = _merge(x, o_na, o_ret, gates, mod, w_proj_na[0].astype(BF16), w_proj_ret[0].astype(BF16),
                w_out[0].astype(BF16))
    return _ffn(x1, mod, norm2_g, final_norm_g.reshape(1, d), w_up[0].astype(BF16),
                conv_w[0], conv_b, w_down[0].astype(BF16))
```

```python
import math

import numpy as np
import jax
import jax.numpy as jnp
from jax import lax
from jax.experimental import pallas as pl
from jax.experimental.pallas import tpu as pltpu

F32 = jnp.float32
BF16 = jnp.bfloat16

D_MODEL = 1024
GRID_W = 64
NA_HEADS = 8
NA_HEAD_DIM = 64
NA_WIN_H = 8
NA_WIN_W = 16
RET_HEADS = 4
RET_QK_DIM = 128
RET_V_DIM = 256
RET_CHUNK = 128
D_FF = int(math.ceil(8 * D_MODEL / 3 / 128)) * 128
CONV_W = 3
ROPE_BASE = 10000.0
EPS = 1e-6

NA_W = NA_HEADS * NA_HEAD_DIM
RET_QK_W = RET_HEADS * RET_QK_DIM
RET_V_W = RET_HEADS * RET_V_DIM
OFF_Q, OFF_K, OFF_V = 0, NA_W, 2 * NA_W
OFF_RQ = 3 * NA_W
OFF_RK = OFF_RQ + RET_QK_W
OFF_RV = OFF_RK + RET_QK_W
OFF_RG = OFF_RV + RET_V_W
OFF_GA = OFF_RG + RET_V_W
OFF_GB = OFF_GA + D_MODEL
IN_COLS = OFF_GB + D_MODEL

NA_SCALE = NA_HEAD_DIM ** -0.5
RK_SCALE = RET_QK_DIM ** -0.5
MASK_VALUE = -1e30

VMEM_LIMIT_BYTES = 56 * 1024 * 1024
LANES = 128
HALO = 8

TM_PROJ = 512
TM_MERGE = 1024
TM_FFN = 512
ADA_ROWS = 16
ADA_TN = 1536
FF_CHUNK = 256
FF_SLOTS = 3
DOWN_EVERY = 5
PROJ_CHUNK = 512
NA_BLOCK = 2
RET_BLOCK = 16


def _cparams(n_axes):
    return pltpu.CompilerParams(dimension_semantics=("arbitrary",) * n_axes,
                                vmem_limit_bytes=VMEM_LIMIT_BYTES)


def _resident(shape):
    nd = len(shape)
    return pl.BlockSpec(shape, lambda *_: (0,) * nd, pipeline_mode=pl.Buffered(1))


def _mod_spec(k, row=None):
    if row is None:
        return pl.BlockSpec((1, 1, D_MODEL), lambda bi, *_: (bi, 0, k))
    return pl.BlockSpec((1, 1, D_MODEL), lambda *_: (row, 0, k))


def _rms_modulate(x, gain, shift, scale):
    y = x * lax.rsqrt(jnp.mean(x * x, axis=-1, keepdims=True) + EPS)
    return (y * gain) * (1.0 + scale) + shift


def _ada_kernel(c_ref, w_ref, b_ref, o_ref):
    s = jax.nn.silu(c_ref[...])
    o_ref[:, 0, :] = jnp.dot(s.astype(BF16), w_ref[...].astype(BF16), preferred_element_type=F32) + b_ref[...]


def _ada(cc, w_ada, b_ada):
    n_out = w_ada.shape[1]
    return pl.pallas_call(
        _ada_kernel,
        grid=(n_out // ADA_TN,),
        in_specs=[pl.BlockSpec((ADA_ROWS, D_MODEL), lambda j: (0, 0)),
                  pl.BlockSpec((D_MODEL, ADA_TN), lambda j: (0, j)),
                  pl.BlockSpec((1, ADA_TN), lambda j: (0, j))],
        out_specs=pl.BlockSpec((ADA_ROWS, 1, ADA_TN), lambda j: (0, 0, j)),
        out_shape=jax.ShapeDtypeStruct((ADA_ROWS, 1, n_out), F32),
        compiler_params=_cparams(1),
        name="ada",
    )(cc, w_ada, b_ada)


def _rowproj_kernel(s_ref, w_ref, o_ref):
    o_ref[:, 0, :] = jnp.dot(s_ref[:, 0, :].astype(BF16), w_ref[...], preferred_element_type=F32)


def _rowproj(rows, w):
    n_out = w.shape[1]
    tn = n_out // 2
    return pl.pallas_call(
        _rowproj_kernel,
        grid=(n_out // tn,),
        in_specs=[pl.BlockSpec((ADA_ROWS, 1, D_MODEL), lambda j: (0, 0, 0)),
                  pl.BlockSpec((D_MODEL, tn), lambda j: (0, j))],
        out_specs=pl.BlockSpec((ADA_ROWS, 1, tn), lambda j: (0, 0, j)),
        out_shape=jax.ShapeDtypeStruct((ADA_ROWS, 1, n_out), F32),
        compiler_params=_cparams(1),
        name="rowproj",
    )(rows, w)


def _inproj_kernel(x_ref, shw_ref, sc_ref, g_ref, w_ref, cs_ref, s1_ref, s2_ref,
                   qkv_ref, rqk_ref, rv_ref, rg_ref, gate_ref):
    x = x_ref[0]
    xa = (x * (g_ref[...] * (1.0 + sc_ref[0]))).astype(BF16)
    inv_rms = lax.rsqrt(jnp.mean(x * x, axis=-1, keepdims=True) + EPS)

    def proj(c0):
        return (jnp.dot(xa, w_ref[:, c0:c0 + PROJ_CHUNK], preferred_element_type=F32) * inv_rms
                + shw_ref[0, :, c0:c0 + PROJ_CHUNK])

    for j in range(2 * D_MODEL // PROJ_CHUNK):
        r = proj(OFF_GA + j * PROJ_CHUNK)
        gate_ref[0, :, j * PROJ_CHUNK:(j + 1) * PROJ_CHUNK] = jax.nn.sigmoid(r).astype(BF16)
    for j in range(RET_V_W // PROJ_CHUNK):
        r = proj(OFF_RG + j * PROJ_CHUNK)
        rg_ref[0, :, j * PROJ_CHUNK:(j + 1) * PROJ_CHUNK] = jax.nn.silu(r).astype(BF16)

    cs, s1, s2 = cs_ref[...], s1_ref[...], s2_ref[...]
    quarter = RET_QK_DIM // 4
    for j in range(2 * RET_QK_W // PROJ_CHUNK):
        r = proj(OFF_RQ + j * PROJ_CHUNK)
        for hh in range(PROJ_CHUNK // RET_QK_DIM):
            t = r[:, hh * RET_QK_DIM:(hh + 1) * RET_QK_DIM]
            t = (t * cs + pltpu.roll(t, RET_QK_DIM - quarter, 1) * s1
                 + pltpu.roll(t, quarter, 1) * s2)
            if j * PROJ_CHUNK >= RET_QK_W:
                t = t * RK_SCALE
            c0 = j * PROJ_CHUNK + hh * RET_QK_DIM
            rqk_ref[0, :, c0:c0 + RET_QK_DIM] = t.astype(BF16)

    for j in range(3 * NA_W // PROJ_CHUNK):
        r = proj(OFF_Q + j * PROJ_CHUNK)
        if j * PROJ_CHUNK < NA_W:
            r = r * NA_SCALE
        qkv_ref[0, :, j * PROJ_CHUNK:(j + 1) * PROJ_CHUNK] = r.astype(BF16)
    for j in range(RET_V_W // PROJ_CHUNK):
        rv_ref[0, :, j * PROJ_CHUNK:(j + 1) * PROJ_CHUNK] = proj(OFF_RV + j * PROJ_CHUNK).astype(BF16)


def _inproj(x, shift_w, mod, gain, w_in, cs, s1, s2):
    b, t, _ = x.shape
    tm = TM_PROJ
    tok = lambda w: pl.BlockSpec((1, tm, w), lambda bi, i: (bi, i, 0))
    tab = pl.BlockSpec((tm, RET_QK_DIM), lambda bi, i: (i, 0))
    out_w = (3 * NA_W, 2 * RET_QK_W, RET_V_W, RET_V_W, 2 * D_MODEL)
    return pl.pallas_call(
        _inproj_kernel,
        grid=(b, t // tm),
        in_specs=[tok(D_MODEL),
                  pl.BlockSpec((1, 1, IN_COLS), lambda bi, i: (bi, 0, 0)),
                  _mod_spec(1),
                  _resident((1, D_MODEL)),
                  _resident((D_MODEL, IN_COLS)),
                  tab, tab, tab],
        out_specs=[tok(w) for w in out_w],
        out_shape=[jax.ShapeDtypeStruct((b, t, w), BF16) for w in out_w],
        compiler_params=_cparams(2),
        name="inproj",
    )(x, shift_w, mod, gain, w_in, cs, s1, s2)


def _inproj_ctx_kernel(x_ref, sh_ref, sc_ref, g_ref, w_ref, kv_ref, rk_ref, rv_ref):
    h = _rms_modulate(x_ref[...], g_ref[...], sh_ref[0], sc_ref[0]).astype(BF16)

    def proj(c0):
        return jnp.dot(h, w_ref[:, c0:c0 + PROJ_CHUNK], preferred_element_type=F32)

    for j in range(2 * NA_W // PROJ_CHUNK):
        kv_ref[:, j * PROJ_CHUNK:(j + 1) * PROJ_CHUNK] = proj(OFF_K + j * PROJ_CHUNK).astype(BF16)
    for j in range(RET_QK_W // PROJ_CHUNK):
        rk_ref[:, j * PROJ_CHUNK:(j + 1) * PROJ_CHUNK] = (
            proj(OFF_RK + j * PROJ_CHUNK) * RK_SCALE).astype(BF16)
    for j in range(RET_V_W // PROJ_CHUNK):
        rv_ref[:, j * PROJ_CHUNK:(j + 1) * PROJ_CHUNK] = proj(OFF_RV + j * PROJ_CHUNK).astype(BF16)


def _inproj_ctx(ctx, mod, mod_row, gain, w_in):
    b, t, _ = ctx.shape
    tm = TM_PROJ
    tok = lambda w: pl.BlockSpec((tm, w), lambda i: (i, 0))
    out_w = (2 * NA_W, RET_QK_W, RET_V_W)
    outs = pl.pallas_call(
        _inproj_ctx_kernel,
        grid=(b * t // tm,),
        in_specs=[tok(D_MODEL), _mod_spec(0, mod_row), _mod_spec(1, mod_row),
                  _resident((1, D_MODEL)), _resident((D_MODEL, IN_COLS))],
        out_specs=[tok(w) for w in out_w],
        out_shape=[jax.ShapeDtypeStruct((b * t, w), BF16) for w in out_w],
        compiler_params=_cparams(1),
        name="inproj_ctx",
    )(ctx.reshape(b * t, D_MODEL), mod, mod, gain, w_in)
    return [o.reshape(b, t, -1) for o in outs]


def _na_kernel(q_ref, k_ref, v_ref, kc_ref, vc_ref, bias_ref, o_ref, vext, vcext, s_scr, m_scr, p_scr, oc_scr):
    n_rows = q_ref.shape[1] // GRID_W
    n_blk = n_rows // NA_BLOCK
    win = NA_WIN_H * GRID_W
    width = s_scr.shape[2]
    pair = 2 * GRID_W
    first = lax.broadcasted_iota(jnp.int32, (GRID_W, LANES), 1) < NA_HEAD_DIM
    nt = (((1,), (1,)), ((), ()))
    kc = kc_ref[0]
    vext[:, :LANES] = v_ref[0]
    vext[:, LANES:] = jnp.ones((vext.shape[0], LANES), BF16)
    vcext[:, :LANES] = vc_ref[0]
    vcext[:, LANES:] = jnp.ones((vcext.shape[0], LANES), BF16)

    def window(r):
        rs = min(max(r - NA_WIN_H // 2, 0), n_rows - NA_WIN_H)
        return (slice(r * GRID_W, (r + 1) * GRID_W), slice(rs * GRID_W, rs * GRID_W + win),
                rs - r + (NA_WIN_H - 1))

    def score_stage(blk, slot):
        base = blk * NA_BLOCK
        stacked = []
        for k in range(NA_BLOCK):
            q = q_ref[0, window(base + k)[0], :]
            zero = jnp.zeros_like(q)
            stacked.append(jnp.concatenate([jnp.where(first, q, zero), jnp.where(first, zero, q)], axis=0))
        s_ctx = lax.dot_general(jnp.concatenate(stacked, axis=0), kc, nt, preferred_element_type=F32)
        s_scr[slot, :, win:] = s_ctx
        for k in range(NA_BLOCK):
            _, wrows, var = window(base + k)
            s_loc = lax.dot_general(stacked[k], k_ref[0, wrows, :], nt, preferred_element_type=F32)
            for hh in range(2):
                dst = slice(k * pair + hh * GRID_W, k * pair + (hh + 1) * GRID_W)
                sl = s_loc[hh * GRID_W:(hh + 1) * GRID_W] + bias_ref[hh, var]
                s_scr[slot, dst, :win] = sl
                mx = jnp.maximum(jnp.max(sl, axis=-1, keepdims=True),
                                 jnp.max(s_ctx[dst], axis=-1, keepdims=True))
                m_scr[slot, dst, :] = jnp.broadcast_to(mx, (GRID_W, LANES))

    def prob_stage(slot):
        for k in range(NA_BLOCK):
            dst = slice(k * pair, (k + 1) * pair)
            mx = m_scr[slot, dst, :]
            for j in range(width // LANES):
                cols = slice(j * LANES, (j + 1) * LANES)
                p_scr[slot, dst, cols] = jnp.exp(s_scr[slot, dst, cols] - mx).astype(BF16)

    def out_stage(blk, slot):
        oc_scr[...] = jnp.dot(p_scr[slot, :, win:], vcext[...], preferred_element_type=F32)
        for k in range(NA_BLOCK):
            rows, wrows, _ = window(blk * NA_BLOCK + k)
            dst = slice(k * pair, (k + 1) * pair)
            o = (jnp.dot(p_scr[slot, dst, :win], vext[wrows, :], preferred_element_type=F32)
                 + oc_scr[dst, :])
            num = jnp.where(first, o[:GRID_W, :LANES], o[GRID_W:, :LANES])
            den = jnp.where(first, o[:GRID_W, LANES:], o[GRID_W:, LANES:])
            o_ref[0, rows, :] = (num / den).astype(BF16)

    assert n_blk % 2 == 0 and n_blk >= 4
    score_stage(0, 0)
    score_stage(1, 1)
    prob_stage(0)

    for blk in range(0, n_blk - 2, 2):
        out_stage(blk, 0)
        prob_stage(1)
        score_stage(blk + 2, 0)
        out_stage(blk + 1, 1)
        prob_stage(0)
        score_stage(blk + 3, 1)
    out_stage(n_blk - 2, 0)
    prob_stage(1)
    out_stage(n_blk - 1, 1)


def _na(qkv, ckv, bias):
    b, n, _ = qkv.shape
    ctx = ckv.shape[1]
    pairs = NA_W // LANES
    win = NA_WIN_H * GRID_W
    stage_rows = NA_BLOCK * 2 * GRID_W
    blk = lambda t, off: pl.BlockSpec((1, t, LANES), lambda p, bi: (bi, 0, off + p))
    return pl.pallas_call(
        _na_kernel,
        grid=(pairs, b),
        in_specs=[blk(n, 0), blk(n, pairs), blk(n, 2 * pairs),
                  blk(ctx, 0), blk(ctx, pairs),
                  pl.BlockSpec((2, NA_WIN_H, GRID_W, win), lambda p, bi: (p, 0, 0, 0))],
        out_specs=pl.BlockSpec((1, n, LANES), lambda p, bi: (bi, 0, p)),
        out_shape=jax.ShapeDtypeStruct((b, n, NA_W), BF16),
        scratch_shapes=[pltpu.VMEM((n, 2 * LANES), BF16), pltpu.VMEM((ctx, 2 * LANES), BF16),
                        pltpu.VMEM((2, stage_rows, win + ctx), F32), pltpu.VMEM((2, stage_rows, LANES), F32),
                        pltpu.VMEM((2, stage_rows, win + ctx), BF16), pltpu.VMEM((stage_rows, 2 * LANES), F32)],
        compiler_params=_cparams(2),
        name="na",
    )(qkv, qkv, qkv, ckv, ckv, bias)


def _ret_kernel(lgf_ref, lgb_ref, q_ref, k_ref, v_ref, g_ref, ck_ref, cv_ref, o_ref,
                dmat, xi_f, xi_b, zeta_f, zeta_b, gl_f, gl_b, sb_scr, p_scr, o_acc):
    L = RET_CHUNK
    hd = pl.program_id(1)
    lgf = lgf_ref[hd]
    lgb = lgb_ref[hd]
    n_chunks = q_ref.shape[1] // L
    c_chunks = ck_ref.shape[1] // L

    li = lax.broadcasted_iota(jnp.int32, (L, L), 0).astype(F32)
    mi = lax.broadcasted_iota(jnp.int32, (L, L), 1).astype(F32)
    diff = li - mi
    dmat[...] = jnp.where(diff >= 0.0, jnp.exp(lgf * jnp.maximum(diff, 0.0)),
                          jnp.exp(lgb * jnp.maximum(-diff, 0.0)))
    zeta_f[...] = jnp.exp(lgf * (L - 1.0 - li))
    zeta_b[...] = jnp.exp(lgb * li)
    lv = lax.broadcasted_iota(jnp.int32, (L, RET_V_DIM), 0).astype(F32)
    xi_f[...] = jnp.exp(lgf * (lv + 1.0))
    xi_b[...] = jnp.exp(lgb * (L - lv))
    gl_f[...] = jnp.exp(jnp.full((L, RET_V_DIM), L, F32) * lgf)
    gl_b[...] = jnp.exp(jnp.full((L, RET_V_DIM), L, F32) * lgb)

    tn = (((0,), (0,)), ((), ()))
    nt = (((1,), (1,)), ((), ()))

    def advance(state, gl, zeta, k, v):
        kz = (k.astype(F32) * zeta[...]).astype(BF16)
        return gl[...] * state + lax.dot_general(kz, v, tn, preferred_element_type=F32)

    sf = jnp.zeros((RET_QK_DIM, RET_V_DIM), F32)
    sb = jnp.zeros((RET_QK_DIM, RET_V_DIM), F32)
    for c in range(c_chunks):
        sf = advance(sf, gl_f, zeta_f, ck_ref[0, c * L:(c + 1) * L, :], cv_ref[0, c * L:(c + 1) * L, :])
    for c in reversed(range(c_chunks)):
        sb = advance(sb, gl_b, zeta_b, ck_ref[0, c * L:(c + 1) * L, :], cv_ref[0, c * L:(c + 1) * L, :])

    chunk = lambda c: slice(c * L, (c + 1) * L)

    for cb in reversed(range(n_chunks)):
        sb_scr[cb] = sb.astype(BF16)
        sb = advance(sb, gl_b, zeta_b, k_ref[0, chunk(cb), :], v_ref[0, chunk(cb), :])

    def probs(c):
        rows = chunk(c)
        a = lax.dot_general(q_ref[0, rows, :], k_ref[0, rows, :], nt, preferred_element_type=F32)
        p_scr[rows, :] = (a * dmat[...]).astype(BF16)

    def outs(c, sf):
        rows = chunk(c)
        q = q_ref[0, rows, :]
        v = v_ref[0, rows, :]
        o_acc[rows, :] = (jnp.dot(p_scr[rows, :], v, preferred_element_type=F32)
                          + jnp.dot(q, sf.astype(BF16), preferred_element_type=F32) * xi_f[...]
                          + jnp.dot(q, sb_scr[c], preferred_element_type=F32) * xi_b[...])
        return advance(sf, gl_f, zeta_f, k_ref[0, rows, :], v)

    def norm(c):
        rows = chunk(c)
        o = o_acc[rows, :]
        mu = jnp.mean(o, axis=-1, keepdims=True)
        d = o - mu
        var = jnp.mean(d * d, axis=-1, keepdims=True)
        o_ref[0, rows, :] = (d * lax.rsqrt(var + EPS) * g_ref[0, rows, :].astype(F32)).astype(BF16)

    n_blk = n_chunks // RET_BLOCK
    block = lambda t: [t * RET_BLOCK + i for i in range(RET_BLOCK)]

    def outs_block(t, sf):
        for c in block(t):
            sf = outs(c, sf)
        return sf

    for c in block(0):
        probs(c)
    sf = outs_block(0, sf)
    for c in block(1):
        probs(c)

    for t in range(2, n_blk):
        for c in block(t - 2):
            norm(c)
        sf = outs_block(t - 1, sf)
        for c in block(t):
            probs(c)
    for c in block(n_blk - 2):
        norm(c)
    outs_block(n_blk - 1, sf)
    for c in block(n_blk - 1):
        norm(c)


def _ret(lgf, lgb, rqk, rv, rg, crk, crv):
    b, n, _ = rqk.shape
    ctx = crk.shape[1]
    L = RET_CHUNK
    smem = pl.BlockSpec(memory_space=pltpu.SMEM)
    qk = lambda t, off: pl.BlockSpec((1, t, RET_QK_DIM), lambda bi, h: (bi, 0, off + h))
    vv = lambda t: pl.BlockSpec((1, t, RET_V_DIM), lambda bi, h: (bi, 0, h))
    vm = lambda r, c: pltpu.VMEM((r, c), F32)
    st = pltpu.VMEM((n // L, RET_QK_DIM, RET_V_DIM), BF16)
    return pl.pallas_call(
        _ret_kernel,
        grid=(b, RET_HEADS),
        in_specs=[smem, smem, qk(n, 0), qk(n, RET_HEADS), vv(n), vv(n), qk(ctx, 0), vv(ctx)],
        out_specs=vv(n),
        out_shape=jax.ShapeDtypeStruct((b, n, RET_V_W), BF16),
        scratch_shapes=[vm(L, L), vm(L, RET_V_DIM), vm(L, RET_V_DIM), vm(L, L), vm(L, L),
                        vm(L, RET_V_DIM), vm(L, RET_V_DIM), st,
                        pltpu.VMEM((n, L), BF16), vm(n, RET_V_DIM)],
        compiler_params=_cparams(2),
        name="ret",
    )(lgf, lgb, rqk, rqk, rv, rg, crk, crv)


def _merge_kernel(x_ref, ona_ref, oret_ref, gate_ref, g1_ref, wna_ref, wret_ref, wout_ref, o_ref):
    a = jnp.dot(ona_ref[0], wna_ref[...], preferred_element_type=F32)
    r = jnp.dot(oret_ref[0], wret_ref[...], preferred_element_type=F32)
    merged = (gate_ref[0, :, :D_MODEL].astype(F32) * a
              + gate_ref[0, :, D_MODEL:].astype(F32) * r).astype(BF16)
    y = jnp.dot(merged, wout_ref[...], preferred_element_type=F32)
    o_ref[0] = x_ref[0] + g1_ref[0] * y


def _merge(x, o_na, o_ret, gates, mod, w_na, w_ret, w_out):
    b, n, _ = x.shape
    tm = TM_MERGE
    tok = lambda w: pl.BlockSpec((1, tm, w), lambda bi, i: (bi, i, 0))
    return pl.pallas_call(
        _merge_kernel,
        grid=(b, n // tm),
        in_specs=[tok(D_MODEL), tok(NA_W), tok(RET_V_W), tok(2 * D_MODEL), _mod_spec(2),
                  _resident((NA_W, D_MODEL)), _resident((RET_V_W, D_MODEL)),
                  _resident((D_MODEL, D_MODEL))],
        out_specs=tok(D_MODEL),
        out_shape=jax.ShapeDtypeStruct((b, n, D_MODEL), F32),
        compiler_params=_cparams(2),
        name="merge",
    )(x, o_na, o_ret, gates, mod, w_na, w_ret, w_out)


def _ffn_kernel(x_ref, xp_ref, xn_ref, sh_ref, sc_ref, g2_ref, gn_ref, gf_ref,
                wup_ref, cw_ref, cb_ref, wdown_ref, o_ref, h_scr, u_scr, y_scr):
    i = pl.program_id(1)
    last = pl.num_programs(1) - 1
    tm = x_ref.shape[1]
    rows = tm + 2 * HALO
    n_ch = D_FF // FF_CHUNK
    x = x_ref[0]
    xa = jnp.concatenate([xp_ref[0], x, xn_ref[0]], axis=0)
    h = _rms_modulate(xa, gn_ref[...], sh_ref[0], sc_ref[0])
    ridx = lax.broadcasted_iota(jnp.int32, (rows, 1), 0)
    valid = jnp.logical_and(jnp.logical_or(ridx >= HALO, i > 0),
                            jnp.logical_or(ridx < tm + HALO, i < last))
    h_scr[...] = jnp.where(valid, h, 0.0).astype(BF16)

    def up(j, slot):
        for g in range(2):
            c0 = g * D_FF + j * FF_CHUNK
            u_scr[slot, g] = jnp.dot(h_scr[...], wup_ref[:, c0:c0 + FF_CHUNK], preferred_element_type=F32)

    def act(j, slot):
        def conv(g):
            c0 = g * D_FF + j * FF_CHUNK
            u = u_scr[slot, g]
            w = cw_ref[:, c0:c0 + FF_CHUNK]
            y = (pltpu.roll(u, 1, 0) * w[0:1] + u * w[1:2] + pltpu.roll(u, rows - 1, 0) * w[2:3]
                 + cb_ref[:, c0:c0 + FF_CHUNK])
            return y[HALO:HALO + tm]

        y_scr[:, j * FF_CHUNK:(j + 1) * FF_CHUNK] = (jax.nn.silu(conv(0)) * conv(1)).astype(BF16)

    def down(j0, j1):
        c0, c1 = j0 * FF_CHUNK, j1 * FF_CHUNK
        return jnp.dot(y_scr[:, c0:c1], wdown_ref[c0:c1, :], preferred_element_type=F32)

    slots = u_scr.shape[0]
    ahead = slots - 1
    for j in range(ahead):
        up(j, j)
    acc = None
    for j in range(n_ch):
        act(j, j % slots)
        if j + ahead < n_ch:
            up(j + ahead, (j + ahead) % slots)
        if j % DOWN_EVERY == 0 and j > 0:
            part = down(j - DOWN_EVERY, j)
            acc = part if acc is None else acc + part
    j0 = (n_ch - 1) // DOWN_EVERY * DOWN_EVERY
    acc = acc + down(j0, n_ch)
    z = x + g2_ref[0] * acc
    o_ref[0] = z * lax.rsqrt(jnp.mean(z * z, axis=-1, keepdims=True) + EPS) * gf_ref[...]


def _ffn(x, mod, gain2, gain_f, w_up, conv_w, conv_b, w_down):
    b, n, _ = x.shape
    tm = TM_FFN
    per = tm // HALO
    n_halo = n // HALO
    rows = tm + 2 * HALO
    tok = pl.BlockSpec((1, tm, D_MODEL), lambda bi, i: (bi, i, 0))
    prev = pl.BlockSpec((1, HALO, D_MODEL), lambda bi, i: (bi, jnp.maximum(i * per - 1, 0), 0))
    nxt = pl.BlockSpec((1, HALO, D_MODEL), lambda bi, i: (bi, jnp.minimum((i + 1) * per, n_halo - 1), 0))
    return pl.pallas_call(
        _ffn_kernel,
        grid=(b, n // tm),
        in_specs=[tok, prev, nxt, _mod_spec(3), _mod_spec(4), _mod_spec(5),
                  _resident((1, D_MODEL)), _resident((1, D_MODEL)),
                  _resident(w_up.shape), _resident(conv_w.shape),
                  _resident(conv_b.shape), _resident((D_FF, D_MODEL))],
        out_specs=tok,
        out_shape=jax.ShapeDtypeStruct((b, n, D_MODEL), F32),
        scratch_shapes=[pltpu.VMEM((rows, D_MODEL), BF16),
                        pltpu.VMEM((FF_SLOTS, 2, rows, FF_CHUNK), F32),
                        pltpu.VMEM((tm, D_FF), BF16)],
        compiler_params=_cparams(2),
        name="ffn",
    )(x, x, x, mod, mod, mod, gain2, gain_f, w_up, conv_w, conv_b, w_down)


def _rope_tables(n):
    nf = RET_QK_DIM // 4
    inv = (ROPE_BASE ** (-np.arange(nf, dtype=np.float32) / nf)).astype(np.float32)
    t = np.arange(n)
    ang_r = (t // GRID_W).astype(np.float32)[:, None] * inv[None, :]
    ang_c = (t % GRID_W).astype(np.float32)[:, None] * inv[None, :]
    cos = np.concatenate([np.cos(ang_r)] * 2 + [np.cos(ang_c)] * 2, axis=-1)
    zero = np.zeros((n, nf), np.float32)
    s1 = np.concatenate([-np.sin(ang_r), zero, -np.sin(ang_c), zero], axis=-1)
    s2 = np.concatenate([zero, np.sin(ang_r), zero, np.sin(ang_c)], axis=-1)
    return tuple(jnp.asarray(a.astype(np.float32)) for a in (cos, s1, s2))


def _na_bias_table(rpb):
    w, kh, kw = GRID_W, NA_WIN_H, NA_WIN_W
    cols = np.arange(w)
    c_start = np.clip(cols - kw // 2, 0, w - kw)
    kcol = np.arange(w)
    valid = (kcol[None, :] >= c_start[:, None]) & (kcol[None, :] < c_start[:, None] + kw)
    dc = kcol[None, :] - cols[:, None] + (kw - 1)
    onehot = ((dc[:, :, None] == np.arange(2 * kw - 1)) & valid[:, :, None]).astype(np.float32)
    base = jnp.einsum('hrd,ckd->hcrk', rpb, onehot, precision=lax.Precision.HIGHEST)
    base = base + np.where(valid, 0.0, MASK_VALUE).astype(np.float32)[:, None, :]
    flat = base.reshape(rpb.shape[0], w, (2 * kh - 1) * w)
    return jnp.stack([flat[:, :, v * w:(v + kh) * w] for v in range(kh)], axis=1)


def kernel(x, c, ctx, c_ctx, w_ada, b_ada, norm1_g, w_in, na_rpb, ret_log_decay_fwd, ret_log_decay_bwd,
           w_proj_na, w_proj_ret, w_out, norm2_g, w_up, conv_w, conv_b, w_down, final_norm_g):
    b, n, d = x.shape
    assert w_ada.shape[0] == 1, "single layer: the per-layer parameter arrays are used as (1, N) rows"

    cc = jnp.concatenate([c, c_ctx[None], jnp.zeros((ADA_ROWS - b - 1, d), F32)], axis=0)
    mod = _ada(cc, w_ada[0], b_ada)

    w_in_b = w_in[0].astype(BF16)
    cos, s1, s2 = _rope_tables(n)
    sh1_w = _rowproj(mod, w_in_b)
    qkv, rqk, rv, rg, gates = _inproj(x, sh1_w, mod, norm1_g, w_in_b, cos, s1, s2)
    ckv, crk, crv = _inproj_ctx(ctx, mod, b, norm1_g, w_in_b)

    o_na = _na(qkv, ckv, _na_bias_table(na_rpb[0]))
    o_ret = _ret(ret_log_decay_fwd[0], ret_log_decay_bwd[0], rqk, rv, rg, crk, crv)

    x1 = _merge(x, o_na, o_ret, gates, mod, w_proj_na[0].astype(BF16), w_proj_ret[0].astype(BF16),
                w_out[0].astype(BF16))
    return _ffn(x1, mod, norm2_g, final_norm_g.reshape(1, d), w_up[0].astype(BF16),
                conv_w[0], conv_b, w_down[0].astype(BF16))
```
